```python
import jax
import jax.numpy as jnp
from jax import lax
import numpy as np

D_MODEL = 2048
BATCH = 4
SEQ = 2048
DEPTH = 2
DEC_BATCH = 32
DEC_SEQ = 4
PAST_LEN = 8192
PAGE_SIZE = 128

A_HEADS = 4
A_HD = 128
A_WIDTH = A_HEADS * A_HD
IDX_HEADS = 16
IDX_HD = 64
A_TOPK_MAX = 256
A_Q_BLOCK = 128
B_HEADS = 4
B_HD = 128
B_WIDTH = B_HEADS * B_HD
MOBA_BLOCK = 256
MOBA_TOPK = 3
MOBA_Q_BLOCK = 64
C_HD = 64
C_HEADS = 16
C_WIDTH = C_HEADS * C_HD
W_LORA = 64
A_LORA = 64
G_LORA = 160
MIX_WIDTH = A_WIDTH + B_WIDTH + C_WIDTH
A_COLS = 3 * A_WIDTH + IDX_HEADS * IDX_HD + IDX_HD + IDX_HEADS
B_COLS = 3 * B_WIDTH
C_COLS = 3 * C_WIDTH + W_LORA + A_LORA + G_LORA
P_IN = A_COLS + B_COLS + C_COLS
D_FF = 4 * D_MODEL
ROPE_THETA = 500000.0
NORM_EPS = 1e-6
GN_EPS = 64e-5
IDX_W_SCALE = (IDX_HEADS * IDX_HD) ** -0.5

kernel_name = 'hymba_dsa_moba_rwkv7_step'


def rms_norm(x, g):
    xf = x.astype(jnp.float32)
    y = xf * lax.rsqrt(jnp.mean(xf * xf, axis=-1, keepdims=True) + NORM_EPS)
    return (y * g.astype(jnp.float32)).astype(x.dtype)


def rope_partial(x, pos):
    hd = x.shape[-1]
    rot = hd // 4
    half = rot // 2
    inv = ROPE_THETA ** (-2.0 * jnp.arange(half, dtype=jnp.float32) / rot)
    ang = pos.astype(jnp.float32)[:, None] * inv[None, :]
    cos = jnp.cos(ang)[:, None, :]
    sin = jnp.sin(ang)[:, None, :]
    xr = x[..., :rot].astype(jnp.float32)
    x1, x2 = xr[..., :half], xr[..., half:]
    xr = jnp.concatenate([x1 * cos - x2 * sin, x2 * cos + x1 * sin], axis=-1)
    return jnp.concatenate([xr.astype(x.dtype), x[..., rot:]], axis=-1)


def split_cols(p, sizes):
    outs, o = [], 0
    for s in sizes:
        outs.append(p[..., o:o + s])
        o += s
    return outs


def in_proj(h, w_in):
    p = jnp.einsum('bsd,dp->bsp', h, w_in)
    return split_cols(p, (A_COLS, B_COLS, C_COLS))


def out_proj(oa, ob, oc, w_o):
    b, s = oc.shape[:2]
    o = jnp.concatenate([oa.reshape(b, s, A_WIDTH), ob.reshape(b, s, B_WIDTH), oc], axis=-1)
    return jnp.einsum('bsm,md->bsd', o, w_o)


def mlp(h, w1, w2):
    return jnp.einsum('bsf,fd->bsd', jnp.square(jax.nn.relu(jnp.einsum('bsd,df->bsf', h, w1))), w2)


def prep_a(pa, pos, q_g, k_g, ik_g):
    b, s = pa.shape[:2]
    q, k, v, qi, ki, wi = split_cols(pa, (A_WIDTH, A_WIDTH, A_WIDTH, IDX_HEADS * IDX_HD, IDX_HD, IDX_HEADS))
    q = rope_partial(rms_norm(q.reshape(b, s, A_HEADS, A_HD), q_g), pos)
    k = rope_partial(rms_norm(k.reshape(b, s, A_HEADS, A_HD), k_g), pos)
    v = v.reshape(b, s, A_HEADS, A_HD)
    qi = rope_partial(qi.reshape(b, s, IDX_HEADS, IDX_HD), pos)
    ki = rope_partial(rms_norm(ki, ik_g)[:, :, None, :], pos)[:, :, 0, :]
    return q, k, v, qi, ki, wi * IDX_W_SCALE


def index_scores(qi, wi, ki):
    s = jnp.einsum('bqhd,bld->bqhl', qi.astype(jnp.float32), ki.astype(jnp.float32))
    return jnp.einsum('bqhl,bqh->bql', jax.nn.relu(s), wi.astype(jnp.float32))


def dsa_attend(q, kg, vg, valid):
    s = jnp.einsum('bqhd,bqkhd->bqhk', q.astype(jnp.float32), kg.astype(jnp.float32)) * (A_HD ** -0.5)
    s = jnp.where(valid[:, :, None, :], s, -jnp.inf)
    p = jax.nn.softmax(s, axis=-1)
    return jnp.einsum('bqhk,bqkhd->bqhd', p, vg.astype(jnp.float32)).astype(q.dtype)


def dsa_prompt(q, k, v, qi, ki, wi):
    b, s = q.shape[:2]
    topk = min(A_TOPK_MAX, s // 4)
    kpos = jnp.arange(s)
    bidx = jnp.arange(b)[:, None, None]

    def block(i):
        s0 = i * A_Q_BLOCK
        qpos = s0 + jnp.arange(A_Q_BLOCK)
        qb = lax.dynamic_slice_in_dim(q, s0, A_Q_BLOCK, 1)
        qib = lax.dynamic_slice_in_dim(qi, s0, A_Q_BLOCK, 1)
        wib = lax.dynamic_slice_in_dim(wi, s0, A_Q_BLOCK, 1)
        isc = index_scores(qib, wib, ki)
        isc = jnp.where(kpos[None, None, :] <= qpos[None, :, None], isc, -jnp.inf)
        _, sel = lax.top_k(isc, topk)
        return dsa_attend(qb, k[bidx, sel], v[bidx, sel], sel <= qpos[None, :, None])

    out = lax.map(block, jnp.arange(s // A_Q_BLOCK))
    return jnp.moveaxis(out, 0, 1).reshape(q.shape)


def dsa_sample(q, k, v, qi, ki, wi, cache_k, cache_v, cache_idx, page_table):
    db, t = q.shape[:2]
    n_pages = PAST_LEN // PAGE_SIZE
    L = PAST_LEN + t
    topk = min(A_TOPK_MAX, L // 4)
    ki_past = cache_idx[page_table].reshape(db, PAST_LEN, IDX_HD)
    ki_all = jnp.concatenate([ki_past, ki], axis=1)
    qpos = PAST_LEN + jnp.arange(t)
    isc = index_scores(qi, wi, ki_all)
    isc = jnp.where(jnp.arange(L)[None, None, :] <= qpos[None, :, None], isc, -jnp.inf)
    _, sel = lax.top_k(isc, topk)
    bidx = jnp.arange(db)[:, None, None]
    phys = page_table[bidx, jnp.minimum(sel // PAGE_SIZE, n_pages - 1)]
    off = sel % PAGE_SIZE
    from_past = (sel < PAST_LEN)[..., None, None]
    new_i = jnp.clip(sel - PAST_LEN, 0, t - 1)
    kg = jnp.where(from_past, cache_k[phys, off], k[bidx, new_i])
    vg = jnp.where(from_past, cache_v[phys, off], v[bidx, new_i])
    return dsa_attend(q, kg, vg, sel <= qpos[None, :, None])


def prep_b(pb, pos, q_g, k_g):
    b, s = pb.shape[:2]
    q, k, v = split_cols(pb, (B_WIDTH, B_WIDTH, B_WIDTH))
    q = rope_partial(rms_norm(q.reshape(b, s, B_HEADS, B_HD), q_g), pos)
    k = rope_partial(rms_norm(k.reshape(b, s, B_HEADS, B_HD), k_g), pos)
    return q, k, v.reshape(b, s, B_HEADS, B_HD)


def two_part_softmax(q, k1, v1, m1, k2, v2, m2):
    scale = B_HD ** -0.5
    qf = q.astype(jnp.float32)
    s1 = jnp.einsum('bqhd,bqhkd->bqhk', qf, k1.astype(jnp.float32)) * scale
    s2 = jnp.einsum('bqhd,bkhd->bqhk', qf, k2.astype(jnp.float32)) * scale
    s = jnp.concatenate([jnp.where(m1, s1, -jnp.inf), jnp.where(m2, s2, -jnp.inf)], axis=-1)
    p = jax.nn.softmax(s, axis=-1)
    n1 = k1.shape[3]
    o = (jnp.einsum('bqhk,bqhkd->bqhd', p[..., :n1], v1.astype(jnp.float32))
         + jnp.einsum('bqhk,bkhd->bqhd', p[..., n1:], v2.astype(jnp.float32)))
    return o.astype(q.dtype)


def moba_prompt(q, k, v):
    b, s, h, hd = q.shape
    nb = -(-s // MOBA_BLOCK)
    pad = nb * MOBA_BLOCK - s
    kp = jnp.pad(k, ((0, 0), (0, pad), (0, 0), (0, 0)))
    vp = jnp.pad(v, ((0, 0), (0, pad), (0, 0), (0, 0)))
    kblk = kp.reshape(b, nb, MOBA_BLOCK, h, hd)
    vblk = vp.reshape(b, nb, MOBA_BLOCK, h, hd)
    kmean = jnp.mean(kblk.astype(jnp.float32), axis=2)
    ksel = max(1, min(MOBA_TOPK, nb - 1))
    bidx = jnp.arange(b)[:, None, None, None, None]
    hidx = jnp.arange(h)[None, None, :, None, None]
    ridx = jnp.arange(MOBA_BLOCK)

    def block(i):
        s0 = i * MOBA_Q_BLOCK
        qb = lax.dynamic_slice_in_dim(q, s0, MOBA_Q_BLOCK, 1)
        qpos = s0 + jnp.arange(MOBA_Q_BLOCK)
        own = s0 // MOBA_BLOCK
        g = jnp.einsum('bqhd,bnhd->bqhn', qb.astype(jnp.float32), kmean)
        g = jnp.where(jnp.arange(nb) < own, g, -jnp.inf)
        _, sel = lax.top_k(g, ksel)
        m1 = jnp.repeat(sel < own, MOBA_BLOCK, axis=-1)
        kg = kblk[bidx, sel[..., None], ridx, hidx].reshape(b, MOBA_Q_BLOCK, h, ksel * MOBA_BLOCK, hd)
        vg = vblk[bidx, sel[..., None], ridx, hidx].reshape(b, MOBA_Q_BLOCK, h, ksel * MOBA_BLOCK, hd)
        kown = lax.dynamic_slice_in_dim(kp, own * MOBA_BLOCK, MOBA_BLOCK, 1)
        vown = lax.dynamic_slice_in_dim(vp, own * MOBA_BLOCK, MOBA_BLOCK, 1)
        kpos = own * MOBA_BLOCK + ridx
        m2 = (kpos[None, :] <= qpos[:, None])[None, :, None, :]
        return two_part_softmax(qb, kg, vg, m1, kown, vown, m2)

    out = lax.map(block, jnp.arange(s // MOBA_Q_BLOCK))
    return jnp.moveaxis(out, 0, 1).reshape(b, s, h, hd)


def moba_sample(q, k, v, cache_k, cache_v, page_table):
    db, t, h, hd = q.shape
    ppb = MOBA_BLOCK // PAGE_SIZE
    n_pages = PAST_LEN // PAGE_SIZE
    nbp = PAST_LEN // MOBA_BLOCK
    own_past = PAST_LEN - nbp * MOBA_BLOCK
    if nbp > 0:
        page_mean = jnp.mean(cache_k.astype(jnp.float32), axis=1)[page_table[:, :nbp * ppb]]
        bmean = jnp.mean(page_mean.reshape(db, nbp, ppb, h, hd), axis=2)
    else:
        bmean = jnp.zeros((db, 1, h, hd), jnp.float32)
    g = jnp.einsum('bqhd,bnhd->bqhn', q.astype(jnp.float32), bmean)
    g = jnp.where(jnp.arange(bmean.shape[1]) < nbp, g, -jnp.inf)
    ksel = max(1, min(MOBA_TOPK, nbp))
    _, sel = lax.top_k(g, ksel)
    m1 = jnp.repeat(sel < nbp, MOBA_BLOCK, axis=-1)
    lpage = jnp.minimum(sel[..., None] * ppb + jnp.arange(ppb), n_pages - 1)
    phys = page_table[jnp.arange(db)[:, None, None, None, None], lpage]
    ridx = jnp.arange(PAGE_SIZE)
    hidx = jnp.arange(h)[None, None, :, None, None, None]
    kg = cache_k[phys[..., None], ridx, hidx].reshape(db, t, h, ksel * MOBA_BLOCK, hd)
    vg = cache_v[phys[..., None], ridx, hidx].reshape(db, t, h, ksel * MOBA_BLOCK, hd)
    if own_past > 0:
        op = page_table[:, nbp * ppb: nbp * ppb + own_past // PAGE_SIZE]
        kown = jnp.concatenate([cache_k[op].reshape(db, own_past, h, hd), k], axis=1)
        vown = jnp.concatenate([cache_v[op].reshape(db, own_past, h, hd), v], axis=1)
    else:
        kown, vown = k, v
    qpos = PAST_LEN + jnp.arange(t)
    kpos = nbp * MOBA_BLOCK + jnp.arange(own_past + t)
    m2 = (kpos[None, :] <= qpos[:, None])[None, :, None, :]
    return two_part_softmax(q, kg, vg, m1, kown, vown, m2)


def wkv7_scan(r, decay, k, v, a_vec, b_vec, s0):
    def step(s, inp):
        rt, dt, kt, vt, at, bt = inp
        sa = jnp.einsum('bhij,bhj->bhi', s, at)
        s = s * dt[:, :, None, :] + sa[..., None] * bt[:, :, None, :] + vt[..., None] * kt[:, :, None, :]
        return s, jnp.einsum('bhij,bhj->bhi', s, rt)
    xs = tuple(jnp.moveaxis(z, 1, 0) for z in (r, decay, k, v, a_vec, b_vec))
    s_fin, ys = lax.scan(step, s0, xs)
    return jnp.moveaxis(ys, 0, 1), s_fin


def rwkv_mix(pc, prev, s0, mu, w0, w2, a0, a2, g2, k_k, k_a, r_k, ln_w, ln_b):
    b, s = pc.shape[:2]
    pcf = pc.astype(jnp.float32)
    shifted = jnp.concatenate([prev.astype(jnp.float32)[:, None], pcf[:, :-1]], axis=1)
    xs = pcf + mu * (shifted - pcf)
    r, k, v, wl, al, gl = split_cols(xs, (C_WIDTH, C_WIDTH, C_WIDTH, W_LORA, A_LORA, G_LORA))
    w = -jax.nn.softplus(-(w0 + jnp.tanh(wl) @ w2)) - 0.5
    a = jax.nn.sigmoid(a0 + al @ a2)
    g = jax.nn.sigmoid(gl) @ g2

    def heads(z):
        return z.reshape(b, s, C_HEADS, C_HD)

    kk = heads(k * k_k)
    kk = kk / jnp.maximum(jnp.sqrt(jnp.sum(kk * kk, axis=-1, keepdims=True)), 1e-12)
    k = k * (1.0 + (a - 1.0) * k_a)
    rh, kh, vh, ah = heads(r), heads(k), heads(v), heads(a)
    y, s_fin = wkv7_scan(rh, jnp.exp(-jnp.exp(heads(w))), kh, vh, -kk, kk * ah, s0.astype(jnp.float32))
    mean = jnp.mean(y, axis=-1, keepdims=True)
    var = jnp.mean(jnp.square(y - mean), axis=-1, keepdims=True)
    yn = ((y - mean) * lax.rsqrt(var + GN_EPS)).reshape(b, s, C_WIDTH) * ln_w + ln_b
    bonus = (jnp.sum(rh * kh * r_k, axis=-1, keepdims=True) * vh).reshape(b, s, C_WIDTH)
    return ((yn + bonus) * g).astype(pc.dtype), s_fin, pc[:, -1]


def setup_inputs(seed: int = 0) -> dict:
    key = jax.random.key(seed)
    ks = jax.random.split(key, 40)
    f32 = jnp.float32

    def nrm(i, shape, scale):
        return jax.random.normal(ks[i], shape, f32) * scale

    n_pages = PAST_LEN // PAGE_SIZE
    n_pool = (DEC_BATCH * n_pages * 5) // 4
    page_table = jax.random.permutation(ks[9], n_pool)[:DEC_BATCH * n_pages].reshape(DEC_BATCH, n_pages).astype(jnp.int32)
    return {
        'x_prompt': nrm(0, (BATCH, SEQ, D_MODEL), 1.0),
        'x_sample': nrm(1, (DEC_BATCH, DEC_SEQ, D_MODEL), 1.0),
        'cache_a_k': nrm(2, (DEPTH, n_pool, PAGE_SIZE, A_HEADS, A_HD), 1.0),
        'cache_a_v': nrm(3, (DEPTH, n_pool, PAGE_SIZE, A_HEADS, A_HD), 1.0),
        'cache_a_idx': nrm(4, (DEPTH, n_pool, PAGE_SIZE, IDX_HD), 1.0),
        'cache_b_k': nrm(5, (DEPTH, n_pool, PAGE_SIZE, B_HEADS, B_HD), 1.0),
        'cache_b_v': nrm(6, (DEPTH, n_pool, PAGE_SIZE, B_HEADS, B_HD), 1.0),
        'state_wkv': nrm(7, (DEPTH, DEC_BATCH, C_HEADS, C_HD, C_HD), 0.1),
        'state_shift': nrm(8, (DEPTH, DEC_BATCH, C_COLS), 1.0),
        'page_table': page_table,
        'norm1_g': 1.0 + nrm(10, (DEPTH, D_MODEL), 0.05),
        'w_in': nrm(11, (DEPTH, D_MODEL, P_IN), D_MODEL ** -0.5),
        'a_q_norm': 1.0 + nrm(12, (DEPTH, A_HD), 0.05),
        'a_k_norm': 1.0 + nrm(13, (DEPTH, A_HD), 0.05),
        'idx_k_norm': 1.0 + nrm(14, (DEPTH, IDX_HD), 0.05),
        'b_q_norm': 1.0 + nrm(15, (DEPTH, B_HD), 0.05),
        'b_k_norm': 1.0 + nrm(16, (DEPTH, B_HD), 0.05),
        'c_mu': jax.random.uniform(ks[17], (DEPTH, C_COLS), f32),
        'c_w0': nrm(18, (DEPTH, C_WIDTH), 0.5),
        'c_w2': nrm(19, (DEPTH, W_LORA, C_WIDTH), 0.5 * W_LORA ** -0.5),
        'c_a0': nrm(20, (DEPTH, C_WIDTH), 0.2),
        'c_a2': nrm(21, (DEPTH, A_LORA, C_WIDTH), A_LORA ** -0.5),
        'c_g2': nrm(22, (DEPTH, G_LORA, C_WIDTH), G_LORA ** -0.5),
        'c_k_k': 0.85 + nrm(23, (DEPTH, C_WIDTH), 0.05),
        'c_k_a': 1.0 + nrm(24, (DEPTH, C_WIDTH), 0.05),
        'c_r_k': nrm(25, (DEPTH, C_HEADS, C_HD), 0.1),
        'c_ln_w': 1.0 + nrm(26, (DEPTH, C_WIDTH), 0.05),
        'c_ln_b': nrm(27, (DEPTH, C_WIDTH), 0.02),
        'w_o': nrm(28, (DEPTH, MIX_WIDTH, D_MODEL), MIX_WIDTH ** -0.5),
        'norm2_g': 1.0 + nrm(29, (DEPTH, D_MODEL), 0.05),
        'w_ff1': nrm(30, (DEPTH, D_MODEL, D_FF), D_MODEL ** -0.5),
        'w_ff2': nrm(31, (DEPTH, D_FF, D_MODEL), 0.5 * D_FF ** -0.5),
    }


def reference(x_prompt, x_sample, cache_a_k, cache_a_v, cache_a_idx, cache_b_k, cache_b_v,
              state_wkv, state_shift, page_table,
              norm1_g, w_in, a_q_norm, a_k_norm, idx_k_norm, b_q_norm, b_k_norm,
              c_mu, c_w0, c_w2, c_a0, c_a2, c_g2, c_k_k, c_k_a, c_r_k, c_ln_w, c_ln_b,
              w_o, norm2_g, w_ff1, w_ff2):
    bp = x_prompt.shape[0]
    pos_p = jnp.arange(x_prompt.shape[1])
    pos_s = PAST_LEN + jnp.arange(x_sample.shape[1])
    shift0 = jnp.zeros((bp, C_COLS), x_prompt.dtype)
    wkv0 = jnp.zeros((bp, C_HEADS, C_HD, C_HD), jnp.float32)
    xp, xs = x_prompt, x_sample
    p_rows, s_rows = [], []
    for l in range(DEPTH):
        c_par = (c_mu[l], c_w0[l], c_w2[l], c_a0[l], c_a2[l], c_g2[l], c_k_k[l], c_k_a[l],
                 c_r_k[l], c_ln_w[l], c_ln_b[l])
        pa, pb, pc = in_proj(rms_norm(xp, norm1_g[l]), w_in[l])
        qa, ka, va, qi, ki, wi = prep_a(pa, pos_p, a_q_norm[l], a_k_norm[l], idx_k_norm[l])
        oa = dsa_prompt(qa, ka, va, qi, ki, wi)
        qb, kb, vb = prep_b(pb, pos_p, b_q_norm[l], b_k_norm[l])
        ob = moba_prompt(qb, kb, vb)
        oc, wkv_fin, shift_fin = rwkv_mix(pc, shift0, wkv0, *c_par)
        xp = xp + out_proj(oa, ob, oc, w_o[l])
        xp = xp + mlp(rms_norm(xp, norm2_g[l]), w_ff1[l], w_ff2[l])
        p_rows.append((ka, va, ki, kb, vb, wkv_fin.astype(x_prompt.dtype), shift_fin))
        sa, sb, sc = in_proj(rms_norm(xs, norm1_g[l]), w_in[l])
        qa, ka, va, qi, ki, wi = prep_a(sa, pos_s, a_q_norm[l], a_k_norm[l], idx_k_norm[l])
        oa = dsa_sample(qa, ka, va, qi, ki, wi, cache_a_k[l], cache_a_v[l], cache_a_idx[l], page_table)
        qb, kb, vb = prep_b(sb, pos_s, b_q_norm[l], b_k_norm[l])
        ob = moba_sample(qb, kb, vb, cache_b_k[l], cache_b_v[l], page_table)
        oc, wkv_fin, shift_fin = rwkv_mix(sc, state_shift[l], state_wkv[l], *c_par)
        xs = xs + out_proj(oa, ob, oc, w_o[l])
        xs = xs + mlp(rms_norm(xs, norm2_g[l]), w_ff1[l], w_ff2[l])
        s_rows.append((ka, va, ki, kb, vb, wkv_fin.astype(state_wkv.dtype), shift_fin))
    p_ak, p_av, p_ai, p_bk, p_bv, p_wkv, p_sh = [jnp.stack(z) for z in zip(*p_rows)]
    s_ak, s_av, s_ai, s_bk, s_bv, s_wkv, s_sh = [jnp.stack(z) for z in zip(*s_rows)]
    return (xp, xs, p_ak, p_av, p_ai, p_bk, p_bv, p_wkv, p_sh,
            s_ak, s_av, s_ai, s_bk, s_bv, s_wkv, s_sh)
```

```python
import functools

import jax
import jax.numpy as jnp
from jax import lax
from jax.experimental import pallas as pl
from jax.experimental.pallas import tpu as pltpu

F32 = jnp.float32
BF16 = jnp.bfloat16
I32 = jnp.int32

VMEM_LIMIT_BYTES = 56 * 1024 * 1024
LANES = 128

PAST_LEN = 8192
PAGE_SIZE = 128
A_HEADS = 4
A_HD = 128
A_WIDTH = A_HEADS * A_HD
IDX_HEADS = 16
IDX_HD = 64
A_TOPK_MAX = 256
A_Q_BLOCK = 128
B_HEADS = 4
B_HD = 128
B_WIDTH = B_HEADS * B_HD
MOBA_BLOCK = 256
MOBA_TOPK = 3
C_HD = 64
C_HEADS = 16
C_WIDTH = C_HEADS * C_HD
W_LORA = 64
A_LORA = 64
G_LORA = 160
A_COLS = 3 * A_WIDTH + IDX_HEADS * IDX_HD + IDX_HD + IDX_HEADS
B_COLS = 3 * B_WIDTH
C_COLS = 3 * C_WIDTH + W_LORA + A_LORA + G_LORA
ROPE_THETA = 500000.0
NORM_EPS = 1e-6
GN_EPS = 64e-5
IDX_W_SCALE = (IDX_HEADS * IDX_HD) ** -0.5
WKV_CHUNK = 64

A_PAD = -A_COLS % LANES
B_OFF = A_COLS + A_PAD
C_OFF = B_OFF + B_COLS
P_PACK = C_OFF + C_COLS + (-(C_OFF + C_COLS) % 512)

NEG_INF = float("-inf")
INT_MIN = -2 ** 31


def _cparams(*sem):
    return pltpu.CompilerParams(dimension_semantics=sem, vmem_limit_bytes=VMEM_LIMIT_BYTES)


def _dot_nt(a, b):
    return lax.dot_general(a, b, (((1,), (1,)), ((), ())), preferred_element_type=F32)


def _dot(a, b):
    return jnp.dot(a, b, preferred_element_type=F32)


def _norm_matmul_kernel(x_ref, g_ref, w_ref, o_ref, xn_ref):
    @pl.when(pl.program_id(1) == 0)
    def _():
        x = x_ref[...]
        y = x * lax.rsqrt(jnp.mean(x * x, axis=-1, keepdims=True) + NORM_EPS)
        xn_ref[...] = (y * g_ref[...]).astype(BF16)

    o_ref[...] = _dot(xn_ref[...], w_ref[...])


def norm_matmul(x, g, w, tm, tn):
    m, d = x.shape
    p = w.shape[1]
    return pl.pallas_call(
        _norm_matmul_kernel,
        grid=(m // tm, p // tn),
        in_specs=[pl.BlockSpec((tm, d), lambda i, j: (i, 0)),
                  pl.BlockSpec((1, d), lambda i, j: (0, 0)),
                  pl.BlockSpec((d, tn), lambda i, j: (0, j))],
        out_specs=pl.BlockSpec((tm, tn), lambda i, j: (i, j)),
        out_shape=jax.ShapeDtypeStruct((m, p), F32),
        scratch_shapes=[pltpu.VMEM((tm, d), BF16)],
        compiler_params=_cparams("parallel", "arbitrary"),
    )(x, g.reshape(1, d), w)


def _out_proj_kernel(x_ref, oa_ref, ob_ref, oc_ref, w_ref, o_ref):
    wa = oa_ref.shape[1]
    wb = ob_ref.shape[1]
    acc = _dot(oa_ref[...], w_ref[0:wa, :])
    acc += _dot(ob_ref[...], w_ref[wa:wa + wb, :])
    acc += _dot(oc_ref[...], w_ref[wa + wb:, :])
    o_ref[...] = x_ref[...] + acc


def out_proj(x, oa, ob, oc, w, tm):
    m, d = x.shape
    row = lambda i: (i, 0)
    return pl.pallas_call(
        _out_proj_kernel,
        grid=(m // tm,),
        in_specs=[pl.BlockSpec((tm, d), row),
                  pl.BlockSpec((tm, oa.shape[1]), row),
                  pl.BlockSpec((tm, ob.shape[1]), row),
                  pl.BlockSpec((tm, oc.shape[1]), row),
                  pl.BlockSpec(w.shape, lambda i: (0, 0))],
        out_specs=pl.BlockSpec((tm, d), row),
        out_shape=jax.ShapeDtypeStruct((m, d), F32),
        compiler_params=_cparams("parallel"),
    )(x, oa, ob, oc, w)


def _mlp_kernel(x_ref, g_ref, w1_ref, w2_ref, o_ref, xn_ref):
    @pl.when(pl.program_id(1) == 0)
    def _():
        x = x_ref[...]
        y = x * lax.rsqrt(jnp.mean(x * x, axis=-1, keepdims=True) + NORM_EPS)
        xn_ref[...] = (y * g_ref[...]).astype(BF16)
        o_ref[...] = x

    h = jnp.maximum(_dot(xn_ref[...], w1_ref[...]), 0.0)
    o_ref[...] += _dot((h * h).astype(BF16), w2_ref[...])


def mlp(x, g, w1, w2, tm, tf):
    m, d = x.shape
    f = w1.shape[1]
    return pl.pallas_call(
        _mlp_kernel,
        grid=(m // tm, f // tf),
        in_specs=[pl.BlockSpec((tm, d), lambda i, j: (i, 0)),
                  pl.BlockSpec((1, d), lambda i, j: (0, 0)),
                  pl.BlockSpec((d, tf), lambda i, j: (0, j)),
                  pl.BlockSpec((tf, d), lambda i, j: (j, 0))],
        out_specs=pl.BlockSpec((tm, d), lambda i, j: (i, 0)),
        out_shape=jax.ShapeDtypeStruct((m, d), F32),
        scratch_shapes=[pltpu.VMEM((tm, d), BF16)],
        compiler_params=_cparams("parallel", "arbitrary"),
    )(x, g.reshape(1, d), w1, w2)


def _sortable_key(x):
    bits = pltpu.bitcast(x, I32)
    return jnp.where(bits >= 0, bits, bits ^ jnp.int32(0x7FFFFFFF))


def _topk_mask(key, pos, topk, n_pos_bits):
    rows = key.shape[0]

    def count(m):
        return jnp.sum(m.astype(I32), axis=-1, keepdims=True)

    def value_step(it, tu):
        cand_u = tu | jnp.left_shift(jnp.int32(1), 31 - it)
        cnt = count(key >= (cand_u ^ jnp.int32(INT_MIN)))
        return jnp.where(cnt >= topk, cand_u, tu)

    tu = lax.fori_loop(0, 32, value_step, jnp.zeros((rows, 1), I32))
    thr = tu ^ jnp.int32(INT_MIN)
    gt = key > thr
    eq = key == thr
    need = topk - count(gt)
    excess = count(eq) - need

    def tie_search(_):
        def pos_step(it, x):
            cand = x | jnp.left_shift(jnp.int32(1), n_pos_bits - 1 - it)
            cnt = count(eq & (pos < cand))
            return jnp.where(cnt < need, cand, x)
        return lax.fori_loop(0, n_pos_bits, pos_step, jnp.zeros((rows, 1), I32))

    last = lax.cond(jnp.max(excess) > 0, tie_search,
                    lambda _: jnp.full((rows, 1), 2 ** n_pos_bits, I32), 0)
    return gt | (eq & (pos <= last))


def _masked_attention(q, k, v, mask, scale):
    s = _dot_nt(q, k) * scale
    s = jnp.where(mask, s, NEG_INF)
    m = jnp.max(s, axis=-1, keepdims=True)
    p = jnp.exp(s - m)
    l = jnp.sum(p, axis=-1, keepdims=True)
    return _dot(p.astype(BF16), v) / l


def _dsa_prompt_kernel(q_ref, k_ref, v_ref, qi_ref, ki_ref, wi_ref, o_ref, *, topk):
    tq = q_ref.shape[0]
    s_len = k_ref.shape[0]
    qpos = pl.program_id(1) * tq + lax.broadcasted_iota(I32, (tq, s_len), 0)
    kpos = lax.broadcasted_iota(I32, (tq, s_len), 1)
    causal = kpos <= qpos

    ki = ki_ref[...]
    wi = wi_ref[...]
    isc = jnp.zeros((tq, s_len), F32)
    for h in range(IDX_HEADS):
        s = _dot_nt(qi_ref[:, h * IDX_HD:(h + 1) * IDX_HD], ki)
        isc += jnp.maximum(s, 0.0) * wi[:, h:h + 1]
    key = _sortable_key(jnp.where(causal, isc, NEG_INF))
    mask = _topk_mask(key, kpos, topk, (s_len - 1).bit_length()) & causal

    for h in range(A_HEADS):
        sl = slice(h * A_HD, (h + 1) * A_HD)
        o = _masked_attention(q_ref[:, sl], k_ref[:, sl], v_ref[:, sl], mask, A_HD ** -0.5)
        o_ref[:, sl] = o.astype(o_ref.dtype)


def dsa_prompt(q, k, v, qi, ki, wi):
    b, s, _ = q.shape
    tq = A_Q_BLOCK
    blk = lambda bi, i: (bi, i, 0)
    full = lambda bi, i: (bi, 0, 0)
    return pl.pallas_call(
        functools.partial(_dsa_prompt_kernel, topk=min(A_TOPK_MAX, s // 4)),
        grid=(b, s // tq),
        in_specs=[pl.BlockSpec((None, tq, A_WIDTH), blk),
                  pl.BlockSpec((None, s, A_WIDTH), full),
                  pl.BlockSpec((None, s, A_WIDTH), full),
                  pl.BlockSpec((None, tq, IDX_HEADS * IDX_HD), blk),
                  pl.BlockSpec((None, s, IDX_HD), full),
                  pl.BlockSpec((None, tq, IDX_HEADS), blk)],
        out_specs=pl.BlockSpec((None, tq, A_WIDTH), blk),
        out_shape=jax.ShapeDtypeStruct((b, s, A_WIDTH), BF16),
        compiler_params=_cparams("parallel", "parallel"),
    )(q, k, v, qi, ki, wi)


def _moba_prompt_kernel(q_ref, k_ref, v_ref, o_ref, kmean_ref, *, ksel):
    tq = q_ref.shape[0]
    s_len = k_ref.shape[0]
    nb = s_len // MOBA_BLOCK
    own = pl.program_id(1)

    @pl.when(own == 0)
    def _():
        member = (lax.broadcasted_iota(I32, (nb, s_len), 1) // MOBA_BLOCK
                  == lax.broadcasted_iota(I32, (nb, s_len), 0))
        kmean_ref[...] = _dot(member.astype(BF16), k_ref[...]) * (1.0 / MOBA_BLOCK)

    qpos = own * tq + lax.broadcasted_iota(I32, (tq, s_len), 0)
    kpos = lax.broadcasted_iota(I32, (tq, s_len), 1)
    kblk = kpos // MOBA_BLOCK
    own_mask = (kblk == own) & (kpos <= qpos)
    nidx = lax.broadcasted_iota(I32, (tq, nb), 1)

    for h in range(B_HEADS):
        sl = slice(h * B_HD, (h + 1) * B_HD)
        q = q_ref[:, sl]
        g = _dot_nt(q, kmean_ref[:, sl].astype(BF16))
        g = jnp.where(nidx < own, g, NEG_INF)
        rank = jnp.zeros((tq, nb), I32)
        for m in range(nb):
            gm = g[:, m:m + 1]
            beats = (gm > g) | ((gm == g) & (m < nidx))
            rank += jnp.where(beats & (m < own), 1, 0)
        chosen = jnp.where((nidx < own) & (rank < ksel), 1.0, 0.0)
        picked = jnp.zeros((tq, s_len), F32)
        for n in range(nb):
            picked = jnp.where(kblk == n, chosen[:, n:n + 1], picked)
        mask = (picked > 0.5) | own_mask
        o = _masked_attention(q, k_ref[:, sl], v_ref[:, sl], mask, B_HD ** -0.5)
        o_ref[:, sl] = o.astype(o_ref.dtype)


def moba_prompt(q, k, v):
    b, s, _ = q.shape
    nb = s // MOBA_BLOCK
    tq = MOBA_BLOCK
    blk = lambda bi, i: (bi, i, 0)
    full = lambda bi, i: (bi, 0, 0)
    return pl.pallas_call(
        functools.partial(_moba_prompt_kernel, ksel=max(1, min(MOBA_TOPK, nb - 1))),
        grid=(b, nb),
        in_specs=[pl.BlockSpec((None, tq, B_WIDTH), blk),
                  pl.BlockSpec((None, s, B_WIDTH), full),
                  pl.BlockSpec((None, s, B_WIDTH), full)],
        out_specs=pl.BlockSpec((None, tq, B_WIDTH), blk),
        out_shape=jax.ShapeDtypeStruct((b, s, B_WIDTH), BF16),
        scratch_shapes=[pltpu.VMEM((nb, B_WIDTH), F32)],
        compiler_params=_cparams("parallel", "arbitrary"),
    )(q, k, v)


def _split3(x):
    hi = x.astype(BF16)
    r1 = x - hi.astype(F32)
    mid = r1.astype(BF16)
    lo = (r1 - mid.astype(F32)).astype(BF16)
    return hi, mid, lo


def _wkv_chunk(r, ld, k, v, kkr, a, state, c):
    kk = kkr / jnp.maximum(jnp.sqrt(jnp.sum(kkr * kkr, axis=-1, keepdims=True)), 1e-12)
    av = -kk
    bv = kk * a
    row = lax.broadcasted_iota(I32, (c, c), 0)
    col = lax.broadcasted_iota(I32, (c, c), 1)
    incl = jnp.where(row >= col, 1.0, 0.0).astype(BF16)
    hi, mid, lo = _split3(ld)
    cum = _dot(incl, hi) + _dot(incl, mid) + _dot(incl, lo)
    g_inc = jnp.exp(cum)
    g_exc = jnp.exp(cum - ld)
    g_inv = jnp.exp(-cum)
    g_end = g_inc[c - 1:c, :]
    ar = jnp.concatenate([av * g_exc, r * g_inc], axis=0).astype(BF16)
    bk = jnp.concatenate([bv * g_inv, k * g_inv], axis=0)
    gram = _dot_nt(ar, bk.astype(BF16))
    strict = row > col
    l_ab = jnp.where(strict, gram[:c, :c], 0.0)
    l_ak = jnp.where(strict, gram[:c, c:], 0.0)
    l_r = jnp.concatenate([jnp.where(row >= col, gram[c:, :c], 0.0),
                           jnp.where(row >= col, gram[c:, c:], 0.0)], axis=1)
    inv = jnp.where(row == col, 1.0, 0.0) + l_ab
    lp = l_ab
    n = 2
    while n < c:
        lp16 = lp.astype(BF16)
        lp = _dot(lp16, lp16)
        inv = inv + _dot(inv.astype(BF16), lp.astype(BF16))
        n *= 2
    arh = _dot_nt(ar, state.astype(BF16))
    v16 = v.astype(BF16)
    u = _dot(inv.astype(BF16), (arh[:c] + _dot(l_ak.astype(BF16), v16)).astype(BF16))
    uv = jnp.concatenate([u.astype(BF16), v16], axis=0)
    y = arh[c:] + _dot(l_r.astype(BF16), uv)
    bk_end = (bk * g_end).astype(BF16)
    new_state = state * g_end + lax.dot_general(uv, bk_end, (((0,), (0,)), ((), ())),
                                                preferred_element_type=F32)
    return y, new_state


def _wkv_kernel(r_ref, ld_ref, k_ref, v_ref, kkr_ref, a_ref, g_ref, lnw_ref, lnb_ref, rk_ref, s0_ref,
                o_ref, sfin_ref, state_ref, *, chunk):
    tb = r_ref.shape[0]
    t_blk = pl.program_id(1)

    @pl.when(t_blk == 0)
    def _():
        state_ref[...] = s0_ref[...]

    def chunk_step(ci, carry):
        rows = pl.ds(pl.multiple_of(ci * chunk, chunk), chunk)
        for h in range(C_HEADS):
            sl = slice(h * C_HD, (h + 1) * C_HD)
            r = r_ref[rows, sl]
            k = k_ref[rows, sl]
            v = v_ref[rows, sl]
            y, new_state = _wkv_chunk(r, ld_ref[rows, sl], k, v, kkr_ref[rows, sl], a_ref[rows, sl],
                                      state_ref[h], chunk)
            state_ref[h] = new_state
            mean = jnp.mean(y, axis=-1, keepdims=True)
            var = jnp.mean(jnp.square(y - mean), axis=-1, keepdims=True)
            yn = (y - mean) * lax.rsqrt(var + GN_EPS) * lnw_ref[:, sl] + lnb_ref[:, sl]
            bonus = jnp.sum(r * k * rk_ref[:, sl], axis=-1, keepdims=True) * v
            o_ref[rows, sl] = ((yn + bonus) * g_ref[rows, sl]).astype(o_ref.dtype)
        return carry

    lax.fori_loop(0, tb // chunk, chunk_step, 0)

    @pl.when(t_blk == pl.num_programs(1) - 1)
    def _():
        sfin_ref[...] = state_ref[...]


def wkv7(r, ld, k, v, kkr, a, g, ln_w, ln_b, r_k, s0, chunk, tb):
    b, s, w = r.shape
    blk = pl.BlockSpec((None, tb, w), lambda bi, i: (bi, i, 0))
    par = pl.BlockSpec((1, w), lambda bi, i: (0, 0))
    st = pl.BlockSpec((None, C_HEADS, C_HD, C_HD), lambda bi, i: (bi, 0, 0, 0))
    return pl.pallas_call(
        functools.partial(_wkv_kernel, chunk=chunk),
        grid=(b, s // tb),
        in_specs=[blk] * 7 + [par] * 3 + [st],
        out_specs=[blk, st],
        out_shape=[jax.ShapeDtypeStruct((b, s, w), BF16),
                   jax.ShapeDtypeStruct((b, C_HEADS, C_HD, C_HD), F32)],
        scratch_shapes=[pltpu.VMEM((C_HEADS, C_HD, C_HD), F32)],
        compiler_params=_cparams("parallel", "arbitrary"),
    )(r, ld, k, v, kkr, a, g, ln_w.reshape(1, w), ln_b.reshape(1, w), r_k.reshape(1, w), s0)


def _rms_norm(x, g):
    y = x * lax.rsqrt(jnp.mean(x * x, axis=-1, keepdims=True) + NORM_EPS)
    return y * g


def _rope_partial(x, pos):
    hd = x.shape[-1]
    rot = hd // 4
    half = rot // 2
    inv = ROPE_THETA ** (-2.0 * jnp.arange(half, dtype=F32) / rot)
    ang = pos.astype(F32)[:, None] * inv[None, :]
    cos = jnp.cos(ang)[:, None, :]
    sin = jnp.sin(ang)[:, None, :]
    x1, x2 = x[..., :half], x[..., half:rot]
    return jnp.concatenate([x1 * cos - x2 * sin, x2 * cos + x1 * sin, x[..., rot:]], axis=-1)


def _prep_a(pa, pos, q_g, k_g, ik_g):
    b, s = pa.shape[:2]
    o = 0
    q = pa[..., o:o + A_WIDTH]; o += A_WIDTH
    k = pa[..., o:o + A_WIDTH]; o += A_WIDTH
    v = pa[..., o:o + A_WIDTH]; o += A_WIDTH
    qi = pa[..., o:o + IDX_HEADS * IDX_HD]; o += IDX_HEADS * IDX_HD
    ki = pa[..., o:o + IDX_HD]; o += IDX_HD
    wi = pa[..., o:o + IDX_HEADS]
    q = _rope_partial(_rms_norm(q.reshape(b, s, A_HEADS, A_HD), q_g), pos)
    k = _rope_partial(_rms_norm(k.reshape(b, s, A_HEADS, A_HD), k_g), pos)
    v = v.reshape(b, s, A_HEADS, A_HD)
    qi = _rope_partial(qi.reshape(b, s, IDX_HEADS, IDX_HD), pos)
    ki = _rope_partial(_rms_norm(ki, ik_g)[:, :, None, :], pos)[:, :, 0, :]
    return q, k, v, qi, ki, wi * IDX_W_SCALE


def _prep_b(pb, pos, q_g, k_g):
    b, s = pb.shape[:2]
    q = pb[..., :B_WIDTH]
    k = pb[..., B_WIDTH:2 * B_WIDTH]
    v = pb[..., 2 * B_WIDTH:]
    q = _rope_partial(_rms_norm(q.reshape(b, s, B_HEADS, B_HD), q_g), pos)
    k = _rope_partial(_rms_norm(k.reshape(b, s, B_HEADS, B_HD), k_g), pos)
    return q, k, v.reshape(b, s, B_HEADS, B_HD)


def _rwkv_pre(pc, prev, mu, w0, w2, a0, a2, g2, k_k, k_a):
    shifted = jnp.concatenate([prev[:, None], pc[:, :-1]], axis=1)
    xs = pc + mu * (shifted - pc)
    o = 0
    r = xs[..., o:o + C_WIDTH]; o += C_WIDTH
    k = xs[..., o:o + C_WIDTH]; o += C_WIDTH
    v = xs[..., o:o + C_WIDTH]; o += C_WIDTH
    wl = xs[..., o:o + W_LORA]; o += W_LORA
    al = xs[..., o:o + A_LORA]; o += A_LORA
    gl = xs[..., o:o + G_LORA]
    w = -jax.nn.softplus(-(w0 + jnp.tanh(wl) @ w2)) - 0.5
    a = jax.nn.sigmoid(a0 + al @ a2)
    g = jax.nn.sigmoid(gl) @ g2
    kkr = k * k_k
    k = k * (1.0 + (a - 1.0) * k_a)
    return r, -jnp.exp(w), k, v, kkr, a, g


def _index_scores(qi, wi, ki):
    s = jnp.einsum('bqhd,bld->bqhl', qi, ki)
    return jnp.einsum('bqhl,bqh->bql', jax.nn.relu(s), wi)


def _dsa_attend(q, kg, vg, valid):
    s = jnp.einsum('bqhd,bqkhd->bqhk', q, kg) * (A_HD ** -0.5)
    s = jnp.where(valid[:, :, None, :], s, -jnp.inf)
    p = jax.nn.softmax(s, axis=-1)
    return jnp.einsum('bqhk,bqkhd->bqhd', p, vg)


def _dsa_sample(q, k, v, qi, ki, wi, cache_k, cache_v, cache_idx, page_table):
    db, t = q.shape[:2]
    n_pages = PAST_LEN // PAGE_SIZE
    L = PAST_LEN + t
    topk = min(A_TOPK_MAX, L // 4)
    ki_past = cache_idx[page_table].reshape(db, PAST_LEN, IDX_HD)
    ki_all = jnp.concatenate([ki_past, ki], axis=1)
    qpos = PAST_LEN + jnp.arange(t)
    isc = _index_scores(qi, wi, ki_all)
    isc = jnp.where(jnp.arange(L)[None, None, :] <= qpos[None, :, None], isc, -jnp.inf)
    _, sel = lax.top_k(isc, topk)
    bidx = jnp.arange(db)[:, None, None]
    phys = page_table[bidx, jnp.minimum(sel // PAGE_SIZE, n_pages - 1)]
    off = sel % PAGE_SIZE
    from_past = (sel < PAST_LEN)[..., None, None]
    new_i = jnp.clip(sel - PAST_LEN, 0, t - 1)
    kg = jnp.where(from_past, cache_k[phys, off], k[bidx, new_i])
    vg = jnp.where(from_past, cache_v[phys, off], v[bidx, new_i])
    return _dsa_attend(q, kg, vg, sel <= qpos[None, :, None])


def _two_part_softmax(q, k1, v1, m1, k2, v2, m2):
    scale = B_HD ** -0.5
    s1 = jnp.einsum('bqhd,bqhkd->bqhk', q, k1) * scale
    s2 = jnp.einsum('bqhd,bkhd->bqhk', q, k2) * scale
    s = jnp.concatenate([jnp.where(m1, s1, -jnp.inf), jnp.where(m2, s2, -jnp.inf)], axis=-1)
    p = jax.nn.softmax(s, axis=-1)
    n1 = k1.shape[3]
    return (jnp.einsum('bqhk,bqhkd->bqhd', p[..., :n1], v1)
            + jnp.einsum('bqhk,bkhd->bqhd', p[..., n1:], v2))


def _moba_sample(q, k, v, cache_k, cache_v, page_table):
    db, t, h, hd = q.shape
    ppb = MOBA_BLOCK // PAGE_SIZE
    n_pages = PAST_LEN // PAGE_SIZE
    nbp = PAST_LEN // MOBA_BLOCK
    page_mean = jnp.mean(cache_k, axis=1)[page_table[:, :nbp * ppb]]
    bmean = jnp.mean(page_mean.reshape(db, nbp, ppb, h, hd), axis=2)
    g = jnp.einsum('bqhd,bnhd->bqhn', q, bmean)
    ksel = max(1, min(MOBA_TOPK, nbp))
    _, sel = lax.top_k(g, ksel)
    m1 = jnp.repeat(sel < nbp, MOBA_BLOCK, axis=-1)
    lpage = jnp.minimum(sel[..., None] * ppb + jnp.arange(ppb), n_pages - 1)
    phys = page_table[jnp.arange(db)[:, None, None, None, None], lpage]
    ridx = jnp.arange(PAGE_SIZE)
    hidx = jnp.arange(h)[None, None, :, None, None, None]
    kg = cache_k[phys[..., None], ridx, hidx].reshape(db, t, h, ksel * MOBA_BLOCK, hd)
    vg = cache_v[phys[..., None], ridx, hidx].reshape(db, t, h, ksel * MOBA_BLOCK, hd)
    qpos = PAST_LEN + jnp.arange(t)
    kpos = nbp * MOBA_BLOCK + jnp.arange(t)
    m2 = (kpos[None, :] <= qpos[:, None])[None, :, None, :]
    return _two_part_softmax(q, kg, vg, m1, k, v, m2)


def _pack_w_in(w):
    d = w.shape[0]
    return jnp.concatenate([w[:, :A_COLS], jnp.zeros((d, A_PAD), w.dtype), w[:, A_COLS:],
                            jnp.zeros((d, P_PACK - C_OFF - C_COLS), w.dtype)], axis=1).astype(BF16)


def _layer(x, pos, prev_shift, wkv0, lw, sample_ctx):
    b, s, d = x.shape
    m = b * s
    tm = min(m, 1024)
    p = norm_matmul(x.reshape(m, d), lw['norm1_g'], lw['w_in'], tm, 512).reshape(b, s, P_PACK)
    pa = p[..., :A_COLS]
    pb = p[..., B_OFF:B_OFF + B_COLS]
    pc = p[..., C_OFF:C_OFF + C_COLS]
    qa, ka, va, qi, ki, wi = _prep_a(pa, pos, lw['a_q_norm'], lw['a_k_norm'], lw['idx_k_norm'])
    qb, kb, vb = _prep_b(pb, pos, lw['b_q_norm'], lw['b_k_norm'])
    if sample_ctx is None:
        flat = lambda z: z.reshape(b, s, -1).astype(BF16)
        oa = dsa_prompt(flat(qa), flat(ka), flat(va), flat(qi), ki.astype(BF16), wi)
        ob = moba_prompt(flat(qb), flat(kb), flat(vb))
        pad = 0
    else:
        cache_a_k, cache_a_v, cache_a_idx, cache_b_k, cache_b_v, page_table = sample_ctx
        oa = _dsa_sample(qa, ka, va, qi, ki, wi, cache_a_k, cache_a_v, cache_a_idx, page_table)
        ob = _moba_sample(qb, kb, vb, cache_b_k, cache_b_v, page_table)
        oa = oa.reshape(b, s, A_WIDTH).astype(BF16)
        ob = ob.reshape(b, s, B_WIDTH).astype(BF16)
        pad = -s % 16
    parts = _rwkv_pre(pc, prev_shift, lw['c_mu'], lw['c_w0'], lw['c_w2'], lw['c_a0'], lw['c_a2'],
                      lw['c_g2'], lw['c_k_k'], lw['c_k_a'])
    if pad:
        parts = [jnp.pad(z, ((0, 0), (0, pad), (0, 0))) for z in parts]
    sp = s + pad
    chunk = min(WKV_CHUNK, sp)
    oc, wkv_fin = wkv7(*parts, lw['c_ln_w'], lw['c_ln_b'], lw['c_r_k'].reshape(-1), wkv0,
                       chunk, min(sp, 4 * chunk))
    oc = oc[:, :s]
    tm2 = min(m, 512)
    x2 = out_proj(x.reshape(m, d), oa.reshape(m, A_WIDTH), ob.reshape(m, B_WIDTH),
                  oc.reshape(m, C_WIDTH), lw['w_o'], tm2)
    x3 = mlp(x2, lw['norm2_g'], lw['w_ff1'], lw['w_ff2'], tm2, 512).reshape(b, s, d)
    return x3, (ka, va, ki, kb, vb, wkv_fin, pc[:, -1])


def kernel(x_prompt, x_sample, cache_a_k, cache_a_v, cache_a_idx, cache_b_k, cache_b_v, state_wkv, state_shift, page_table, norm1_g, w_in, a_q_norm, a_k_norm, idx_k_norm, b_q_norm, b_k_norm, c_mu, c_w0, c_w2, c_a0, c_a2, c_g2, c_k_k, c_k_a, c_r_k, c_ln_w, c_ln_b, w_o, norm2_g, w_ff1, w_ff2):
    depth = w_in.shape[0]
    bp = x_prompt.shape[0]
    pos_p = jnp.arange(x_prompt.shape[1])
    pos_s = PAST_LEN + jnp.arange(x_sample.shape[1])
    shift0 = jnp.zeros((bp, C_COLS), x_prompt.dtype)
    wkv0 = jnp.zeros((bp, C_HEADS, C_HD, C_HD), F32)
    xp, xs = x_prompt, x_sample
    p_rows, s_rows = [], []
    for l in range(depth):
        lw = dict(norm1_g=norm1_g[l], w_in=_pack_w_in(w_in[l]), a_q_norm=a_q_norm[l], a_k_norm=a_k_norm[l],
                  idx_k_norm=idx_k_norm[l], b_q_norm=b_q_norm[l], b_k_norm=b_k_norm[l], c_mu=c_mu[l],
                  c_w0=c_w0[l], c_w2=c_w2[l], c_a0=c_a0[l], c_a2=c_a2[l], c_g2=c_g2[l], c_k_k=c_k_k[l],
                  c_k_a=c_k_a[l], c_r_k=c_r_k[l], c_ln_w=c_ln_w[l], c_ln_b=c_ln_b[l],
                  w_o=w_o[l].astype(BF16), norm2_g=norm2_g[l], w_ff1=w_ff1[l].astype(BF16),
                  w_ff2=w_ff2[l].astype(BF16))
        xp, rows = _layer(xp, pos_p, shift0, wkv0, lw, None)
        p_rows.append(rows)
        ctx = (cache_a_k[l], cache_a_v[l], cache_a_idx[l], cache_b_k[l], cache_b_v[l], page_table)
        xs, rows = _layer(xs, pos_s, state_shift[l], state_wkv[l], lw, ctx)
        s_rows.append(rows)
    p_out = [jnp.stack(z) for z in zip(*p_rows)]
    s_out = [jnp.stack(z) for z in zip(*s_rows)]
    return (xp, xs, *p_out, *s_out)
```

```python
import functools

import jax
import jax.numpy as jnp
from jax import lax
from jax.experimental import pallas as pl
from jax.experimental.pallas import tpu as pltpu

F32 = jnp.float32
BF16 = jnp.bfloat16
I32 = jnp.int32

VMEM_LIMIT_BYTES = 56 * 1024 * 1024
LANES = 128

PAST_LEN = 8192
PAGE_SIZE = 128
A_HEADS = 4
A_HD = 128
A_WIDTH = A_HEADS * A_HD
IDX_HEADS = 16
IDX_HD = 64
A_TOPK_MAX = 256
A_Q_BLOCK = 128
B_HEADS = 4
B_HD = 128
B_WIDTH = B_HEADS * B_HD
MOBA_BLOCK = 256
MOBA_TOPK = 3
C_HD = 64
C_HEADS = 16
C_WIDTH = C_HEADS * C_HD
W_LORA = 64
A_LORA = 64
G_LORA = 160
A_COLS = 3 * A_WIDTH + IDX_HEADS * IDX_HD + IDX_HD + IDX_HEADS
B_COLS = 3 * B_WIDTH
C_COLS = 3 * C_WIDTH + W_LORA + A_LORA + G_LORA
ROPE_THETA = 500000.0
NORM_EPS = 1e-6
GN_EPS = 64e-5
IDX_W_SCALE = (IDX_HEADS * IDX_HD) ** -0.5
WKV_CHUNK = 64
SAMPLE_PAGES_PER_STEP = 8

A_PAD = -A_COLS % LANES
B_OFF = A_COLS + A_PAD
C_OFF = B_OFF + B_COLS
P_PACK = C_OFF + C_COLS + (-(C_OFF + C_COLS) % 512)

NEG_INF = float("-inf")
INT_MIN = -2 ** 31


def _cparams(*sem):
    return pltpu.CompilerParams(dimension_semantics=sem, vmem_limit_bytes=VMEM_LIMIT_BYTES)


def _dot_nt(a, b):
    return lax.dot_general(a, b, (((1,), (1,)), ((), ())), preferred_element_type=F32)


def _dot(a, b):
    return jnp.dot(a, b, preferred_element_type=F32)


def _norm_matmul_kernel(x_ref, g_ref, w_ref, o_ref, xn_ref):
    @pl.when(pl.program_id(1) == 0)
    def _():
        x = x_ref[...]
        y = x * lax.rsqrt(jnp.mean(x * x, axis=-1, keepdims=True) + NORM_EPS)
        xn_ref[...] = (y * g_ref[...]).astype(BF16)

    o_ref[...] = _dot(xn_ref[...], w_ref[...])


def norm_matmul(x, g, w, tm, tn):
    m, d = x.shape
    p = w.shape[1]
    return pl.pallas_call(
        _norm_matmul_kernel,
        grid=(m // tm, p // tn),
        in_specs=[pl.BlockSpec((tm, d), lambda i, j: (i, 0)),
                  pl.BlockSpec((1, d), lambda i, j: (0, 0)),
                  pl.BlockSpec((d, tn), lambda i, j: (0, j))],
        out_specs=pl.BlockSpec((tm, tn), lambda i, j: (i, j)),
        out_shape=jax.ShapeDtypeStruct((m, p), F32),
        scratch_shapes=[pltpu.VMEM((tm, d), BF16)],
        compiler_params=_cparams("parallel", "arbitrary"),
    )(x, g.reshape(1, d), w)


def _out_proj_kernel(x_ref, oa_ref, ob_ref, oc_ref, w_ref, o_ref):
    wa = oa_ref.shape[1]
    wb = ob_ref.shape[1]
    acc = _dot(oa_ref[...], w_ref[0:wa, :])
    acc += _dot(ob_ref[...], w_ref[wa:wa + wb, :])
    acc += _dot(oc_ref[...], w_ref[wa + wb:, :])
    o_ref[...] = x_ref[...] + acc


def out_proj(x, oa, ob, oc, w, tm):
    m, d = x.shape
    row = lambda i: (i, 0)
    return pl.pallas_call(
        _out_proj_kernel,
        grid=(m // tm,),
        in_specs=[pl.BlockSpec((tm, d), row),
                  pl.BlockSpec((tm, oa.shape[1]), row),
                  pl.BlockSpec((tm, ob.shape[1]), row),
                  pl.BlockSpec((tm, oc.shape[1]), row),
                  pl.BlockSpec(w.shape, lambda i: (0, 0))],
        out_specs=pl.BlockSpec((tm, d), row),
        out_shape=jax.ShapeDtypeStruct((m, d), F32),
        compiler_params=_cparams("parallel"),
    )(x, oa, ob, oc, w)


def _mlp_kernel(x_ref, g_ref, w1_ref, w2_ref, o_ref, xn_ref):
    @pl.when(pl.program_id(1) == 0)
    def _():
        x = x_ref[...]
        y = x * lax.rsqrt(jnp.mean(x * x, axis=-1, keepdims=True) + NORM_EPS)
        xn_ref[...] = (y * g_ref[...]).astype(BF16)
        o_ref[...] = x

    h = jnp.maximum(_dot(xn_ref[...], w1_ref[...]), 0.0)
    o_ref[...] += _dot((h * h).astype(BF16), w2_ref[...])


def mlp(x, g, w1, w2, tm, tf):
    m, d = x.shape
    f = w1.shape[1]
    return pl.pallas_call(
        _mlp_kernel,
        grid=(m // tm, f // tf),
        in_specs=[pl.BlockSpec((tm, d), lambda i, j: (i, 0)),
                  pl.BlockSpec((1, d), lambda i, j: (0, 0)),
                  pl.BlockSpec((d, tf), lambda i, j: (0, j)),
                  pl.BlockSpec((tf, d), lambda i, j: (j, 0))],
        out_specs=pl.BlockSpec((tm, d), lambda i, j: (i, 0)),
        out_shape=jax.ShapeDtypeStruct((m, d), F32),
        scratch_shapes=[pltpu.VMEM((tm, d), BF16)],
        compiler_params=_cparams("parallel", "arbitrary"),
    )(x, g.reshape(1, d), w1, w2)


def _sortable_key(x):
    bits = pltpu.bitcast(x, I32)
    return jnp.where(bits >= 0, bits, bits ^ jnp.int32(0x7FFFFFFF))


def _topk_mask(key, pos, topk, n_pos_bits):
    rows = key.shape[0]

    def count(m):
        return jnp.sum(m.astype(I32), axis=-1, keepdims=True)

    def value_step(it, tu):
        cand_u = tu | jnp.left_shift(jnp.int32(1), 31 - it)
        cnt = count(key >= (cand_u ^ jnp.int32(INT_MIN)))
        return jnp.where(cnt >= topk, cand_u, tu)

    tu = lax.fori_loop(0, 32, value_step, jnp.zeros((rows, 1), I32))
    thr = tu ^ jnp.int32(INT_MIN)
    gt = key > thr
    eq = key == thr
    need = topk - count(gt)
    excess = count(eq) - need

    def tie_search(_):
        def pos_step(it, x):
            cand = x | jnp.left_shift(jnp.int32(1), n_pos_bits - 1 - it)
            cnt = count(eq & (pos < cand))
            return jnp.where(cnt < need, cand, x)
        return lax.fori_loop(0, n_pos_bits, pos_step, jnp.zeros((rows, 1), I32))

    last = lax.cond(jnp.max(excess) > 0, tie_search,
                    lambda _: jnp.full((rows, 1), 2 ** n_pos_bits, I32), 0)
    return gt | (eq & (pos <= last))


def _masked_attention(q, k, v, mask, scale):
    s = _dot_nt(q, k) * scale
    s = jnp.where(mask, s, NEG_INF)
    m = jnp.max(s, axis=-1, keepdims=True)
    p = jnp.exp(s - m)
    l = jnp.sum(p, axis=-1, keepdims=True)
    return _dot(p.astype(BF16), v) / l


def _dsa_prompt_kernel(q_ref, k_ref, v_ref, qi_ref, ki_ref, wi_ref, o_ref, *, topk):
    tq = q_ref.shape[0]
    s_len = k_ref.shape[0]
    qpos = pl.program_id(1) * tq + lax.broadcasted_iota(I32, (tq, s_len), 0)
    kpos = lax.broadcasted_iota(I32, (tq, s_len), 1)
    causal = kpos <= qpos

    ki = ki_ref[...]
    wi = wi_ref[...]
    isc = jnp.zeros((tq, s_len), F32)
    for h in range(IDX_HEADS):
        s = _dot_nt(qi_ref[:, h * IDX_HD:(h + 1) * IDX_HD], ki)
        isc += jnp.maximum(s, 0.0) * wi[:, h:h + 1]
    key = _sortable_key(jnp.where(causal, isc, NEG_INF))
    mask = _topk_mask(key, kpos, topk, (s_len - 1).bit_length()) & causal

    for h in range(A_HEADS):
        sl = slice(h * A_HD, (h + 1) * A_HD)
        o = _masked_attention(q_ref[:, sl], k_ref[:, sl], v_ref[:, sl], mask, A_HD ** -0.5)
        o_ref[:, sl] = o.astype(o_ref.dtype)


def dsa_prompt(q, k, v, qi, ki, wi):
    b, s, _ = q.shape
    tq = A_Q_BLOCK
    blk = lambda bi, i: (bi, i, 0)
    full = lambda bi, i: (bi, 0, 0)
    return pl.pallas_call(
        functools.partial(_dsa_prompt_kernel, topk=min(A_TOPK_MAX, s // 4)),
        grid=(b, s // tq),
        in_specs=[pl.BlockSpec((None, tq, A_WIDTH), blk),
                  pl.BlockSpec((None, s, A_WIDTH), full),
                  pl.BlockSpec((None, s, A_WIDTH), full),
                  pl.BlockSpec((None, tq, IDX_HEADS * IDX_HD), blk),
                  pl.BlockSpec((None, s, IDX_HD), full),
                  pl.BlockSpec((None, tq, IDX_HEADS), blk)],
        out_specs=pl.BlockSpec((None, tq, A_WIDTH), blk),
        out_shape=jax.ShapeDtypeStruct((b, s, A_WIDTH), BF16),
        compiler_params=_cparams("parallel", "parallel"),
    )(q, k, v, qi, ki, wi)


def _moba_prompt_kernel(q_ref, k_ref, v_ref, o_ref, kmean_ref, *, ksel):
    tq = q_ref.shape[0]
    s_len = k_ref.shape[0]
    nb = s_len // MOBA_BLOCK
    own = pl.program_id(1)

    @pl.when(own == 0)
    def _():
        member = (lax.broadcasted_iota(I32, (nb, s_len), 1) // MOBA_BLOCK
                  == lax.broadcasted_iota(I32, (nb, s_len), 0))
        kmean_ref[...] = _dot(member.astype(BF16), k_ref[...]) * (1.0 / MOBA_BLOCK)

    qpos = own * tq + lax.broadcasted_iota(I32, (tq, s_len), 0)
    kpos = lax.broadcasted_iota(I32, (tq, s_len), 1)
    kblk = kpos // MOBA_BLOCK
    own_mask = (kblk == own) & (kpos <= qpos)
    nidx = lax.broadcasted_iota(I32, (tq, nb), 1)

    for h in range(B_HEADS):
        sl = slice(h * B_HD, (h + 1) * B_HD)
        q = q_ref[:, sl]
        g = _dot_nt(q, kmean_ref[:, sl].astype(BF16))
        g = jnp.where(nidx < own, g, NEG_INF)
        rank = jnp.zeros((tq, nb), I32)
        for m in range(nb):
            gm = g[:, m:m + 1]
            beats = (gm > g) | ((gm == g) & (m < nidx))
            rank += jnp.where(beats & (m < own), 1, 0)
        chosen = jnp.where((nidx < own) & (rank < ksel), 1.0, 0.0)
        picked = jnp.zeros((tq, s_len), F32)
        for n in range(nb):
            picked = jnp.where(kblk == n, chosen[:, n:n + 1], picked)
        mask = (picked > 0.5) | own_mask
        o = _masked_attention(q, k_ref[:, sl], v_ref[:, sl], mask, B_HD ** -0.5)
        o_ref[:, sl] = o.astype(o_ref.dtype)


def moba_prompt(q, k, v):
    b, s, _ = q.shape
    nb = s // MOBA_BLOCK
    tq = MOBA_BLOCK
    blk = lambda bi, i: (bi, i, 0)
    full = lambda bi, i: (bi, 0, 0)
    return pl.pallas_call(
        functools.partial(_moba_prompt_kernel, ksel=max(1, min(MOBA_TOPK, nb - 1))),
        grid=(b, nb),
        in_specs=[pl.BlockSpec((None, tq, B_WIDTH), blk),
                  pl.BlockSpec((None, s, B_WIDTH), full),
                  pl.BlockSpec((None, s, B_WIDTH), full)],
        out_specs=pl.BlockSpec((None, tq, B_WIDTH), blk),
        out_shape=jax.ShapeDtypeStruct((b, s, B_WIDTH), BF16),
        scratch_shapes=[pltpu.VMEM((nb, B_WIDTH), F32)],
        compiler_params=_cparams("parallel", "arbitrary"),
    )(q, k, v)


def _split3(x):
    hi = x.astype(BF16)
    r1 = x - hi.astype(F32)
    mid = r1.astype(BF16)
    lo = (r1 - mid.astype(F32)).astype(BF16)
    return hi, mid, lo


def _dot_tn(a, b):
    return lax.dot_general(a, b, (((0,), (0,)), ((), ())), preferred_element_type=F32)


def _wkv_chunk(r, ld, k, v, kkr, a, states, c):
    heads = range(len(states))
    hs = lambda x, h: x[:, h * C_HD:(h + 1) * C_HD]
    row = lax.broadcasted_iota(I32, (c, c), 0)
    col = lax.broadcasted_iota(I32, (c, c), 1)
    strict = row > col
    incl = row >= col
    incl16 = jnp.where(incl, 1.0, 0.0).astype(BF16)
    eye = jnp.where(row == col, 1.0, 0.0)
    hi, mid, lo = _split3(ld)
    cum = _dot(incl16, hi) + _dot(incl16, mid) + _dot(incl16, lo)
    g_inc = jnp.exp(cum)
    g_exc = jnp.exp(cum - ld)
    g_inv = jnp.exp(-cum)
    g_end = g_inc[c - 1:c, :]
    rt = r * g_inc
    kt = k * g_inv
    k_end = kt * g_end

    kk = []
    for h in heads:
        x = hs(kkr, h)
        kk.append(x / jnp.maximum(jnp.sqrt(jnp.sum(x * x, axis=-1, keepdims=True)), 1e-12))
    at16 = [(-kk[h] * hs(g_exc, h)).astype(BF16) for h in heads]
    bt = [kk[h] * hs(a, h) * hs(g_inv, h) for h in heads]
    ar16 = [jnp.concatenate([at16[h], hs(rt, h).astype(BF16)], axis=0) for h in heads]
    bk16 = [jnp.concatenate([bt[h], hs(kt, h)], axis=0).astype(BF16) for h in heads]
    gram = [_dot_nt(ar16[h], bk16[h]) for h in heads]
    l_ab = [jnp.where(strict, gram[h][:c, :c], 0.0) for h in heads]
    l_ak16 = [jnp.where(strict, gram[h][:c, c:], 0.0).astype(BF16) for h in heads]
    l_rb16 = [jnp.where(incl, gram[h][c:, :c], 0.0).astype(BF16) for h in heads]
    l_rk16 = [jnp.where(incl, gram[h][c:, c:], 0.0).astype(BF16) for h in heads]
    inv = [eye + l_ab[h] for h in heads]
    lp16 = [l_ab[h].astype(BF16) for h in heads]
    n = 2
    while n < c:
        lp16 = [_dot(lp16[h], lp16[h]).astype(BF16) for h in heads]
        inv = [inv[h] + _dot(inv[h].astype(BF16), lp16[h]) for h in heads]
        n *= 2
    inv16 = [inv[h].astype(BF16) for h in heads]
    v16 = [hs(v, h).astype(BF16) for h in heads]
    lakv16 = [_dot(l_ak16[h], v16[h]).astype(BF16) for h in heads]
    w16 = [_dot(inv16[h], at16[h]).astype(BF16) for h in heads]
    u0_16 = [_dot(inv16[h], lakv16[h]).astype(BF16) for h in heads]
    r2_16 = [(hs(rt, h) + _dot(l_rb16[h], w16[h])).astype(BF16) for h in heads]
    y0 = [_dot(l_rb16[h], u0_16[h]) + _dot(l_rk16[h], v16[h]) for h in heads]
    bend16 = [(bt[h] * hs(g_end, h)).astype(BF16) for h in heads]
    kend16 = [hs(k_end, h).astype(BF16) for h in heads]
    m2_16 = [_dot_tn(w16[h], bend16[h]).astype(BF16) for h in heads]
    nmat = [_dot_tn(u0_16[h], bend16[h]) + _dot_tn(v16[h], kend16[h]) for h in heads]
    s16 = [states[h].astype(BF16) for h in heads]
    ys = [_dot_nt(r2_16[h], s16[h]) + y0[h] for h in heads]
    new_states = [states[h] * hs(g_end, h) + _dot(s16[h], m2_16[h]) + nmat[h] for h in heads]
    return ys, new_states


def _wkv_kernel(r_ref, ld_ref, k_ref, v_ref, kkr_ref, a_ref, g_ref, lnw_ref, lnb_ref, rk_ref, s0_ref,
                o_ref, sfin_ref, state_ref, *, chunk):
    tb = r_ref.shape[0]
    t_blk = pl.program_id(1)
    hs = lambda x, h: x[:, h * C_HD:(h + 1) * C_HD]

    @pl.when(t_blk == 0)
    def _():
        state_ref[...] = s0_ref[...]

    def chunk_step(ci, carry):
        rows = pl.ds(pl.multiple_of(ci * chunk, chunk), chunk)
        r = r_ref[rows, :]
        k = k_ref[rows, :]
        v = v_ref[rows, :]
        states = [state_ref[h] for h in range(C_HEADS)]
        ys, new_states = _wkv_chunk(r, ld_ref[rows, :], k, v, kkr_ref[rows, :], a_ref[rows, :], states, chunk)
        for h in range(C_HEADS):
            state_ref[h] = new_states[h]
        rkr = r * k * rk_ref[...]
        outs = []
        for h in range(C_HEADS):
            y = ys[h]
            mean = jnp.mean(y, axis=-1, keepdims=True)
            var = jnp.mean(jnp.square(y - mean), axis=-1, keepdims=True)
            yn = (y - mean) * lax.rsqrt(var + GN_EPS)
            bonus = jnp.sum(hs(rkr, h), axis=-1, keepdims=True) * hs(v, h)
            outs.append(yn * lnw_ref[:, h * C_HD:(h + 1) * C_HD] + lnb_ref[:, h * C_HD:(h + 1) * C_HD] + bonus)
        o_ref[rows, :] = (jnp.concatenate(outs, axis=1) * g_ref[rows, :]).astype(o_ref.dtype)
        return carry

    lax.fori_loop(0, tb // chunk, chunk_step, 0)

    @pl.when(t_blk == pl.num_programs(1) - 1)
    def _():
        sfin_ref[...] = state_ref[...]


def wkv7(r, ld, k, v, kkr, a, g, ln_w, ln_b, r_k, s0, chunk, tb):
    b, s, w = r.shape
    blk = pl.BlockSpec((None, tb, w), lambda bi, i: (bi, i, 0))
    par = pl.BlockSpec((1, w), lambda bi, i: (0, 0))
    st = pl.BlockSpec((None, C_HEADS, C_HD, C_HD), lambda bi, i: (bi, 0, 0, 0))
    return pl.pallas_call(
        functools.partial(_wkv_kernel, chunk=chunk),
        grid=(b, s // tb),
        in_specs=[blk] * 7 + [par] * 3 + [st],
        out_specs=[blk, st],
        out_shape=[jax.ShapeDtypeStruct((b, s, w), BF16),
                   jax.ShapeDtypeStruct((b, C_HEADS, C_HD, C_HD), F32)],
        scratch_shapes=[pltpu.VMEM((C_HEADS, C_HD, C_HD), F32)],
        compiler_params=_cparams("parallel", "arbitrary"),
    )(r, ld, k, v, kkr, a, g, ln_w.reshape(1, w), ln_b.reshape(1, w), r_k.reshape(1, w), s0)


def _split2(x):
    hi = x.astype(BF16)
    return hi, (x - hi.astype(F32)).astype(BF16)


def _page_spec(rows, width, layer, slot, n_pages):
    def index_map(b, j, pt):
        return (layer, pt[b, jnp.minimum(j * SAMPLE_PAGES_PER_STEP + slot, n_pages - 1)], 0, 0)
    return pl.BlockSpec((None, None, rows, width), index_map)


def _idx_sample_kernel(pt_ref, qi_ref, w_ref, knew_ref, *rest, t):
    pages, o_ref = rest[:-1], rest[-1]
    j = pl.program_id(1)
    last = pl.num_programs(1) - 1
    q_hi, q_lo = _split2(qi_ref[...])
    w = w_ref[...]

    def score(kpage):
        k_hi, k_lo = _split2(kpage)
        s = _dot_nt(q_hi, k_hi) + _dot_nt(q_hi, k_lo) + _dot_nt(q_lo, k_hi)
        s = jnp.maximum(s, 0.0) * w
        return jnp.sum(s.reshape(t, IDX_HEADS, PAGE_SIZE), axis=1)

    @pl.when(j < last)
    def _():
        for i, page in enumerate(pages):
            o_ref[:, i * PAGE_SIZE:(i + 1) * PAGE_SIZE] = score(page[...])

    @pl.when(j == last)
    def _():
        o_ref[...] = jnp.zeros(o_ref.shape, F32)
        o_ref[:, 0:PAGE_SIZE] = score(knew_ref[...])


def idx_sample(page_table, qi, w, knew, cache_idx, layer):
    db, rows, _ = qi.shape
    t = rows // IDX_HEADS
    n_pages = page_table.shape[1]
    pg = SAMPLE_PAGES_PER_STEP
    steps = n_pages // pg + 1
    per_b = lambda b, j, pt: (b, 0, 0)
    return pl.pallas_call(
        functools.partial(_idx_sample_kernel, t=t),
        grid_spec=pltpu.PrefetchScalarGridSpec(
            num_scalar_prefetch=1,
            grid=(db, steps),
            in_specs=[pl.BlockSpec((None, rows, IDX_HD), per_b),
                      pl.BlockSpec((None, rows, 1), per_b),
                      pl.BlockSpec((None, PAGE_SIZE, IDX_HD), per_b)]
                     + [_page_spec(PAGE_SIZE, IDX_HD, layer, i, n_pages) for i in range(pg)],
            out_specs=pl.BlockSpec((None, t, pg * PAGE_SIZE), lambda b, j, pt: (b, 0, j))),
        out_shape=jax.ShapeDtypeStruct((db, t, steps * pg * PAGE_SIZE), F32),
        compiler_params=_cparams("parallel", "arbitrary"),
    )(page_table, qi, w, knew, *([cache_idx] * pg))


def _select_sample_kernel(isc_ref, o_ref, *, topk, t, past_len):
    rows, n = isc_ref.shape
    pos = lax.broadcasted_iota(I32, (rows, n), 1)
    r = pl.program_id(0) * rows + lax.broadcasted_iota(I32, (rows, n), 0)
    causal = pos <= past_len + (r & (t - 1))
    key = _sortable_key(jnp.where(causal, isc_ref[...], NEG_INF))
    mask = _topk_mask(key, pos, topk, n.bit_length()) & causal
    o_ref[...] = jnp.where(mask, 1.0, 0.0)


def select_sample(isc, topk, t, past_len):
    rows, n = isc.shape
    tr = 16
    assert t & (t - 1) == 0 and rows % tr == 0
    return pl.pallas_call(
        functools.partial(_select_sample_kernel, topk=topk, t=t, past_len=past_len),
        grid=(rows // tr,),
        in_specs=[pl.BlockSpec((tr, n), lambda i: (i, 0))],
        out_specs=pl.BlockSpec((tr, n), lambda i: (i, 0)),
        out_shape=jax.ShapeDtypeStruct((rows, n), F32),
        compiler_params=_cparams("parallel"),
    )(isc)


def _own_head(shape, heads):
    return ((lax.broadcasted_iota(I32, shape, 1) & (heads - 1))
            == (lax.broadcasted_iota(I32, shape, 0) & (heads - 1)))


def _dsa_sample_kernel(pt_ref, q_ref, mask_ref, knew_ref, vnew_ref, *rest, heads):
    pg = SAMPLE_PAGES_PER_STEP
    kpages, vpages = rest[:pg], rest[pg:2 * pg]
    o_ref, m_ref, l_ref, acc_ref = rest[2 * pg:]
    j = pl.program_id(1)
    last = pl.num_programs(1) - 1
    q = q_ref[...]
    page_rows = PAGE_SIZE * heads
    shift = heads.bit_length() - 1
    spread = jnp.where(
        jnp.right_shift(lax.broadcasted_iota(I32, (PAGE_SIZE, page_rows), 1), shift)
        == lax.broadcasted_iota(I32, (PAGE_SIZE, page_rows), 0), 1.0, 0.0).astype(BF16)

    @pl.when(j == 0)
    def _():
        m_ref[...] = jnp.full(m_ref.shape, NEG_INF, F32)
        l_ref[...] = jnp.zeros(l_ref.shape, F32)
        acc_ref[...] = jnp.zeros(acc_ref.shape, F32)

    def update(ks, vs, mask):
        s = jnp.concatenate([_dot_nt(q, kp.astype(BF16)) for kp in ks], axis=1) * (A_HD ** -0.5)
        sel = jnp.concatenate([_dot(mask[:, i * PAGE_SIZE:(i + 1) * PAGE_SIZE].astype(BF16), spread)
                               for i in range(len(ks))], axis=1)
        s = jnp.where((sel > 0.5) & _own_head(s.shape, heads), s, NEG_INF)
        m_old = m_ref[...]
        m_new = jnp.maximum(m_old, jnp.max(s, axis=-1, keepdims=True))
        m_safe = jnp.where(m_new == NEG_INF, 0.0, m_new)
        p = jnp.exp(s - m_safe)
        alpha = jnp.exp(m_old - m_safe)
        l_ref[...] = alpha * l_ref[...] + jnp.sum(p, axis=-1, keepdims=True)
        acc = alpha * acc_ref[...]
        for i, vp in enumerate(vs):
            acc += _dot(p[:, i * page_rows:(i + 1) * page_rows].astype(BF16), vp.astype(BF16))
        acc_ref[...] = acc
        m_ref[...] = m_new

    @pl.when(j < last)
    def _():
        update([kp[...] for kp in kpages], [vp[...] for vp in vpages], mask_ref[...])

    @pl.when(j == last)
    def _():
        update([knew_ref[...]], [vnew_ref[...]], mask_ref[:, 0:PAGE_SIZE])
        o_ref[...] = acc_ref[...] / l_ref[...]


def dsa_sample(page_table, q, mask, knew, vnew, cache_k, cache_v, layer, heads):
    db, rows, hd = q.shape
    n_pages = page_table.shape[1]
    pg = SAMPLE_PAGES_PER_STEP
    steps = n_pages // pg + 1
    page_rows = PAGE_SIZE * heads
    per_b = lambda b, j, pt: (b, 0, 0)
    return pl.pallas_call(
        functools.partial(_dsa_sample_kernel, heads=heads),
        grid_spec=pltpu.PrefetchScalarGridSpec(
            num_scalar_prefetch=1,
            grid=(db, steps),
            in_specs=[pl.BlockSpec((None, rows, hd), per_b),
                      pl.BlockSpec((None, rows, pg * PAGE_SIZE), lambda b, j, pt: (b, 0, j)),
                      pl.BlockSpec((None, page_rows, hd), per_b),
                      pl.BlockSpec((None, page_rows, hd), per_b)]
                     + [_page_spec(page_rows, hd, layer, i, n_pages) for i in range(pg)]
                     + [_page_spec(page_rows, hd, layer, i, n_pages) for i in range(pg)],
            out_specs=pl.BlockSpec((None, rows, hd), per_b),
            scratch_shapes=[pltpu.VMEM((rows, 1), F32), pltpu.VMEM((rows, 1), F32),
                            pltpu.VMEM((rows, hd), F32)]),
        out_shape=jax.ShapeDtypeStruct((db, rows, hd), F32),
        compiler_params=_cparams("parallel", "arbitrary"),
    )(page_table, q, mask, knew, vnew, *([cache_k] * pg), *([cache_v] * pg))


def _moba_sample_kernel(pt_ref, q_ref, knew_ref, vnew_ref, *rest, heads, ksel):
    pg = SAMPLE_PAGES_PER_STEP
    ppb = MOBA_BLOCK // PAGE_SIZE
    kpages, vpages = rest[:pg], rest[pg:2 * pg]
    o_ref, g_ref, m_ref, l_ref, ob_ref = rest[2 * pg:]
    j = pl.program_id(1)
    last = pl.num_programs(1) - 1
    q = q_ref[...]
    q16 = q.astype(BF16)
    rows, hd = q.shape
    scale = B_HD ** -0.5
    sublanes = 8
    assert 2 * heads == sublanes and hd == LANES and rows % sublanes == 0
    wide = lambda x: jnp.broadcast_to(x, (rows, LANES))

    def partial_softmax(s, v16):
        m = jnp.max(s, axis=-1, keepdims=True)
        p = jnp.exp(s - m)
        return m, jnp.sum(p, axis=-1, keepdims=True), _dot(p.astype(BF16), v16)

    @pl.when(j < last)
    def _():
        for blk in range(pg // ppb):
            kblk = jnp.concatenate([kpages[blk * ppb + i][...] for i in range(ppb)], axis=0)
            v16 = jnp.concatenate([vpages[blk * ppb + i][...] for i in range(ppb)], axis=0).astype(BF16)
            s = _dot_nt(q16, kblk.astype(BF16)) * scale
            m, l, o = partial_softmax(jnp.where(_own_head(s.shape, heads), s, NEG_INF), v16)
            n = j * (pg // ppb) + blk
            fold = jnp.sum(kblk.reshape(kblk.shape[0] // sublanes, sublanes, hd), axis=0)
            kmean = (fold + pltpu.roll(fold, heads, 0)) * (1.0 / MOBA_BLOCK)
            kmean = jnp.concatenate([kmean] * (rows // sublanes), axis=0)
            g_ref[n] = wide(jnp.sum(q * kmean, axis=-1, keepdims=True))
            m_ref[n] = wide(m)
            l_ref[n] = wide(l)
            ob_ref[n] = o

    @pl.when(j == last)
    def _():
        shift = heads.bit_length() - 1
        s = _dot_nt(q16, knew_ref[...].astype(BF16)) * scale
        causal = (jnp.right_shift(lax.broadcasted_iota(I32, s.shape, 1), shift)
                  <= jnp.right_shift(lax.broadcasted_iota(I32, s.shape, 0), shift))
        s = jnp.where(causal & _own_head(s.shape, heads), s, NEG_INF)
        m_own, l_own, o_own = partial_softmax(s, vnew_ref[...].astype(BF16))
        m_own, l_own = wide(m_own), wide(l_own)
        gates = g_ref[...]
        nidx = lax.broadcasted_iota(I32, gates.shape, 0)
        chosen = jnp.zeros(gates.shape, F32)
        for _ in range(ksel):
            best = jnp.max(gates, axis=0, keepdims=True)
            first = jnp.min(jnp.where(gates == best, nidx, gates.shape[0]), axis=0, keepdims=True)
            pick = nidx == first
            chosen = jnp.where(pick, 1.0, chosen)
            gates = jnp.where(pick, NEG_INF, gates)
        m_blk = m_ref[...]
        m_tot = jnp.maximum(m_own, jnp.max(jnp.where(chosen > 0.5, m_blk, NEG_INF), axis=0))
        wgt = jnp.where(chosen > 0.5, jnp.exp(m_blk - m_tot[None]), 0.0)
        w_own = jnp.exp(m_own - m_tot)
        l_tot = w_own * l_own + jnp.sum(wgt * l_ref[...], axis=0)
        acc = w_own * o_own
        for n in range(gates.shape[0]):
            acc += wgt[n] * ob_ref[n]
        o_ref[...] = acc / l_tot


def moba_sample(page_table, q, knew, vnew, cache_k, cache_v, layer, heads):
    db, rows, hd = q.shape
    n_pages = page_table.shape[1]
    pg = SAMPLE_PAGES_PER_STEP
    steps = n_pages // pg + 1
    nbp = n_pages * PAGE_SIZE // MOBA_BLOCK
    page_rows = PAGE_SIZE * heads
    per_b = lambda b, j, pt: (b, 0, 0)
    return pl.pallas_call(
        functools.partial(_moba_sample_kernel, heads=heads, ksel=max(1, min(MOBA_TOPK, nbp))),
        grid_spec=pltpu.PrefetchScalarGridSpec(
            num_scalar_prefetch=1,
            grid=(db, steps),
            in_specs=[pl.BlockSpec((None, rows, hd), per_b),
                      pl.BlockSpec((None, page_rows, hd), per_b),
                      pl.BlockSpec((None, page_rows, hd), per_b)]
                     + [_page_spec(page_rows, hd, layer, i, n_pages) for i in range(pg)]
                     + [_page_spec(page_rows, hd, layer, i, n_pages) for i in range(pg)],
            out_specs=pl.BlockSpec((None, rows, hd), per_b),
            scratch_shapes=[pltpu.VMEM((nbp, rows, LANES), F32), pltpu.VMEM((nbp, rows, LANES), F32),
                            pltpu.VMEM((nbp, rows, LANES), F32), pltpu.VMEM((nbp, rows, hd), F32)]),
        out_shape=jax.ShapeDtypeStruct((db, rows, hd), F32),
        compiler_params=_cparams("parallel", "arbitrary"),
    )(page_table, q, knew, vnew, *([cache_k] * pg), *([cache_v] * pg))


def _new_page(x, rows):
    return jnp.pad(x, ((0, 0), (0, rows - x.shape[1]), (0, 0)))


def _sample_attention(qa, ka, va, qi, ki, wi, qb, kb, vb, caches, page_table, layer):
    cache_a_k, cache_a_v, cache_a_idx, cache_b_k, cache_b_v = caches
    db, t = qa.shape[:2]
    past_len = page_table.shape[1] * PAGE_SIZE
    topk = min(A_TOPK_MAX, (past_len + t) // 4)
    isc = idx_sample(page_table, qi.reshape(db, t * IDX_HEADS, IDX_HD),
                     wi.reshape(db, t * IDX_HEADS, 1), _new_page(ki, PAGE_SIZE), cache_a_idx, layer)
    n = isc.shape[-1]
    mask = select_sample(isc.reshape(db * t, n), topk, t, past_len).reshape(db, t, n)
    rows_a = lambda z: z.reshape(db, t * A_HEADS, A_HD)
    rows_b = lambda z: z.reshape(db, t * B_HEADS, B_HD)
    oa = dsa_sample(page_table, rows_a(qa).astype(BF16), jnp.repeat(mask, A_HEADS, axis=1),
                    _new_page(rows_a(ka), PAGE_SIZE * A_HEADS), _new_page(rows_a(va), PAGE_SIZE * A_HEADS),
                    cache_a_k, cache_a_v, layer, A_HEADS)
    ob = moba_sample(page_table, rows_b(qb), _new_page(rows_b(kb), PAGE_SIZE * B_HEADS),
                     _new_page(rows_b(vb), PAGE_SIZE * B_HEADS), cache_b_k, cache_b_v, layer, B_HEADS)
    return oa.reshape(db, t, A_WIDTH), ob.reshape(db, t, B_WIDTH)


def _rms_norm(x, g):
    y = x * lax.rsqrt(jnp.mean(x * x, axis=-1, keepdims=True) + NORM_EPS)
    return y * g


def _rope_partial(x, pos):
    hd = x.shape[-1]
    rot = hd // 4
    half = rot // 2
    inv = ROPE_THETA ** (-2.0 * jnp.arange(half, dtype=F32) / rot)
    ang = pos.astype(F32)[:, None] * inv[None, :]
    cos = jnp.cos(ang)[:, None, :]
    sin = jnp.sin(ang)[:, None, :]
    x1, x2 = x[..., :half], x[..., half:rot]
    return jnp.concatenate([x1 * cos - x2 * sin, x2 * cos + x1 * sin, x[..., rot:]], axis=-1)


def _prep_a(pa, pos, q_g, k_g, ik_g):
    b, s = pa.shape[:2]
    o = 0
    q = pa[..., o:o + A_WIDTH]; o += A_WIDTH
    k = pa[..., o:o + A_WIDTH]; o += A_WIDTH
    v = pa[..., o:o + A_WIDTH]; o += A_WIDTH
    qi = pa[..., o:o + IDX_HEADS * IDX_HD]; o += IDX_HEADS * IDX_HD
    ki = pa[..., o:o + IDX_HD]; o += IDX_HD
    wi = pa[..., o:o + IDX_HEADS]
    q = _rope_partial(_rms_norm(q.reshape(b, s, A_HEADS, A_HD), q_g), pos)
    k = _rope_partial(_rms_norm(k.reshape(b, s, A_HEADS, A_HD), k_g), pos)
    v = v.reshape(b, s, A_HEADS, A_HD)
    qi = _rope_partial(qi.reshape(b, s, IDX_HEADS, IDX_HD), pos)
    ki = _rope_partial(_rms_norm(ki, ik_g)[:, :, None, :], pos)[:, :, 0, :]
    return q, k, v, qi, ki, wi * IDX_W_SCALE


def _prep_b(pb, pos, q_g, k_g):
    b, s = pb.shape[:2]
    q = pb[..., :B_WIDTH]
    k = pb[..., B_WIDTH:2 * B_WIDTH]
    v = pb[..., 2 * B_WIDTH:]
    q = _rope_partial(_rms_norm(q.reshape(b, s, B_HEADS, B_HD), q_g), pos)
    k = _rope_partial(_rms_norm(k.reshape(b, s, B_HEADS, B_HD), k_g), pos)
    return q, k, v.reshape(b, s, B_HEADS, B_HD)


def _rwkv_pre(pc, prev, mu, w0, w2, a0, a2, g2, k_k, k_a):
    shifted = jnp.concatenate([prev[:, None], pc[:, :-1]], axis=1)
    xs = pc + mu * (shifted - pc)
    o = 0
    r = xs[..., o:o + C_WIDTH]; o += C_WIDTH
    k = xs[..., o:o + C_WIDTH]; o += C_WIDTH
    v = xs[..., o:o + C_WIDTH]; o += C_WIDTH
    wl = xs[..., o:o + W_LORA]; o += W_LORA
    al = xs[..., o:o + A_LORA]; o += A_LORA
    gl = xs[..., o:o + G_LORA]
    w = -jax.nn.softplus(-(w0 + jnp.tanh(wl) @ w2)) - 0.5
    a = jax.nn.sigmoid(a0 + al @ a2)
    g = jax.nn.sigmoid(gl) @ g2
    kkr = k * k_k
    k = k * (1.0 + (a - 1.0) * k_a)
    return r, -jnp.exp(w), k, v, kkr, a, g


def _pack_w_in(w):
    d = w.shape[0]
    return jnp.concatenate([w[:, :A_COLS], jnp.zeros((d, A_PAD), w.dtype), w[:, A_COLS:],
                            jnp.zeros((d, P_PACK - C_OFF - C_COLS), w.dtype)], axis=1).astype(BF16)


def _layer(x, pos, prev_shift, wkv0, lw, sample_ctx):
    b, s, d = x.shape
    m = b * s
    tm = min(m, 1024)
    p = norm_matmul(x.reshape(m, d), lw['norm1_g'], lw['w_in'], tm, 512).reshape(b, s, P_PACK)
    pa = p[..., :A_COLS]
    pb = p[..., B_OFF:B_OFF + B_COLS]
    pc = p[..., C_OFF:C_OFF + C_COLS]
    qa, ka, va, qi, ki, wi = _prep_a(pa, pos, lw['a_q_norm'], lw['a_k_norm'], lw['idx_k_norm'])
    qb, kb, vb = _prep_b(pb, pos, lw['b_q_norm'], lw['b_k_norm'])
    if sample_ctx is None:
        flat = lambda z: z.reshape(b, s, -1).astype(BF16)
        oa = dsa_prompt(flat(qa), flat(ka), flat(va), flat(qi), ki.astype(BF16), wi)
        ob = moba_prompt(flat(qb), flat(kb), flat(vb))
        pad = 0
    else:
        caches, page_table, layer = sample_ctx
        oa, ob = _sample_attention(qa, ka, va, qi, ki, wi, qb, kb, vb, caches, page_table, layer)
        oa = oa.astype(BF16)
        ob = ob.astype(BF16)
        pad = -s % 16
    parts = _rwkv_pre(pc, prev_shift, lw['c_mu'], lw['c_w0'], lw['c_w2'], lw['c_a0'], lw['c_a2'],
                      lw['c_g2'], lw['c_k_k'], lw['c_k_a'])
    if pad:
        parts = [jnp.pad(z, ((0, 0), (0, pad), (0, 0))) for z in parts]
    sp = s + pad
    chunk = min(WKV_CHUNK, sp)
    oc, wkv_fin = wkv7(*parts, lw['c_ln_w'], lw['c_ln_b'], lw['c_r_k'].reshape(-1), wkv0,
                       chunk, min(sp, 4 * chunk))
    oc = oc[:, :s]
    tm2 = min(m, 512)
    x2 = out_proj(x.reshape(m, d), oa.reshape(m, A_WIDTH), ob.reshape(m, B_WIDTH),
                  oc.reshape(m, C_WIDTH), lw['w_o'], tm2)
    x3 = mlp(x2, lw['norm2_g'], lw['w_ff1'], lw['w_ff2'], tm2, 512).reshape(b, s, d)
    return x3, (ka, va, ki, kb, vb, wkv_fin, pc[:, -1])


def kernel(x_prompt, x_sample, cache_a_k, cache_a_v, cache_a_idx, cache_b_k, cache_b_v, state_wkv, state_shift, page_table, norm1_g, w_in, a_q_norm, a_k_norm, idx_k_norm, b_q_norm, b_k_norm, c_mu, c_w0, c_w2, c_a0, c_a2, c_g2, c_k_k, c_k_a, c_r_k, c_ln_w, c_ln_b, w_o, norm2_g, w_ff1, w_ff2):
    depth = w_in.shape[0]
    bp = x_prompt.shape[0]
    pos_p = jnp.arange(x_prompt.shape[1])
    pos_s = page_table.shape[1] * PAGE_SIZE + jnp.arange(x_sample.shape[1])
    shift0 = jnp.zeros((bp, C_COLS), x_prompt.dtype)
    wkv0 = jnp.zeros((bp, C_HEADS, C_HD, C_HD), F32)
    xp, xs = x_prompt, x_sample
    by_row = lambda c: c.reshape(c.shape[0], c.shape[1], -1, c.shape[-1])
    caches = (by_row(cache_a_k), by_row(cache_a_v), cache_a_idx, by_row(cache_b_k), by_row(cache_b_v))
    p_rows, s_rows = [], []
    for l in range(depth):
        lw = dict(norm1_g=norm1_g[l], w_in=_pack_w_in(w_in[l]), a_q_norm=a_q_norm[l], a_k_norm=a_k_norm[l],
                  idx_k_norm=idx_k_norm[l], b_q_norm=b_q_norm[l], b_k_norm=b_k_norm[l], c_mu=c_mu[l],
                  c_w0=c_w0[l], c_w2=c_w2[l], c_a0=c_a0[l], c_a2=c_a2[l], c_g2=c_g2[l], c_k_k=c_k_k[l],
                  c_k_a=c_k_a[l], c_r_k=c_r_k[l], c_ln_w=c_ln_w[l], c_ln_b=c_ln_b[l],
                  w_o=w_o[l].astype(BF16), norm2_g=norm2_g[l], w_ff1=w_ff1[l].astype(BF16),
                  w_ff2=w_ff2[l].astype(BF16))
        xp, rows = _layer(xp, pos_p, shift0, wkv0, lw, None)
        p_rows.append(rows)
        xs, rows = _layer(xs, pos_s, state_shift[l], state_wkv[l], lw, (caches, page_table, l))
        s_rows.append(rows)
    p_out = [jnp.stack(z) for z in zip(*p_rows)]
    s_out = [jnp.stack(z) for z in zip(*s_rows)]
    return (xp, xs, *p_out, *s_out)
```

```python
import functools

import jax
import jax.numpy as jnp
from jax import lax
from jax.experimental import pallas as pl
from jax.experimental.pallas import tpu as pltpu

F32 = jnp.float32
BF16 = jnp.bfloat16
I32 = jnp.int32

VMEM_LIMIT_BYTES = 56 * 1024 * 1024
LANES = 128

PAST_LEN = 8192
PAGE_SIZE = 128
A_HEADS = 4
A_HD = 128
A_WIDTH = A_HEADS * A_HD
IDX_HEADS = 16
IDX_HD = 64
A_TOPK_MAX = 256
A_Q_BLOCK = 128
B_HEADS = 4
B_HD = 128
B_WIDTH = B_HEADS * B_HD
MOBA_BLOCK = 256
MOBA_TOPK = 3
C_HD = 64
C_HEADS = 16
C_WIDTH = C_HEADS * C_HD
W_LORA = 64
A_LORA = 64
G_LORA = 160
A_COLS = 3 * A_WIDTH + IDX_HEADS * IDX_HD + IDX_HD + IDX_HEADS
B_COLS = 3 * B_WIDTH
C_COLS = 3 * C_WIDTH + W_LORA + A_LORA + G_LORA
ROPE_THETA = 500000.0
NORM_EPS = 1e-6
GN_EPS = 64e-5
IDX_W_SCALE = (IDX_HEADS * IDX_HD) ** -0.5
WKV_CHUNK = 64
SAMPLE_PAGES_PER_STEP = 16
DSA_KEY_BUCKET = 512
PROJ_TILE_N = 512

C_BLOCK = C_COLS + (-C_COLS % LANES)
A_OFF = C_BLOCK
B_OFF = A_OFF + A_COLS + (-A_COLS % LANES)
P_PACK = B_OFF + B_COLS
assert P_PACK % PROJ_TILE_N == 0
LORA_OFF = 3 * C_WIDTH
LORA_BLOCK = C_BLOCK - LORA_OFF
A_QI_OFF = A_OFF + 3 * A_WIDTH
A_KI_OFF = A_QI_OFF + IDX_HEADS * IDX_HD
assert A_KI_OFF + LANES == B_OFF and IDX_HD + IDX_HEADS <= LANES

NEG_INF = float("-inf")
INT_MIN = -2 ** 31


def _cparams(*sem):
    return pltpu.CompilerParams(dimension_semantics=sem, vmem_limit_bytes=VMEM_LIMIT_BYTES)


def _dot_nt(a, b):
    return lax.dot_general(a, b, (((1,), (1,)), ((), ())), preferred_element_type=F32)


def _dot(a, b):
    return jnp.dot(a, b, preferred_element_type=F32)


def _norm_matmul_kernel(x_ref, g_ref, w_ref, o_ref, xn_ref):
    @pl.when(pl.program_id(1) == 0)
    def _():
        x = x_ref[...]
        y = x * lax.rsqrt(jnp.mean(x * x, axis=-1, keepdims=True) + NORM_EPS)
        xn_ref[...] = (y * g_ref[...]).astype(BF16)

    o_ref[...] = _dot(xn_ref[...], w_ref[...])


def norm_matmul(x, g, w, tm, tn):
    m, d = x.shape
    p = w.shape[1]
    return pl.pallas_call(
        _norm_matmul_kernel,
        grid=(m // tm, p // tn),
        in_specs=[pl.BlockSpec((tm, d), lambda i, j: (i, 0)),
                  pl.BlockSpec((1, d), lambda i, j: (0, 0)),
                  pl.BlockSpec((d, tn), lambda i, j: (0, j))],
        out_specs=pl.BlockSpec((tm, tn), lambda i, j: (i, j)),
        out_shape=jax.ShapeDtypeStruct((m, p), F32),
        scratch_shapes=[pltpu.VMEM((tm, d), BF16)],
        compiler_params=_cparams("parallel", "arbitrary"),
    )(x, g.reshape(1, d), w)


def _out_proj_kernel(x_ref, oa_ref, ob_ref, oc_ref, w_ref, o_ref):
    wa = oa_ref.shape[1]
    wb = ob_ref.shape[1]
    acc = _dot(oa_ref[...], w_ref[0:wa, :])
    acc += _dot(ob_ref[...], w_ref[wa:wa + wb, :])
    acc += _dot(oc_ref[...], w_ref[wa + wb:, :])
    o_ref[...] = x_ref[...] + acc


def out_proj(x, oa, ob, oc, w, tm):
    m, d = x.shape
    row = lambda i: (i, 0)
    return pl.pallas_call(
        _out_proj_kernel,
        grid=(m // tm,),
        in_specs=[pl.BlockSpec((tm, d), row),
                  pl.BlockSpec((tm, oa.shape[1]), row),
                  pl.BlockSpec((tm, ob.shape[1]), row),
                  pl.BlockSpec((tm, oc.shape[1]), row),
                  pl.BlockSpec(w.shape, lambda i: (0, 0))],
        out_specs=pl.BlockSpec((tm, d), row),
        out_shape=jax.ShapeDtypeStruct((m, d), F32),
        compiler_params=_cparams("parallel"),
    )(x, oa, ob, oc, w)


def _mlp_kernel(x_ref, g_ref, w1_ref, w2_ref, o_ref, xn_ref):
    @pl.when(pl.program_id(1) == 0)
    def _():
        x = x_ref[...]
        y = x * lax.rsqrt(jnp.mean(x * x, axis=-1, keepdims=True) + NORM_EPS)
        xn_ref[...] = (y * g_ref[...]).astype(BF16)
        o_ref[...] = x

    h = jnp.maximum(_dot(xn_ref[...], w1_ref[...]), 0.0)
    o_ref[...] += _dot((h * h).astype(BF16), w2_ref[...])


def mlp(x, g, w1, w2, tm, tf):
    m, d = x.shape
    f = w1.shape[1]
    return pl.pallas_call(
        _mlp_kernel,
        grid=(m // tm, f // tf),
        in_specs=[pl.BlockSpec((tm, d), lambda i, j: (i, 0)),
                  pl.BlockSpec((1, d), lambda i, j: (0, 0)),
                  pl.BlockSpec((d, tf), lambda i, j: (0, j)),
                  pl.BlockSpec((tf, d), lambda i, j: (j, 0))],
        out_specs=pl.BlockSpec((tm, d), lambda i, j: (i, 0)),
        out_shape=jax.ShapeDtypeStruct((m, d), F32),
        scratch_shapes=[pltpu.VMEM((tm, d), BF16)],
        compiler_params=_cparams("parallel", "arbitrary"),
    )(x, g.reshape(1, d), w1, w2)


def _rope_tables(pos, hd, reps):
    rot = hd // 4
    half = rot // 2
    inv = ROPE_THETA ** (-2.0 * jnp.arange(half, dtype=F32) / rot)
    ang = pos.astype(F32)[:, None] * inv[None, :]
    cos, sin = jnp.cos(ang), jnp.sin(ang)
    zeros = jnp.zeros_like(sin)
    rest = jnp.zeros((pos.shape[0], hd - rot), F32)
    tabs = [jnp.concatenate([cos, cos, rest + 1.0], axis=1),
            jnp.concatenate([zeros, sin, rest], axis=1),
            jnp.concatenate([-sin, zeros, rest], axis=1)]
    return jnp.stack([jnp.tile(z, (1, reps)) for z in tabs])


def _rope(y, tab_ref, half):
    width = y.shape[1]
    return (y * tab_ref[0] + pltpu.roll(y, half, 1) * tab_ref[1]
            + pltpu.roll(y, width - half, 1) * tab_ref[2])


def _prep_kernel(p_ref, ta_ref, ti_ref, g_ref, qa_ref, ka_ref, va_ref, qi_ref, kiwi_ref, qb_ref, kb_ref, vb_ref):
    def normed(x, g):
        return x * lax.rsqrt(jnp.mean(x * x, axis=-1, keepdims=True) + NORM_EPS) * g

    for base, q_ref, k_ref, v_ref, heads, hd, gq, gk in (
            (A_OFF, qa_ref, ka_ref, va_ref, A_HEADS, A_HD, 0, 1),
            (B_OFF, qb_ref, kb_ref, vb_ref, B_HEADS, B_HD, 2, 3)):
        width = heads * hd
        for h in range(heads):
            sl = slice(h * hd, (h + 1) * hd)
            q_ref[:, sl] = _rope(normed(p_ref[:, base + h * hd:base + (h + 1) * hd], g_ref[gq:gq + 1, :]),
                                 ta_ref, hd // 8)
            k_ref[:, sl] = _rope(normed(p_ref[:, base + width + h * hd:base + width + (h + 1) * hd],
                                        g_ref[gk:gk + 1, :]), ta_ref, hd // 8)
        v_ref[...] = p_ref[:, base + 2 * width:base + 3 * width]
    for j in range(IDX_HEADS * IDX_HD // LANES):
        sl = slice(j * LANES, (j + 1) * LANES)
        qi_ref[:, sl] = _rope(p_ref[:, A_QI_OFF + j * LANES:A_QI_OFF + (j + 1) * LANES], ti_ref, IDX_HD // 8)
    x = p_ref[:, A_KI_OFF:A_KI_OFF + LANES]
    is_key = lax.broadcasted_iota(I32, x.shape, 1) < IDX_HD
    ms = jnp.sum(jnp.where(is_key, x * x, 0.0), axis=-1, keepdims=True) * (1.0 / IDX_HD)
    ki = _rope(x * lax.rsqrt(ms + NORM_EPS) * g_ref[4:5, :], ti_ref, IDX_HD // 8)
    kiwi_ref[...] = jnp.where(is_key, ki, x * IDX_W_SCALE)


def prep(p, tab_a, tab_i, gains, tm):
    m = p.shape[0]
    period = tab_a.shape[1] // tm
    row = lambda i: (i, 0)
    tab = pl.BlockSpec((3, tm, LANES), lambda i: (0, i % period, 0))
    widths = (A_WIDTH, A_WIDTH, A_WIDTH, IDX_HEADS * IDX_HD, LANES, B_WIDTH, B_WIDTH, B_WIDTH)
    return pl.pallas_call(
        _prep_kernel,
        grid=(m // tm,),
        in_specs=[pl.BlockSpec((tm, P_PACK), row), tab, tab, pl.BlockSpec(gains.shape, lambda i: (0, 0))],
        out_specs=[pl.BlockSpec((tm, w), row) for w in widths],
        out_shape=[jax.ShapeDtypeStruct((m, w), F32) for w in widths],
        compiler_params=_cparams("parallel"),
    )(p, tab_a, tab_i, gains)


def _sortable_key(x):
    bits = pltpu.bitcast(x, I32)
    return jnp.where(bits >= 0, bits, bits ^ jnp.int32(0x7FFFFFFF))


def _topk_mask(key, pos, topk, n_pos_bits):
    rows = key.shape[0]

    def count(m):
        return jnp.sum(m.astype(I32), axis=-1, keepdims=True)

    def value_step(it, tu):
        cand_u = tu | jnp.left_shift(jnp.int32(1), 31 - it)
        cnt = count(key >= (cand_u ^ jnp.int32(INT_MIN)))
        return jnp.where(cnt >= topk, cand_u, tu)

    tu = lax.fori_loop(0, 32, value_step, jnp.zeros((rows, 1), I32))
    thr = tu ^ jnp.int32(INT_MIN)
    gt = key > thr
    eq = key == thr
    need = topk - count(gt)
    excess = count(eq) - need

    def tie_search(_):
        def pos_step(it, x):
            cand = x | jnp.left_shift(jnp.int32(1), n_pos_bits - 1 - it)
            cnt = count(eq & (pos < cand))
            return jnp.where(cnt < need, cand, x)
        return lax.fori_loop(0, n_pos_bits, pos_step, jnp.zeros((rows, 1), I32))

    last = lax.cond(jnp.max(excess) > 0, tie_search,
                    lambda _: jnp.full((rows, 1), 2 ** n_pos_bits, I32), 0)
    return gt | (eq & (pos <= last))


def _masked_attention(q, k, v, mask, scale):
    s = _dot_nt(q, k) * scale
    s = jnp.where(mask, s, NEG_INF)
    m = jnp.max(s, axis=-1, keepdims=True)
    p = jnp.exp(s - m)
    l = jnp.sum(p, axis=-1, keepdims=True)
    return _dot(p.astype(BF16), v) / l


def _dsa_prompt_kernel(q_ref, k_ref, v_ref, qi_ref, kiwi_ref, kiwi_blk_ref, o_ref, k16_ref, v16_ref, ki16_ref,
                       *, topk, bucket):
    tq = q_ref.shape[0]
    s_len = k_ref.shape[0]
    i = pl.program_id(1)

    @pl.when(i == 0)
    def _():
        k16_ref[...] = k_ref[...].astype(BF16)
        v16_ref[...] = v_ref[...].astype(BF16)
        ki16_ref[...] = kiwi_ref[:, 0:IDX_HD].astype(BF16)

    wi = kiwi_blk_ref[:, IDX_HD:IDX_HD + IDX_HEADS]
    qi16 = qi_ref[...].astype(BF16)
    q16 = q_ref[...].astype(BF16)

    def attend(n_keys):
        qpos = i * tq + lax.broadcasted_iota(I32, (tq, n_keys), 0)
        kpos = lax.broadcasted_iota(I32, (tq, n_keys), 1)
        causal = kpos <= qpos
        ki = ki16_ref[0:n_keys, :]
        isc = jnp.zeros((tq, n_keys), F32)
        for h in range(IDX_HEADS):
            s = _dot_nt(qi16[:, h * IDX_HD:(h + 1) * IDX_HD], ki)
            isc += jnp.maximum(s, 0.0) * wi[:, h:h + 1]
        key = _sortable_key(jnp.where(causal, isc, NEG_INF))
        mask = _topk_mask(key, kpos, topk, (n_keys - 1).bit_length()) & causal
        for h in range(A_HEADS):
            sl = slice(h * A_HD, (h + 1) * A_HD)
            o = _masked_attention(q16[:, sl], k16_ref[0:n_keys, sl], v16_ref[0:n_keys, sl], mask, A_HD ** -0.5)
            o_ref[:, sl] = o.astype(o_ref.dtype)

    for bkt in range(s_len // bucket):
        pl.when(i // (bucket // tq) == bkt)(functools.partial(attend, (bkt + 1) * bucket))


def dsa_prompt(q, k, v, qi, kiwi):
    b, s, _ = q.shape
    tq = A_Q_BLOCK
    topk = min(A_TOPK_MAX, s // 4)
    bucket = min(s, DSA_KEY_BUCKET)
    assert s % bucket == 0 and bucket % tq == 0 and bucket >= topk
    blk = lambda bi, i: (bi, i, 0)
    full = lambda bi, i: (bi, 0, 0)
    return pl.pallas_call(
        functools.partial(_dsa_prompt_kernel, topk=topk, bucket=bucket),
        grid=(b, s // tq),
        in_specs=[pl.BlockSpec((None, tq, A_WIDTH), blk),
                  pl.BlockSpec((None, s, A_WIDTH), full),
                  pl.BlockSpec((None, s, A_WIDTH), full),
                  pl.BlockSpec((None, tq, IDX_HEADS * IDX_HD), blk),
                  pl.BlockSpec((None, s, LANES), full),
                  pl.BlockSpec((None, tq, LANES), blk)],
        out_specs=pl.BlockSpec((None, tq, A_WIDTH), blk),
        out_shape=jax.ShapeDtypeStruct((b, s, A_WIDTH), BF16),
        scratch_shapes=[pltpu.VMEM((s, A_WIDTH), BF16), pltpu.VMEM((s, A_WIDTH), BF16),
                        pltpu.VMEM((s, IDX_HD), BF16)],
        compiler_params=_cparams("parallel", "arbitrary"),
    )(q, k, v, qi, kiwi, kiwi)


def _moba_prompt_kernel(q_ref, k_ref, v_ref, o_ref, kmean_ref, k16_ref, v16_ref, *, ksel):
    tq = q_ref.shape[0]
    s_len, width = k_ref.shape
    nb = s_len // MOBA_BLOCK
    own_id = pl.program_id(1)

    @pl.when(own_id == 0)
    def _():
        k = k_ref[...]
        k16_ref[...] = k.astype(BF16)
        v16_ref[...] = v_ref[...].astype(BF16)
        kmean_ref[...] = jnp.mean(k.reshape(nb, MOBA_BLOCK, width), axis=1)

    q16 = q_ref[...].astype(BF16)
    diagonal = jnp.where(lax.broadcasted_iota(I32, (tq, MOBA_BLOCK), 1)
                         <= lax.broadcasted_iota(I32, (tq, MOBA_BLOCK), 0), 1.0, 0.0)

    def attend(own):
        nidx = lax.broadcasted_iota(I32, (tq, nb), 1)
        for h in range(B_HEADS):
            sl = slice(h * B_HD, (h + 1) * B_HD)
            q = q16[:, sl]
            g = jnp.where(nidx < own, _dot_nt(q, kmean_ref[:, sl].astype(BF16)), NEG_INF)
            rank = jnp.zeros((tq, nb), I32)
            for m in range(own):
                gm = g[:, m:m + 1]
                rank += jnp.where((gm > g) | ((gm == g) & (m < nidx)), 1, 0)
            chosen = jnp.where((nidx < own) & (rank < ksel), 1.0, 0.0)
            mask = jnp.concatenate(
                [jnp.broadcast_to(chosen[:, n:n + 1], (tq, MOBA_BLOCK)) for n in range(own)] + [diagonal], axis=1)
            n_keys = (own + 1) * MOBA_BLOCK
            o = _masked_attention(q, k16_ref[0:n_keys, sl], v16_ref[0:n_keys, sl], mask > 0.5, B_HD ** -0.5)
            o_ref[:, sl] = o.astype(o_ref.dtype)

    for own in range(nb):
        pl.when(own_id == own)(functools.partial(attend, own))


def moba_prompt(q, k, v):
    b, s, _ = q.shape
    nb = s // MOBA_BLOCK
    tq = MOBA_BLOCK
    blk = lambda bi, i: (bi, i, 0)
    full = lambda bi, i: (bi, 0, 0)
    return pl.pallas_call(
        functools.partial(_moba_prompt_kernel, ksel=max(1, min(MOBA_TOPK, nb - 1))),
        grid=(b, nb),
        in_specs=[pl.BlockSpec((None, tq, B_WIDTH), blk),
                  pl.BlockSpec((None, s, B_WIDTH), full),
                  pl.BlockSpec((None, s, B_WIDTH), full)],
        out_specs=pl.BlockSpec((None, tq, B_WIDTH), blk),
        out_shape=jax.ShapeDtypeStruct((b, s, B_WIDTH), BF16),
        scratch_shapes=[pltpu.VMEM((nb, B_WIDTH), F32), pltpu.VMEM((s, B_WIDTH), BF16),
                        pltpu.VMEM((s, B_WIDTH), BF16)],
        compiler_params=_cparams("parallel", "arbitrary"),
    )(q, k, v)


def _split3(x):
    hi = x.astype(BF16)
    r1 = x - hi.astype(F32)
    mid = r1.astype(BF16)
    lo = (r1 - mid.astype(F32)).astype(BF16)
    return hi, mid, lo


def _dot_tn(a, b):
    return lax.dot_general(a, b, (((0,), (0,)), ((), ())), preferred_element_type=F32)


def _wkv_chunk(r, ld, k, v, kkr, a, states, c):
    heads = range(len(states))
    hs = lambda x, h: x[:, h * C_HD:(h + 1) * C_HD]
    row = lax.broadcasted_iota(I32, (c, c), 0)
    col = lax.broadcasted_iota(I32, (c, c), 1)
    strict = row > col
    incl = row >= col
    incl16 = jnp.where(incl, 1.0, 0.0).astype(BF16)
    eye = jnp.where(row == col, 1.0, 0.0)
    hi, mid, lo = _split3(ld)
    cum = _dot(incl16, hi) + _dot(incl16, mid) + _dot(incl16, lo)
    g_inc = jnp.exp(cum)
    g_exc = jnp.exp(cum - ld)
    g_inv = jnp.exp(-cum)
    g_end = g_inc[c - 1:c, :]
    rt = r * g_inc
    kt = k * g_inv
    k_end = kt * g_end

    kk = []
    for h in heads:
        x = hs(kkr, h)
        kk.append(x / jnp.maximum(jnp.sqrt(jnp.sum(x * x, axis=-1, keepdims=True)), 1e-12))
    at16 = [(-kk[h] * hs(g_exc, h)).astype(BF16) for h in heads]
    bt = [kk[h] * hs(a, h) * hs(g_inv, h) for h in heads]
    ar16 = [jnp.concatenate([at16[h], hs(rt, h).astype(BF16)], axis=0) for h in heads]
    bk16 = [jnp.concatenate([bt[h], hs(kt, h)], axis=0).astype(BF16) for h in heads]
    gram = [_dot_nt(ar16[h], bk16[h]) for h in heads]
    l_ab = [jnp.where(strict, gram[h][:c, :c], 0.0) for h in heads]
    l_ak16 = [jnp.where(strict, gram[h][:c, c:], 0.0).astype(BF16) for h in heads]
    l_rb16 = [jnp.where(incl, gram[h][c:, :c], 0.0).astype(BF16) for h in heads]
    l_rk16 = [jnp.where(incl, gram[h][c:, c:], 0.0).astype(BF16) for h in heads]
    inv = [eye + l_ab[h] for h in heads]
    lp16 = [l_ab[h].astype(BF16) for h in heads]
    n = 2
    while n < c:
        lp16 = [_dot(lp16[h], lp16[h]).astype(BF16) for h in heads]
        inv = [inv[h] + _dot(inv[h].astype(BF16), lp16[h]) for h in heads]
        n *= 2
    inv16 = [inv[h].astype(BF16) for h in heads]
    v16 = [hs(v, h).astype(BF16) for h in heads]
    lakv16 = [_dot(l_ak16[h], v16[h]).astype(BF16) for h in heads]
    w16 = [_dot(inv16[h], at16[h]).astype(BF16) for h in heads]
    u0_16 = [_dot(inv16[h], lakv16[h]).astype(BF16) for h in heads]
    r2_16 = [(hs(rt, h) + _dot(l_rb16[h], w16[h])).astype(BF16) for h in heads]
    y0 = [_dot(l_rb16[h], u0_16[h]) + _dot(l_rk16[h], v16[h]) for h in heads]
    bend16 = [(bt[h] * hs(g_end, h)).astype(BF16) for h in heads]
    kend16 = [hs(k_end, h).astype(BF16) for h in heads]
    m2_16 = [_dot_tn(w16[h], bend16[h]).astype(BF16) for h in heads]
    nmat = [_dot_tn(u0_16[h], bend16[h]) + _dot_tn(v16[h], kend16[h]) for h in heads]
    s16 = [states[h].astype(BF16) for h in heads]
    ys = [_dot_nt(r2_16[h], s16[h]) + y0[h] for h in heads]
    new_states = [states[h] * hs(g_end, h) + _dot(s16[h], m2_16[h]) + nmat[h] for h in heads]
    return ys, new_states


WKV_VECS = ('c_w0', 'c_a0', 'c_k_k', 'c_k_a', 'c_ln_w', 'c_ln_b', 'c_r_k')


def _wkv_kernel(p_ref, prev_ref, mu_ref, vec_ref, w2_ref, a2_ref, g2_ref, s0_ref,
                o_ref, sfin_ref, state_ref, last_ref, *, chunk, valid_len):
    tb = p_ref.shape[0]
    t_blk = pl.program_id(1)
    hs = lambda x, h: x[:, h * C_HD:(h + 1) * C_HD]
    w0, a0, k_k, k_a, ln_w, ln_b, r_k = (vec_ref[i:i + 1, :] for i in range(len(WKV_VECS)))

    @pl.when(t_blk == 0)
    def _():
        state_ref[...] = s0_ref[...]
        last_ref[...] = prev_ref[...]

    def chunk_step(ci, prev_row):
        start = pl.multiple_of(ci * chunk, chunk)
        rows = pl.ds(start, chunk)
        pc = p_ref[rows, :]
        first = lax.broadcasted_iota(I32, pc.shape, 0) == 0
        shifted = jnp.where(first, prev_row, pltpu.roll(pc, 1, 0))
        xs = pc + mu_ref[...] * (shifted - pc)
        r = xs[:, 0:C_WIDTH]
        k = xs[:, C_WIDTH:2 * C_WIDTH]
        v = xs[:, 2 * C_WIDTH:3 * C_WIDTH]
        lora = xs[:, LORA_OFF:C_BLOCK]
        z = w0 + _dot(jnp.tanh(lora).astype(BF16), w2_ref[...])
        ld = -jnp.exp(jnp.minimum(z, 0.0) - jnp.log1p(jnp.exp(-jnp.abs(z))) - 0.5)
        a = 1.0 / (1.0 + jnp.exp(-(a0 + _dot(lora.astype(BF16), a2_ref[...]))))
        g = _dot((1.0 / (1.0 + jnp.exp(-lora))).astype(BF16), g2_ref[...])
        kkr = k * k_k
        k = k * (1.0 + (a - 1.0) * k_a)
        if valid_len is not None:
            live = (t_blk * tb + start + lax.broadcasted_iota(I32, k.shape, 0)) < valid_len
            ld, kkr, k, v = (jnp.where(live, x, 0.0) for x in (ld, kkr, k, v))
        states = [state_ref[h] for h in range(C_HEADS)]
        ys, new_states = _wkv_chunk(r, ld, k, v, kkr, a, states, chunk)
        for h in range(C_HEADS):
            state_ref[h] = new_states[h]
        rkr = r * k * r_k
        outs = []
        for h in range(C_HEADS):
            y = ys[h]
            mean = jnp.mean(y, axis=-1, keepdims=True)
            var = jnp.mean(jnp.square(y - mean), axis=-1, keepdims=True)
            yn = (y - mean) * lax.rsqrt(var + GN_EPS)
            bonus = jnp.sum(hs(rkr, h), axis=-1, keepdims=True) * hs(v, h)
            outs.append(yn * hs(ln_w, h) + hs(ln_b, h) + bonus)
        o_ref[rows, :] = (jnp.concatenate(outs, axis=1) * g).astype(o_ref.dtype)
        return pc[chunk - 1:chunk, :]

    last_ref[...] = lax.fori_loop(0, tb // chunk, chunk_step, last_ref[...])

    @pl.when(t_blk == pl.num_programs(1) - 1)
    def _():
        sfin_ref[...] = state_ref[...]


def wkv7(p, prev, mu, vecs, w2, a2, g2, s0, chunk, tb, valid_len):
    b, s, _ = p.shape
    const = lambda shape: pl.BlockSpec(shape, lambda bi, i: (0,) * len(shape))
    st = pl.BlockSpec((None, C_HEADS, C_HD, C_HD), lambda bi, i: (bi, 0, 0, 0))
    return pl.pallas_call(
        functools.partial(_wkv_kernel, chunk=chunk, valid_len=valid_len if valid_len < s else None),
        grid=(b, s // tb),
        in_specs=[pl.BlockSpec((None, tb, C_BLOCK), lambda bi, i: (bi, i, 0)),
                  pl.BlockSpec((None, 1, C_BLOCK), lambda bi, i: (bi, 0, 0)),
                  const(mu.shape), const(vecs.shape), const(w2.shape), const(a2.shape), const(g2.shape), st],
        out_specs=[pl.BlockSpec((None, tb, C_WIDTH), lambda bi, i: (bi, i, 0)), st],
        out_shape=[jax.ShapeDtypeStruct((b, s, C_WIDTH), BF16),
                   jax.ShapeDtypeStruct((b, C_HEADS, C_HD, C_HD), F32)],
        scratch_shapes=[pltpu.VMEM((C_HEADS, C_HD, C_HD), F32), pltpu.VMEM((1, C_BLOCK), F32)],
        compiler_params=_cparams("parallel", "arbitrary"),
    )(p, prev, mu, vecs, w2, a2, g2, s0)


def _split2(x):
    hi = x.astype(BF16)
    return hi, (x - hi.astype(F32)).astype(BF16)


def _page_spec(rows, width, layer, slot, n_pages):
    def index_map(b, j, pt):
        return (layer, pt[b, jnp.minimum(j * SAMPLE_PAGES_PER_STEP + slot, n_pages - 1)], 0, 0)
    return pl.BlockSpec((None, None, rows, width), index_map)


def _idx_sample_kernel(pt_ref, qi_ref, w_ref, knew_ref, *rest, t):
    pages, o_ref = rest[:-1], rest[-1]
    j = pl.program_id(1)
    last = pl.num_programs(1) - 1
    q_hi, q_lo = _split2(qi_ref[...])
    w = w_ref[...]

    def score(kpage):
        k_hi, k_lo = _split2(kpage)
        s = _dot(q_hi, k_hi) + _dot(q_hi, k_lo) + _dot(q_lo, k_hi)
        s = jnp.maximum(s, 0.0) * w
        return jnp.sum(s.reshape(t, IDX_HEADS, PAGE_SIZE), axis=1)

    @pl.when(j < last)
    def _():
        for i, page in enumerate(pages):
            o_ref[:, i * PAGE_SIZE:(i + 1) * PAGE_SIZE] = score(page[...])

    @pl.when(j == last)
    def _():
        o_ref[...] = jnp.zeros(o_ref.shape, F32)
        o_ref[:, 0:PAGE_SIZE] = score(knew_ref[...])


def idx_sample(page_table, qi, w, knew, cache_idx, layer):
    db, rows, _ = qi.shape
    t = rows // IDX_HEADS
    n_pages = page_table.shape[1]
    pg = SAMPLE_PAGES_PER_STEP
    steps = n_pages // pg + 1
    per_b = lambda b, j, pt: (b, 0, 0)
    return pl.pallas_call(
        functools.partial(_idx_sample_kernel, t=t),
        grid_spec=pltpu.PrefetchScalarGridSpec(
            num_scalar_prefetch=1,
            grid=(db, steps),
            in_specs=[pl.BlockSpec((None, rows, IDX_HD), per_b),
                      pl.BlockSpec((None, rows, 1), per_b),
                      pl.BlockSpec((None, IDX_HD, PAGE_SIZE), per_b)]
                     + [_page_spec(IDX_HD, PAGE_SIZE, layer, i, n_pages) for i in range(pg)],
            out_specs=pl.BlockSpec((None, t, pg * PAGE_SIZE), lambda b, j, pt: (b, 0, j))),
        out_shape=jax.ShapeDtypeStruct((db, t, steps * pg * PAGE_SIZE), F32),
        compiler_params=_cparams("parallel", "arbitrary"),
    )(page_table, qi, w, knew, *([cache_idx] * pg))


def _select_sample_kernel(isc_ref, o_ref, *, topk, t, past_len):
    rows, n = isc_ref.shape
    pos = lax.broadcasted_iota(I32, (rows, n), 1)
    r = pl.program_id(0) * rows + lax.broadcasted_iota(I32, (rows, n), 0)
    causal = pos <= past_len + (r & (t - 1))
    key = _sortable_key(jnp.where(causal, isc_ref[...], NEG_INF))
    mask = _topk_mask(key, pos, topk, n.bit_length()) & causal
    o_ref[...] = jnp.where(mask, 1.0, 0.0)


def select_sample(isc, topk, t, past_len):
    rows, n = isc.shape
    tr = 16
    assert t & (t - 1) == 0 and rows % tr == 0
    return pl.pallas_call(
        functools.partial(_select_sample_kernel, topk=topk, t=t, past_len=past_len),
        grid=(rows // tr,),
        in_specs=[pl.BlockSpec((tr, n), lambda i: (i, 0))],
        out_specs=pl.BlockSpec((tr, n), lambda i: (i, 0)),
        out_shape=jax.ShapeDtypeStruct((rows, n), F32),
        compiler_params=_cparams("parallel"),
    )(isc)


def _own_head(shape, heads):
    return ((lax.broadcasted_iota(I32, shape, 1) & (heads - 1))
            == (lax.broadcasted_iota(I32, shape, 0) & (heads - 1)))


def _dsa_sample_kernel(pt_ref, q_ref, mask_ref, knew_ref, vnew_ref, *rest, heads):
    pg = SAMPLE_PAGES_PER_STEP
    kpages, vpages = rest[:pg], rest[pg:2 * pg]
    o_ref, m_ref, l_ref, acc_ref = rest[2 * pg:]
    j = pl.program_id(1)
    last = pl.num_programs(1) - 1
    q = q_ref[...]
    page_rows = PAGE_SIZE * heads
    shift = heads.bit_length() - 1
    spread = jnp.where(
        jnp.right_shift(lax.broadcasted_iota(I32, (PAGE_SIZE, page_rows), 1), shift)
        == lax.broadcasted_iota(I32, (PAGE_SIZE, page_rows), 0), 1.0, 0.0).astype(BF16)

    @pl.when(j == 0)
    def _():
        m_ref[...] = jnp.full(m_ref.shape, NEG_INF, F32)
        l_ref[...] = jnp.zeros(l_ref.shape, F32)
        acc_ref[...] = jnp.zeros(acc_ref.shape, F32)

    def update(ks, vs, mask):
        s = jnp.concatenate([_dot_nt(q, kp.astype(BF16)) for kp in ks], axis=1) * (A_HD ** -0.5)
        sel = jnp.concatenate([_dot(mask[:, i * PAGE_SIZE:(i + 1) * PAGE_SIZE].astype(BF16), spread)
                               for i in range(len(ks))], axis=1)
        s = jnp.where((sel > 0.5) & _own_head(s.shape, heads), s, NEG_INF)
        m_old = m_ref[...]
        m_new = jnp.maximum(m_old, jnp.max(s, axis=-1, keepdims=True))
        m_safe = jnp.where(m_new == NEG_INF, 0.0, m_new)
        p = jnp.exp(s - m_safe)
        alpha = jnp.exp(m_old - m_safe)
        l_ref[...] = alpha * l_ref[...] + jnp.sum(p, axis=-1, keepdims=True)
        acc = alpha * acc_ref[...]
        for i, vp in enumerate(vs):
            acc += _dot(p[:, i * page_rows:(i + 1) * page_rows].astype(BF16), vp.astype(BF16))
        acc_ref[...] = acc
        m_ref[...] = m_new

    @pl.when(j < last)
    def _():
        update([kp[...] for kp in kpages], [vp[...] for vp in vpages], mask_ref[...])

    @pl.when(j == last)
    def _():
        update([knew_ref[...]], [vnew_ref[...]], mask_ref[:, 0:PAGE_SIZE])
        o_ref[...] = acc_ref[...] / l_ref[...]


def dsa_sample(page_table, q, mask, knew, vnew, cache_k, cache_v, layer, heads):
    db, rows, hd = q.shape
    n_pages = page_table.shape[1]
    pg = SAMPLE_PAGES_PER_STEP
    steps = n_pages // pg + 1
    page_rows = PAGE_SIZE * heads
    per_b = lambda b, j, pt: (b, 0, 0)
    return pl.pallas_call(
        functools.partial(_dsa_sample_kernel, heads=heads),
        grid_spec=pltpu.PrefetchScalarGridSpec(
            num_scalar_prefetch=1,
            grid=(db, steps),
            in_specs=[pl.BlockSpec((None, rows, hd), per_b),
                      pl.BlockSpec((None, rows, pg * PAGE_SIZE), lambda b, j, pt: (b, 0, j)),
                      pl.BlockSpec((None, page_rows, hd), per_b),
                      pl.BlockSpec((None, page_rows, hd), per_b)]
                     + [_page_spec(page_rows, hd, layer, i, n_pages) for i in range(pg)]
                     + [_page_spec(page_rows, hd, layer, i, n_pages) for i in range(pg)],
            out_specs=pl.BlockSpec((None, rows, hd), per_b),
            scratch_shapes=[pltpu.VMEM((rows, 1), F32), pltpu.VMEM((rows, 1), F32),
                            pltpu.VMEM((rows, hd), F32)]),
        out_shape=jax.ShapeDtypeStruct((db, rows, hd), F32),
        compiler_params=_cparams("parallel", "arbitrary"),
    )(page_table, q, mask, knew, vnew, *([cache_k] * pg), *([cache_v] * pg))


def _moba_sample_kernel(pt_ref, q_ref, knew_ref, vnew_ref, *rest, heads, ksel):
    pg = SAMPLE_PAGES_PER_STEP
    ppb = MOBA_BLOCK // PAGE_SIZE
    kpages, vpages = rest[:pg], rest[pg:2 * pg]
    o_ref, g_ref, m_ref, l_ref, ob_ref = rest[2 * pg:]
    j = pl.program_id(1)
    last = pl.num_programs(1) - 1
    q = q_ref[...]
    q16 = q.astype(BF16)
    rows, hd = q.shape
    scale = B_HD ** -0.5
    sublanes = 8
    assert 2 * heads == sublanes and hd == LANES and rows % sublanes == 0
    wide = lambda x: jnp.broadcast_to(x, (rows, LANES))

    def partial_softmax(s, v16):
        m = jnp.max(s, axis=-1, keepdims=True)
        p = jnp.exp(s - m)
        return m, jnp.sum(p, axis=-1, keepdims=True), _dot(p.astype(BF16), v16)

    @pl.when(j < last)
    def _():
        for blk in range(pg // ppb):
            kblk = jnp.concatenate([kpages[blk * ppb + i][...] for i in range(ppb)], axis=0)
            v16 = jnp.concatenate([vpages[blk * ppb + i][...] for i in range(ppb)], axis=0).astype(BF16)
            s = _dot_nt(q16, kblk.astype(BF16)) * scale
            m, l, o = partial_softmax(jnp.where(_own_head(s.shape, heads), s, NEG_INF), v16)
            n = j * (pg // ppb) + blk
            fold = jnp.sum(kblk.reshape(kblk.shape[0] // sublanes, sublanes, hd), axis=0)
            kmean = (fold + pltpu.roll(fold, heads, 0)) * (1.0 / MOBA_BLOCK)
            kmean = jnp.concatenate([kmean] * (rows // sublanes), axis=0)
            g_ref[n] = wide(jnp.sum(q * kmean, axis=-1, keepdims=True))
            m_ref[n] = wide(m)
            l_ref[n] = wide(l)
            ob_ref[n] = o

    @pl.when(j == last)
    def _():
        shift = heads.bit_length() - 1
        s = _dot_nt(q16, knew_ref[...].astype(BF16)) * scale
        causal = (jnp.right_shift(lax.broadcasted_iota(I32, s.shape, 1), shift)
                  <= jnp.right_shift(lax.broadcasted_iota(I32, s.shape, 0), shift))
        s = jnp.where(causal & _own_head(s.shape, heads), s, NEG_INF)
        m_own, l_own, o_own = partial_softmax(s, vnew_ref[...].astype(BF16))
        m_own, l_own = wide(m_own), wide(l_own)
        gates = g_ref[...]
        nidx = lax.broadcasted_iota(I32, gates.shape, 0)
        chosen = jnp.zeros(gates.shape, F32)
        for _ in range(ksel):
            best = jnp.max(gates, axis=0, keepdims=True)
            first = jnp.min(jnp.where(gates == best, nidx, gates.shape[0]), axis=0, keepdims=True)
            pick = nidx == first
            chosen = jnp.where(pick, 1.0, chosen)
            gates = jnp.where(pick, NEG_INF, gates)
        m_blk = m_ref[...]
        m_tot = jnp.maximum(m_own, jnp.max(jnp.where(chosen > 0.5, m_blk, NEG_INF), axis=0))
        wgt = jnp.where(chosen > 0.5, jnp.exp(m_blk - m_tot[None]), 0.0)
        w_own = jnp.exp(m_own - m_tot)
        l_tot = w_own * l_own + jnp.sum(wgt * l_ref[...], axis=0)
        acc = w_own * o_own
        for n in range(gates.shape[0]):
            acc += wgt[n] * ob_ref[n]
        o_ref[...] = acc / l_tot


def moba_sample(page_table, q, knew, vnew, cache_k, cache_v, layer, heads):
    db, rows, hd = q.shape
    n_pages = page_table.shape[1]
    pg = SAMPLE_PAGES_PER_STEP
    steps = n_pages // pg + 1
    nbp = n_pages * PAGE_SIZE // MOBA_BLOCK
    page_rows = PAGE_SIZE * heads
    per_b = lambda b, j, pt: (b, 0, 0)
    return pl.pallas_call(
        functools.partial(_moba_sample_kernel, heads=heads, ksel=max(1, min(MOBA_TOPK, nbp))),
        grid_spec=pltpu.PrefetchScalarGridSpec(
            num_scalar_prefetch=1,
            grid=(db, steps),
            in_specs=[pl.BlockSpec((None, rows, hd), per_b),
                      pl.BlockSpec((None, page_rows, hd), per_b),
                      pl.BlockSpec((None, page_rows, hd), per_b)]
                     + [_page_spec(page_rows, hd, layer, i, n_pages) for i in range(pg)]
                     + [_page_spec(page_rows, hd, layer, i, n_pages) for i in range(pg)],
            out_specs=pl.BlockSpec((None, rows, hd), per_b),
            scratch_shapes=[pltpu.VMEM((nbp, rows, LANES), F32), pltpu.VMEM((nbp, rows, LANES), F32),
                            pltpu.VMEM((nbp, rows, LANES), F32), pltpu.VMEM((nbp, rows, hd), F32)]),
        out_shape=jax.ShapeDtypeStruct((db, rows, hd), F32),
        compiler_params=_cparams("parallel", "arbitrary"),
    )(page_table, q, knew, vnew, *([cache_k] * pg), *([cache_v] * pg))


def _new_page(x, rows):
    return jnp.pad(x, ((0, 0), (0, rows - x.shape[1]), (0, 0)))


def _sample_attention(qa, ka, va, qi, kiwi, qb, kb, vb, caches, page_table, layer, db, t):
    cache_a_k, cache_a_v, cache_a_idx, cache_b_k, cache_b_v = caches
    past_len = page_table.shape[1] * PAGE_SIZE
    topk = min(A_TOPK_MAX, (past_len + t) // 4)
    ki = kiwi[:, :IDX_HD].reshape(db, t, IDX_HD)
    wi = kiwi[:, IDX_HD:IDX_HD + IDX_HEADS]
    knew = _new_page(ki, PAGE_SIZE).transpose(0, 2, 1)
    isc = idx_sample(page_table, qi.reshape(db, t * IDX_HEADS, IDX_HD), wi.reshape(db, t * IDX_HEADS, 1),
                     knew, cache_a_idx, layer)
    n = isc.shape[-1]
    mask = select_sample(isc.reshape(db * t, n), topk, t, past_len).reshape(db, t, n)
    rows_a = lambda z: z.reshape(db, t * A_HEADS, A_HD)
    rows_b = lambda z: z.reshape(db, t * B_HEADS, B_HD)
    oa = dsa_sample(page_table, rows_a(qa).astype(BF16), jnp.repeat(mask, A_HEADS, axis=1),
                    _new_page(rows_a(ka), PAGE_SIZE * A_HEADS), _new_page(rows_a(va), PAGE_SIZE * A_HEADS),
                    cache_a_k, cache_a_v, layer, A_HEADS)
    ob = moba_sample(page_table, rows_b(qb), _new_page(rows_b(kb), PAGE_SIZE * B_HEADS),
                     _new_page(rows_b(vb), PAGE_SIZE * B_HEADS), cache_b_k, cache_b_v, layer, B_HEADS)
    return oa.reshape(db * t, A_WIDTH), ob.reshape(db * t, B_WIDTH)


def _pad_cols(x, width):
    return jnp.pad(x, [(0, 0)] * (x.ndim - 1) + [(0, width - x.shape[-1])])


def _layer_params(l, norm1_g, w_in, a_q_norm, a_k_norm, idx_k_norm, b_q_norm, b_k_norm, c_mu, c_w0, c_w2, c_a0,
                  c_a2, c_g2, c_k_k, c_k_a, c_r_k, c_ln_w, c_ln_b, w_o, norm2_g, w_ff1, w_ff2):
    w = w_in[l]
    a_cols, b_cols, c_cols = w[:, :A_COLS], w[:, A_COLS:A_COLS + B_COLS], w[:, A_COLS + B_COLS:]
    w_pack = jnp.concatenate([_pad_cols(c_cols, A_OFF), _pad_cols(a_cols, B_OFF - A_OFF), b_cols],
                             axis=1).astype(BF16)
    lora_rows = lambda x, off: jnp.pad(x, ((off, LORA_BLOCK - off - x.shape[0]), (0, 0))).astype(BF16)
    vec = dict(c_w0=c_w0[l], c_a0=c_a0[l], c_k_k=c_k_k[l], c_k_a=c_k_a[l], c_ln_w=c_ln_w[l], c_ln_b=c_ln_b[l],
               c_r_k=c_r_k[l].reshape(-1))
    vecs = jnp.stack([vec[name] for name in WKV_VECS] + [jnp.zeros((C_WIDTH,), F32)])
    gains = jnp.stack([a_q_norm[l], a_k_norm[l], b_q_norm[l], b_k_norm[l], _pad_cols(idx_k_norm[l], LANES)]
                      + [jnp.zeros((LANES,), F32)] * 3)
    return dict(norm1_g=norm1_g[l], w_in=w_pack, gains=gains, mu=_pad_cols(c_mu[l], C_BLOCK)[None], vecs=vecs,
                w2=lora_rows(c_w2[l], 0), a2=lora_rows(c_a2[l], W_LORA), g2=lora_rows(c_g2[l], W_LORA + A_LORA),
                w_o=w_o[l].astype(BF16), norm2_g=norm2_g[l], w_ff1=w_ff1[l].astype(BF16),
                w_ff2=w_ff2[l].astype(BF16))


def _layer(x, tabs, prev_shift, wkv0, lw, sample_ctx):
    b, s, d = x.shape
    m = b * s
    p = norm_matmul(x.reshape(m, d), lw['norm1_g'], lw['w_in'], min(m, 1024), PROJ_TILE_N)
    qa, ka, va, qi, kiwi, qb, kb, vb = prep(p, tabs[0], tabs[1], lw['gains'], min(m, 256))
    p3 = p.reshape(b, s, P_PACK)
    if sample_ctx is None:
        seq = lambda z: z.reshape(b, s, z.shape[-1])
        oa = dsa_prompt(seq(qa), seq(ka), seq(va), seq(qi), seq(kiwi)).reshape(m, A_WIDTH)
        ob = moba_prompt(seq(qb), seq(kb), seq(vb)).reshape(m, B_WIDTH)
        p_scan = p3
    else:
        caches, page_table, layer = sample_ctx
        oa, ob = _sample_attention(qa, ka, va, qi, kiwi, qb, kb, vb, caches, page_table, layer, b, s)
        oa, ob = oa.astype(BF16), ob.astype(BF16)
        p_scan = jnp.pad(p3, ((0, 0), (0, -s % 16), (0, 0)))
    sp = p_scan.shape[1]
    chunk = min(WKV_CHUNK, sp)
    oc, wkv_fin = wkv7(p_scan, _pad_cols(prev_shift, C_BLOCK)[:, None, :], lw['mu'], lw['vecs'], lw['w2'],
                       lw['a2'], lw['g2'], wkv0, chunk, min(sp, 4 * chunk), s)
    tm2 = min(m, 512)
    x2 = out_proj(x.reshape(m, d), oa, ob, oc[:, :s].reshape(m, C_WIDTH), lw['w_o'], tm2)
    x3 = mlp(x2, lw['norm2_g'], lw['w_ff1'], lw['w_ff2'], tm2, 512).reshape(b, s, d)
    rows = (ka.reshape(b, s, A_HEADS, A_HD), va.reshape(b, s, A_HEADS, A_HD),
            kiwi[:, :IDX_HD].reshape(b, s, IDX_HD), kb.reshape(b, s, B_HEADS, B_HD),
            vb.reshape(b, s, B_HEADS, B_HD), wkv_fin, p3[:, s - 1, :C_COLS])
    return x3, rows


def kernel(x_prompt, x_sample, cache_a_k, cache_a_v, cache_a_idx, cache_b_k, cache_b_v, state_wkv, state_shift, page_table, norm1_g, w_in, a_q_norm, a_k_norm, idx_k_norm, b_q_norm, b_k_norm, c_mu, c_w0, c_w2, c_a0, c_a2, c_g2, c_k_k, c_k_a, c_r_k, c_ln_w, c_ln_b, w_o, norm2_g, w_ff1, w_ff2):
    depth = w_in.shape[0]
    bp, sp = x_prompt.shape[:2]
    db, t = x_sample.shape[:2]
    pos_p = jnp.arange(sp)
    pos_s = jnp.tile(page_table.shape[1] * PAGE_SIZE + jnp.arange(t), db)
    tabs_p = (_rope_tables(pos_p, A_HD, 1), _rope_tables(pos_p, IDX_HD, LANES // IDX_HD))
    tabs_s = (_rope_tables(pos_s, A_HD, 1), _rope_tables(pos_s, IDX_HD, LANES // IDX_HD))
    shift0 = jnp.zeros((bp, C_COLS), x_prompt.dtype)
    wkv0 = jnp.zeros((bp, C_HEADS, C_HD, C_HD), F32)
    xp, xs = x_prompt, x_sample
    by_row = lambda c: c.reshape(c.shape[0], c.shape[1], -1, c.shape[-1])
    caches = (by_row(cache_a_k), by_row(cache_a_v), cache_a_idx.transpose(0, 1, 3, 2),
              by_row(cache_b_k), by_row(cache_b_v))
    p_rows, s_rows = [], []
    for l in range(depth):
        lw = _layer_params(l, norm1_g, w_in, a_q_norm, a_k_norm, idx_k_norm, b_q_norm, b_k_norm, c_mu, c_w0, c_w2,
                           c_a0, c_a2, c_g2, c_k_k, c_k_a, c_r_k, c_ln_w, c_ln_b, w_o, norm2_g, w_ff1, w_ff2)
        xp, rows = _layer(xp, tabs_p, shift0, wkv0, lw, None)
        p_rows.append(rows)
        xs, rows = _layer(xs, tabs_s, state_shift[l], state_wkv[l], lw, (caches, page_table, l))
        s_rows.append(rows)
    p_out = [jnp.stack(z) for z in zip(*p_rows)]
    s_out = [jnp.stack(z) for z in zip(*s_rows)]
    return (xp, xs, *p_out, *s_out)
```

```python
import functools

import jax
import jax.numpy as jnp
from jax import lax
from jax.experimental import pallas as pl
from jax.experimental.pallas import tpu as pltpu

F32 = jnp.float32
BF16 = jnp.bfloat16
I32 = jnp.int32

VMEM_LIMIT_BYTES = 56 * 1024 * 1024
LANES = 128

PAGE_SIZE = 128
A_HEADS = 4
A_HD = 128
A_WIDTH = A_HEADS * A_HD
IDX_HEADS = 16
IDX_HD = 64
A_TOPK_MAX = 256
A_Q_BLOCK = 128
B_HEADS = 4
B_HD = 128
B_WIDTH = B_HEADS * B_HD
MOBA_BLOCK = 256
MOBA_TOPK = 3
C_HD = 64
C_HEADS = 16
C_WIDTH = C_HEADS * C_HD
W_LORA = 64
A_LORA = 64
G_LORA = 160
A_COLS = 3 * A_WIDTH + IDX_HEADS * IDX_HD + IDX_HD + IDX_HEADS
B_COLS = 3 * B_WIDTH
C_COLS = 3 * C_WIDTH + W_LORA + A_LORA + G_LORA
ROPE_THETA = 500000.0
NORM_EPS = 1e-6
GN_EPS = 64e-5
IDX_W_SCALE = (IDX_HEADS * IDX_HD) ** -0.5
WKV_CHUNK = 64
SAMPLE_PAGES_PER_STEP = 16
DSA_KEY_BUCKET = 512
PROJ_TILE_N = 1536
MLP_TILE_F = 1024

C_BLOCK = C_COLS + (-C_COLS % LANES)
A_OFF = C_BLOCK
B_OFF = A_OFF + A_COLS + (-A_COLS % LANES)
P_PACK = B_OFF + B_COLS
assert P_PACK % PROJ_TILE_N == 0
LORA_OFF = 3 * C_WIDTH
LORA_BLOCK = C_BLOCK - LORA_OFF
A_QI_OFF = A_OFF + 3 * A_WIDTH
A_KI_OFF = A_QI_OFF + IDX_HEADS * IDX_HD
assert A_KI_OFF + LANES == B_OFF and IDX_HD + IDX_HEADS <= LANES

NEG_INF = float("-inf")
INT_MIN = -2 ** 31


def _cparams(*sem):
    return pltpu.CompilerParams(dimension_semantics=sem, vmem_limit_bytes=VMEM_LIMIT_BYTES)


def _dot_nt(a, b):
    return lax.dot_general(a, b, (((1,), (1,)), ((), ())), preferred_element_type=F32)


def _dot(a, b):
    return jnp.dot(a, b, preferred_element_type=F32)


def _norm_matmul_kernel(x_ref, g_ref, w_ref, o_ref, xn_ref):
    @pl.when(pl.program_id(1) == 0)
    def _():
        x = x_ref[...]
        y = x * lax.rsqrt(jnp.mean(x * x, axis=-1, keepdims=True) + NORM_EPS)
        xn_ref[...] = (y * g_ref[...]).astype(BF16)

    o_ref[...] = _dot(xn_ref[...], w_ref[...])


def norm_matmul(x, g, w, tm, tn):
    m, d = x.shape
    p = w.shape[1]
    return pl.pallas_call(
        _norm_matmul_kernel,
        grid=(m // tm, p // tn),
        in_specs=[pl.BlockSpec((tm, d), lambda i, j: (i, 0)),
                  pl.BlockSpec((1, d), lambda i, j: (0, 0)),
                  pl.BlockSpec((d, tn), lambda i, j: (0, j))],
        out_specs=pl.BlockSpec((tm, tn), lambda i, j: (i, j)),
        out_shape=jax.ShapeDtypeStruct((m, p), F32),
        scratch_shapes=[pltpu.VMEM((tm, d), BF16)],
        compiler_params=_cparams("parallel", "arbitrary"),
    )(x, g.reshape(1, d), w)


def _out_proj_kernel(x_ref, oa_ref, ob_ref, oc_ref, w_ref, o_ref):
    wa = oa_ref.shape[1]
    wb = ob_ref.shape[1]
    acc = _dot(oa_ref[...], w_ref[0:wa, :])
    acc += _dot(ob_ref[...], w_ref[wa:wa + wb, :])
    acc += _dot(oc_ref[...], w_ref[wa + wb:, :])
    o_ref[...] = x_ref[...] + acc


def out_proj(x, oa, ob, oc, w, tm):
    m, d = x.shape
    row = lambda i: (i, 0)
    return pl.pallas_call(
        _out_proj_kernel,
        grid=(m // tm,),
        in_specs=[pl.BlockSpec((tm, d), row),
                  pl.BlockSpec((tm, oa.shape[1]), row),
                  pl.BlockSpec((tm, ob.shape[1]), row),
                  pl.BlockSpec((tm, oc.shape[1]), row),
                  pl.BlockSpec(w.shape, lambda i: (0, 0))],
        out_specs=pl.BlockSpec((tm, d), row),
        out_shape=jax.ShapeDtypeStruct((m, d), F32),
        compiler_params=_cparams("parallel"),
    )(x, oa, ob, oc, w)


def _mlp_kernel(x_ref, g_ref, w1_ref, w2_ref, o_ref, xn_ref):
    @pl.when(pl.program_id(1) == 0)
    def _():
        x = x_ref[...]
        y = x * lax.rsqrt(jnp.mean(x * x, axis=-1, keepdims=True) + NORM_EPS)
        xn_ref[...] = (y * g_ref[...]).astype(BF16)
        o_ref[...] = x

    h = jnp.maximum(_dot(xn_ref[...], w1_ref[...]), 0.0)
    o_ref[...] += _dot((h * h).astype(BF16), w2_ref[...])


def mlp(x, g, w1, w2, tm, tf):
    m, d = x.shape
    f = w1.shape[1]
    return pl.pallas_call(
        _mlp_kernel,
        grid=(m // tm, f // tf),
        in_specs=[pl.BlockSpec((tm, d), lambda i, j: (i, 0)),
                  pl.BlockSpec((1, d), lambda i, j: (0, 0)),
                  pl.BlockSpec((d, tf), lambda i, j: (0, j)),
                  pl.BlockSpec((tf, d), lambda i, j: (j, 0))],
        out_specs=pl.BlockSpec((tm, d), lambda i, j: (i, 0)),
        out_shape=jax.ShapeDtypeStruct((m, d), F32),
        scratch_shapes=[pltpu.VMEM((tm, d), BF16)],
        compiler_params=_cparams("parallel", "arbitrary"),
    )(x, g.reshape(1, d), w1, w2)


def _rope_tables(pos, hd, reps):
    rot = hd // 4
    half = rot // 2
    inv = ROPE_THETA ** (-2.0 * jnp.arange(half, dtype=F32) / rot)
    ang = pos.astype(F32)[:, None] * inv[None, :]
    cos, sin = jnp.cos(ang), jnp.sin(ang)
    zeros = jnp.zeros_like(sin)
    rest = jnp.zeros((pos.shape[0], hd - rot), F32)
    tabs = [jnp.concatenate([cos, cos, rest + 1.0], axis=1),
            jnp.concatenate([zeros, sin, rest], axis=1),
            jnp.concatenate([-sin, zeros, rest], axis=1)]
    return jnp.stack([jnp.tile(z, (1, reps)) for z in tabs])


def _rope(y, tab_ref, half):
    width = y.shape[1]
    return (y * tab_ref[0] + pltpu.roll(y, half, 1) * tab_ref[1]
            + pltpu.roll(y, width - half, 1) * tab_ref[2])


def _prep_kernel(p_ref, ta_ref, ti_ref, g_ref, qa_ref, ka_ref, va_ref, qi_ref, kiwi_ref, qb_ref, kb_ref, vb_ref):
    def normed(x, g):
        return x * lax.rsqrt(jnp.mean(x * x, axis=-1, keepdims=True) + NORM_EPS) * g

    for base, q_ref, k_ref, v_ref, heads, hd, gq, gk in (
            (A_OFF, qa_ref, ka_ref, va_ref, A_HEADS, A_HD, 0, 1),
            (B_OFF, qb_ref, kb_ref, vb_ref, B_HEADS, B_HD, 2, 3)):
        width = heads * hd
        for h in range(heads):
            sl = slice(h * hd, (h + 1) * hd)
            q_ref[:, sl] = _rope(normed(p_ref[:, base + h * hd:base + (h + 1) * hd], g_ref[gq:gq + 1, :]),
                                 ta_ref, hd // 8)
            k_ref[:, sl] = _rope(normed(p_ref[:, base + width + h * hd:base + width + (h + 1) * hd],
                                        g_ref[gk:gk + 1, :]), ta_ref, hd // 8)
        v_ref[...] = p_ref[:, base + 2 * width:base + 3 * width]
    for j in range(IDX_HEADS * IDX_HD // LANES):
        sl = slice(j * LANES, (j + 1) * LANES)
        qi_ref[:, sl] = _rope(p_ref[:, A_QI_OFF + j * LANES:A_QI_OFF + (j + 1) * LANES], ti_ref, IDX_HD // 8)
    x = p_ref[:, A_KI_OFF:A_KI_OFF + LANES]
    is_key = lax.broadcasted_iota(I32, x.shape, 1) < IDX_HD
    ms = jnp.sum(jnp.where(is_key, x * x, 0.0), axis=-1, keepdims=True) * (1.0 / IDX_HD)
    ki = _rope(x * lax.rsqrt(ms + NORM_EPS) * g_ref[4:5, :], ti_ref, IDX_HD // 8)
    kiwi_ref[...] = jnp.where(is_key, ki, x * IDX_W_SCALE)


def prep(p, tab_a, tab_i, gains, tm):
    m = p.shape[0]
    period = tab_a.shape[1] // tm
    row = lambda i: (i, 0)
    tab = pl.BlockSpec((3, tm, LANES), lambda i: (0, i % period, 0))
    widths = (A_WIDTH, A_WIDTH, A_WIDTH, IDX_HEADS * IDX_HD, LANES, B_WIDTH, B_WIDTH, B_WIDTH)
    return pl.pallas_call(
        _prep_kernel,
        grid=(m // tm,),
        in_specs=[pl.BlockSpec((tm, P_PACK), row), tab, tab, pl.BlockSpec(gains.shape, lambda i: (0, 0))],
        out_specs=[pl.BlockSpec((tm, w), row) for w in widths],
        out_shape=[jax.ShapeDtypeStruct((m, w), F32) for w in widths],
        compiler_params=_cparams("parallel"),
    )(p, tab_a, tab_i, gains)


def _sortable_key(x):
    bits = pltpu.bitcast(x, I32)
    return jnp.where(bits >= 0, bits, bits ^ jnp.int32(0x7FFFFFFF))


def _topk_mask(key, pos, topk, n_pos_bits):
    rows = key.shape[0]

    def count(m):
        return jnp.sum(jnp.where(m, 1.0, 0.0), axis=-1, keepdims=True).astype(I32)

    groups = 2 if rows % 16 == 0 else 1
    step = rows // groups
    parts = [key[g * step:(g + 1) * step] for g in range(groups)]

    def value_step(it, tus):
        bit = jnp.left_shift(jnp.int32(1), 31 - it)
        out = []
        for part, tu in zip(parts, tus):
            cand_u = tu | bit
            cnt = count(part >= (cand_u ^ jnp.int32(INT_MIN)))
            out.append(jnp.where(cnt >= topk, cand_u, tu))
        return tuple(out)

    tus = lax.fori_loop(0, 32, value_step, tuple(jnp.zeros((step, 1), I32) for _ in range(groups)))
    thr = jnp.concatenate(tus, axis=0) ^ jnp.int32(INT_MIN)
    gt = key > thr
    eq = key == thr
    need = topk - count(gt)
    excess = count(eq) - need

    def tie_search(_):
        def pos_step(it, x):
            cand = x | jnp.left_shift(jnp.int32(1), n_pos_bits - 1 - it)
            cnt = count(eq & (pos < cand))
            return jnp.where(cnt < need, cand, x)
        return lax.fori_loop(0, n_pos_bits, pos_step, jnp.zeros((rows, 1), I32))

    last = lax.cond(jnp.max(excess) > 0, tie_search,
                    lambda _: jnp.full((rows, 1), 2 ** n_pos_bits, I32), 0)
    return gt | (eq & (pos <= last))


def _masked_attention(q, k, v, mask, scale):
    s = _dot_nt(q, k) * scale
    s = jnp.where(mask, s, NEG_INF)
    m = jnp.max(s, axis=-1, keepdims=True)
    p = jnp.exp(s - m)
    l = jnp.sum(p, axis=-1, keepdims=True)
    return _dot(p.astype(BF16), v) / l


def _dsa_prompt_kernel(q_ref, k_ref, v_ref, qi_ref, kiwi_ref, kiwi_blk_ref, o_ref, k16_ref, v16_ref, ki16_ref,
                       *, topk, bucket):
    tq = q_ref.shape[0]
    s_len = k_ref.shape[0]
    i = pl.program_id(1)

    @pl.when(i == 0)
    def _():
        k16_ref[...] = k_ref[...].astype(BF16)
        v16_ref[...] = v_ref[...].astype(BF16)
        ki16_ref[...] = kiwi_ref[:, 0:IDX_HD].astype(BF16)

    wi = kiwi_blk_ref[:, IDX_HD:IDX_HD + IDX_HEADS]
    qi16 = qi_ref[...].astype(BF16)
    q16 = q_ref[...].astype(BF16)

    def attend(n_keys):
        qpos = i * tq + lax.broadcasted_iota(I32, (tq, n_keys), 0)
        kpos = lax.broadcasted_iota(I32, (tq, n_keys), 1)
        causal = kpos <= qpos
        ki = ki16_ref[0:n_keys, :]
        isc = jnp.zeros((tq, n_keys), F32)
        for h in range(IDX_HEADS):
            s = _dot_nt(qi16[:, h * IDX_HD:(h + 1) * IDX_HD], ki)
            isc += jnp.maximum(s, 0.0) * wi[:, h:h + 1]
        key = _sortable_key(jnp.where(causal, isc, NEG_INF))
        mask = _topk_mask(key, kpos, topk, (n_keys - 1).bit_length()) & causal
        for h in range(A_HEADS):
            sl = slice(h * A_HD, (h + 1) * A_HD)
            o = _masked_attention(q16[:, sl], k16_ref[0:n_keys, sl], v16_ref[0:n_keys, sl], mask, A_HD ** -0.5)
            o_ref[:, sl] = o.astype(o_ref.dtype)

    for bkt in range(s_len // bucket):
        pl.when(i // (bucket // tq) == bkt)(functools.partial(attend, (bkt + 1) * bucket))


def dsa_prompt(q, k, v, qi, kiwi):
    b, s, _ = q.shape
    tq = A_Q_BLOCK
    topk = min(A_TOPK_MAX, s // 4)
    bucket = min(s, DSA_KEY_BUCKET)
    assert s % bucket == 0 and bucket % tq == 0 and bucket >= topk
    blk = lambda bi, i: (bi, i, 0)
    full = lambda bi, i: (bi, 0, 0)
    return pl.pallas_call(
        functools.partial(_dsa_prompt_kernel, topk=topk, bucket=bucket),
        grid=(b, s // tq),
        in_specs=[pl.BlockSpec((None, tq, A_WIDTH), blk),
                  pl.BlockSpec((None, s, A_WIDTH), full),
                  pl.BlockSpec((None, s, A_WIDTH), full),
                  pl.BlockSpec((None, tq, IDX_HEADS * IDX_HD), blk),
                  pl.BlockSpec((None, s, LANES), full),
                  pl.BlockSpec((None, tq, LANES), blk)],
        out_specs=pl.BlockSpec((None, tq, A_WIDTH), blk),
        out_shape=jax.ShapeDtypeStruct((b, s, A_WIDTH), BF16),
        scratch_shapes=[pltpu.VMEM((s, A_WIDTH), BF16), pltpu.VMEM((s, A_WIDTH), BF16),
                        pltpu.VMEM((s, IDX_HD), BF16)],
        compiler_params=_cparams("parallel", "arbitrary"),
    )(q, k, v, qi, kiwi, kiwi)


def _moba_prompt_kernel(q_ref, k_ref, v_ref, o_ref, kmean_ref, k16_ref, v16_ref, *, ksel):
    tq = q_ref.shape[0]
    s_len, width = k_ref.shape
    nb = s_len // MOBA_BLOCK
    own_id = pl.program_id(1)

    @pl.when(own_id == 0)
    def _():
        k = k_ref[...]
        k16_ref[...] = k.astype(BF16)
        v16_ref[...] = v_ref[...].astype(BF16)
        kmean_ref[...] = jnp.mean(k.reshape(nb, MOBA_BLOCK, width), axis=1)

    q16 = q_ref[...].astype(BF16)
    diagonal = jnp.where(lax.broadcasted_iota(I32, (tq, MOBA_BLOCK), 1)
                         <= lax.broadcasted_iota(I32, (tq, MOBA_BLOCK), 0), 1.0, 0.0)

    def attend(own):
        nidx = lax.broadcasted_iota(I32, (tq, nb), 1)
        for h in range(B_HEADS):
            sl = slice(h * B_HD, (h + 1) * B_HD)
            q = q16[:, sl]
            g = jnp.where(nidx < own, _dot_nt(q, kmean_ref[:, sl].astype(BF16)), NEG_INF)
            rank = jnp.zeros((tq, nb), I32)
            for m in range(own):
                gm = g[:, m:m + 1]
                rank += jnp.where((gm > g) | ((gm == g) & (m < nidx)), 1, 0)
            chosen = jnp.where((nidx < own) & (rank < ksel), 1.0, 0.0)
            mask = jnp.concatenate(
                [jnp.broadcast_to(chosen[:, n:n + 1], (tq, MOBA_BLOCK)) for n in range(own)] + [diagonal], axis=1)
            n_keys = (own + 1) * MOBA_BLOCK
            o = _masked_attention(q, k16_ref[0:n_keys, sl], v16_ref[0:n_keys, sl], mask > 0.5, B_HD ** -0.5)
            o_ref[:, sl] = o.astype(o_ref.dtype)

    for own in range(nb):
        pl.when(own_id == own)(functools.partial(attend, own))


def moba_prompt(q, k, v):
    b, s, _ = q.shape
    nb = s // MOBA_BLOCK
    tq = MOBA_BLOCK
    blk = lambda bi, i: (bi, i, 0)
    full = lambda bi, i: (bi, 0, 0)
    return pl.pallas_call(
        functools.partial(_moba_prompt_kernel, ksel=max(1, min(MOBA_TOPK, nb - 1))),
        grid=(b, nb),
        in_specs=[pl.BlockSpec((None, tq, B_WIDTH), blk),
                  pl.BlockSpec((None, s, B_WIDTH), full),
                  pl.BlockSpec((None, s, B_WIDTH), full)],
        out_specs=pl.BlockSpec((None, tq, B_WIDTH), blk),
        out_shape=jax.ShapeDtypeStruct((b, s, B_WIDTH), BF16),
        scratch_shapes=[pltpu.VMEM((nb, B_WIDTH), F32), pltpu.VMEM((s, B_WIDTH), BF16),
                        pltpu.VMEM((s, B_WIDTH), BF16)],
        compiler_params=_cparams("parallel", "arbitrary"),
    )(q, k, v)


def _split3(x):
    hi = x.astype(BF16)
    r1 = x - hi.astype(F32)
    mid = r1.astype(BF16)
    lo = (r1 - mid.astype(F32)).astype(BF16)
    return hi, mid, lo


def _dot_tn(a, b):
    return lax.dot_general(a, b, (((0,), (0,)), ((), ())), preferred_element_type=F32)


def _wkv_chunk(r, ld, k, v, kkr, a, states, c):
    heads = range(len(states))
    hs = lambda x, h: x[:, h * C_HD:(h + 1) * C_HD]
    row = lax.broadcasted_iota(I32, (c, c), 0)
    col = lax.broadcasted_iota(I32, (c, c), 1)
    strict = row > col
    incl = row >= col
    incl16 = jnp.where(incl, 1.0, 0.0).astype(BF16)
    eye = jnp.where(row == col, 1.0, 0.0)
    hi, mid, lo = _split3(ld)
    cum = _dot(incl16, hi) + _dot(incl16, mid) + _dot(incl16, lo)
    g_inc = jnp.exp(cum)
    g_exc = jnp.exp(cum - ld)
    g_inv = jnp.exp(-cum)
    g_end = g_inc[c - 1:c, :]
    rt = r * g_inc
    kt = k * g_inv
    k_end = kt * g_end

    kk = []
    for h in heads:
        x = hs(kkr, h)
        kk.append(x / jnp.maximum(jnp.sqrt(jnp.sum(x * x, axis=-1, keepdims=True)), 1e-12))
    at16 = [(-kk[h] * hs(g_exc, h)).astype(BF16) for h in heads]
    bt = [kk[h] * hs(a, h) * hs(g_inv, h) for h in heads]
    ar16 = [jnp.concatenate([at16[h], hs(rt, h).astype(BF16)], axis=0) for h in heads]
    bk16 = [jnp.concatenate([bt[h], hs(kt, h)], axis=0).astype(BF16) for h in heads]
    gram = [_dot_nt(ar16[h], bk16[h]) for h in heads]
    l_ab = [jnp.where(strict, gram[h][:c, :c], 0.0) for h in heads]
    l_ak16 = [jnp.where(strict, gram[h][:c, c:], 0.0).astype(BF16) for h in heads]
    col2 = lax.broadcasted_iota(I32, (c, 2 * c), 1)
    incl2 = lax.broadcasted_iota(I32, (c, 2 * c), 0) >= jnp.where(col2 >= c, col2 - c, col2)
    l_r16 = [jnp.where(incl2, gram[h][c:, :], 0.0).astype(BF16) for h in heads]
    l_rb16 = [l_r16[h][:, :c] for h in heads]
    inv = [eye + l_ab[h] for h in heads]
    lp16 = [l_ab[h].astype(BF16) for h in heads]
    n = 2
    while n < c:
        lp16 = [_dot(lp16[h], lp16[h]).astype(BF16) for h in heads]
        inv = [inv[h] + _dot(inv[h].astype(BF16), lp16[h]) for h in heads]
        n *= 2
    inv16 = [inv[h].astype(BF16) for h in heads]
    v16 = [hs(v, h).astype(BF16) for h in heads]
    lakv16 = [_dot(l_ak16[h], v16[h]).astype(BF16) for h in heads]
    w16 = [_dot(inv16[h], at16[h]).astype(BF16) for h in heads]
    u0_16 = [_dot(inv16[h], lakv16[h]).astype(BF16) for h in heads]
    r2_16 = [(hs(rt, h) + _dot(l_rb16[h], w16[h])).astype(BF16) for h in heads]
    uv16 = [jnp.concatenate([u0_16[h], v16[h]], axis=0) for h in heads]
    y0 = [_dot(l_r16[h], uv16[h]) for h in heads]
    bend16 = [(bt[h] * hs(g_end, h)).astype(BF16) for h in heads]
    bkend16 = [jnp.concatenate([bend16[h], hs(k_end, h).astype(BF16)], axis=0) for h in heads]
    m2_16 = [_dot_tn(w16[h], bend16[h]).astype(BF16) for h in heads]
    nmat = [_dot_tn(uv16[h], bkend16[h]) for h in heads]
    s16 = [states[h].astype(BF16) for h in heads]
    ys = [_dot_nt(r2_16[h], s16[h]) + y0[h] for h in heads]
    new_states = [states[h] * hs(g_end, h) + _dot(s16[h], m2_16[h]) + nmat[h] for h in heads]
    return ys, new_states


WKV_VECS = ('c_w0', 'c_a0', 'c_k_k', 'c_k_a', 'c_ln_w', 'c_ln_b', 'c_r_k')


def _wkv_kernel(p_ref, prev_ref, mu_ref, vec_ref, w2_ref, a2_ref, g2_ref, s0_ref,
                o_ref, sfin_ref, state_ref, last_ref, *, chunk, valid_len):
    tb = p_ref.shape[0]
    t_blk = pl.program_id(1)
    hs = lambda x, h: x[:, h * C_HD:(h + 1) * C_HD]
    w0, a0, k_k, k_a, ln_w, ln_b, r_k = (vec_ref[i:i + 1, :] for i in range(len(WKV_VECS)))

    @pl.when(t_blk == 0)
    def _():
        state_ref[...] = s0_ref[...]
        last_ref[...] = prev_ref[...]

    def chunk_step(ci, prev_row):
        start = pl.multiple_of(ci * chunk, chunk)
        rows = pl.ds(start, chunk)
        pc = p_ref[rows, :]
        first = lax.broadcasted_iota(I32, pc.shape, 0) == 0
        shifted = jnp.where(first, prev_row, pltpu.roll(pc, 1, 0))
        xs = pc + mu_ref[...] * (shifted - pc)
        r = xs[:, 0:C_WIDTH]
        k = xs[:, C_WIDTH:2 * C_WIDTH]
        v = xs[:, 2 * C_WIDTH:3 * C_WIDTH]
        lora = xs[:, LORA_OFF:C_BLOCK]
        z = w0 + _dot(jnp.tanh(lora).astype(BF16), w2_ref[...])
        ld = -jnp.exp(jnp.minimum(z, 0.0) - jnp.log1p(jnp.exp(-jnp.abs(z))) - 0.5)
        a = 1.0 / (1.0 + jnp.exp(-(a0 + _dot(lora.astype(BF16), a2_ref[...]))))
        g = _dot((1.0 / (1.0 + jnp.exp(-lora))).astype(BF16), g2_ref[...])
        kkr = k * k_k
        k = k * (1.0 + (a - 1.0) * k_a)
        if valid_len is not None:
            live = (t_blk * tb + start + lax.broadcasted_iota(I32, k.shape, 0)) < valid_len
            ld, kkr, k, v = (jnp.where(live, x, 0.0) for x in (ld, kkr, k, v))
        states = [state_ref[h] for h in range(C_HEADS)]
        ys, new_states = _wkv_chunk(r, ld, k, v, kkr, a, states, chunk)
        for h in range(C_HEADS):
            state_ref[h] = new_states[h]
        rkr = r * k * r_k
        outs = []
        for h in range(C_HEADS):
            y = ys[h]
            mean = jnp.mean(y, axis=-1, keepdims=True)
            var = jnp.mean(jnp.square(y - mean), axis=-1, keepdims=True)
            yn = (y - mean) * lax.rsqrt(var + GN_EPS)
            bonus = jnp.sum(hs(rkr, h), axis=-1, keepdims=True) * hs(v, h)
            outs.append(yn * hs(ln_w, h) + hs(ln_b, h) + bonus)
        o_ref[rows, :] = (jnp.concatenate(outs, axis=1) * g).astype(o_ref.dtype)
        return pc[chunk - 1:chunk, :]

    last_ref[...] = lax.fori_loop(0, tb // chunk, chunk_step, last_ref[...])

    @pl.when(t_blk == pl.num_programs(1) - 1)
    def _():
        sfin_ref[...] = state_ref[...]


def wkv7(p, prev, mu, vecs, w2, a2, g2, s0, chunk, tb, valid_len):
    b, s, _ = p.shape
    const = lambda shape: pl.BlockSpec(shape, lambda bi, i: (0,) * len(shape))
    st = pl.BlockSpec((None, C_HEADS, C_HD, C_HD), lambda bi, i: (bi, 0, 0, 0))
    return pl.pallas_call(
        functools.partial(_wkv_kernel, chunk=chunk, valid_len=valid_len if valid_len < s else None),
        grid=(b, s // tb),
        in_specs=[pl.BlockSpec((None, tb, C_BLOCK), lambda bi, i: (bi, i, 0)),
                  pl.BlockSpec((None, 1, C_BLOCK), lambda bi, i: (bi, 0, 0)),
                  const(mu.shape), const(vecs.shape), const(w2.shape), const(a2.shape), const(g2.shape), st],
        out_specs=[pl.BlockSpec((None, tb, C_WIDTH), lambda bi, i: (bi, i, 0)), st],
        out_shape=[jax.ShapeDtypeStruct((b, s, C_WIDTH), BF16),
                   jax.ShapeDtypeStruct((b, C_HEADS, C_HD, C_HD), F32)],
        scratch_shapes=[pltpu.VMEM((C_HEADS, C_HD, C_HD), F32), pltpu.VMEM((1, C_BLOCK), F32)],
        compiler_params=_cparams("parallel", "arbitrary"),
    )(p, prev, mu, vecs, w2, a2, g2, s0)


def _split2(x):
    hi = x.astype(BF16)
    return hi, (x - hi.astype(F32)).astype(BF16)


def _page_spec(rows, width, layer, slot, n_pages):
    last_group = n_pages // SAMPLE_PAGES_PER_STEP - 1

    def index_map(b, j, pt):
        return (layer, pt[b, jnp.minimum(j, last_group) * SAMPLE_PAGES_PER_STEP + slot], 0, 0)
    return pl.BlockSpec((None, None, rows, width), index_map)


def _idx_sample_kernel(pt_ref, qi_ref, w_ref, knew_ref, *rest, t):
    pages, o_ref = rest[:-1], rest[-1]
    j = pl.program_id(1)
    last = pl.num_programs(1) - 1
    q_hi, q_lo = _split2(qi_ref[...])
    w = w_ref[...]

    def score(kpage):
        k_hi, k_lo = _split2(kpage)
        s = _dot(q_hi, k_hi) + _dot(q_hi, k_lo) + _dot(q_lo, k_hi)
        s = jnp.maximum(s, 0.0) * w
        return jnp.sum(s.reshape(t, IDX_HEADS, PAGE_SIZE), axis=1)

    @pl.when(j < last)
    def _():
        for i, page in enumerate(pages):
            o_ref[:, i * PAGE_SIZE:(i + 1) * PAGE_SIZE] = score(page[...])

    @pl.when(j == last)
    def _():
        o_ref[...] = jnp.zeros(o_ref.shape, F32)
        o_ref[:, 0:PAGE_SIZE] = score(knew_ref[...])


def idx_sample(page_table, qi, w, knew, cache_idx, layer):
    db, rows, _ = qi.shape
    t = rows // IDX_HEADS
    n_pages = page_table.shape[1]
    pg = SAMPLE_PAGES_PER_STEP
    steps = n_pages // pg + 1
    per_b = lambda b, j, pt: (b, 0, 0)
    return pl.pallas_call(
        functools.partial(_idx_sample_kernel, t=t),
        grid_spec=pltpu.PrefetchScalarGridSpec(
            num_scalar_prefetch=1,
            grid=(db, steps),
            in_specs=[pl.BlockSpec((None, rows, IDX_HD), per_b),
                      pl.BlockSpec((None, rows, 1), per_b),
                      pl.BlockSpec((None, IDX_HD, PAGE_SIZE), per_b)]
                     + [_page_spec(IDX_HD, PAGE_SIZE, layer, i, n_pages) for i in range(pg)],
            out_specs=pl.BlockSpec((None, t, pg * PAGE_SIZE), lambda b, j, pt: (b, 0, j))),
        out_shape=jax.ShapeDtypeStruct((db, t, steps * pg * PAGE_SIZE), F32),
        compiler_params=_cparams("parallel", "arbitrary"),
    )(page_table, qi, w, knew, *([cache_idx] * pg))


def _select_sample_kernel(isc_ref, o_ref, *, topk, t, past_len):
    rows, n = isc_ref.shape
    pos = lax.broadcasted_iota(I32, (rows, n), 1)
    r = pl.program_id(0) * rows + lax.broadcasted_iota(I32, (rows, n), 0)
    causal = pos <= past_len + (r & (t - 1))
    key = _sortable_key(jnp.where(causal, isc_ref[...], NEG_INF))
    mask = _topk_mask(key, pos, topk, n.bit_length()) & causal
    o_ref[...] = jnp.where(mask, 1.0, 0.0)


def select_sample(isc, topk, t, past_len):
    rows, n = isc.shape
    tr = 16
    assert t & (t - 1) == 0 and rows % tr == 0
    return pl.pallas_call(
        functools.partial(_select_sample_kernel, topk=topk, t=t, past_len=past_len),
        grid=(rows // tr,),
        in_specs=[pl.BlockSpec((tr, n), lambda i: (i, 0))],
        out_specs=pl.BlockSpec((tr, n), lambda i: (i, 0)),
        out_shape=jax.ShapeDtypeStruct((rows, n), F32),
        compiler_params=_cparams("parallel"),
    )(isc)


def _own_head(shape, heads):
    return ((lax.broadcasted_iota(I32, shape, 1) & (heads - 1))
            == (lax.broadcasted_iota(I32, shape, 0) & (heads - 1)))


def _dsa_sample_kernel(pt_ref, q_ref, mask_ref, knew_ref, vnew_ref, *rest, heads):
    pg = SAMPLE_PAGES_PER_STEP
    kpages, vpages = rest[:pg], rest[pg:2 * pg]
    o_ref, m_ref, l_ref, acc_ref = rest[2 * pg:]
    j = pl.program_id(1)
    last = pl.num_programs(1) - 1
    q = q_ref[...]
    page_rows = PAGE_SIZE * heads
    shift = heads.bit_length() - 1
    spread = jnp.where(
        jnp.right_shift(lax.broadcasted_iota(I32, (PAGE_SIZE, page_rows), 1), shift)
        == lax.broadcasted_iota(I32, (PAGE_SIZE, page_rows), 0), 1.0, 0.0).astype(BF16)

    @pl.when(j == 0)
    def _():
        m_ref[...] = jnp.full(m_ref.shape, NEG_INF, F32)
        l_ref[...] = jnp.zeros(l_ref.shape, F32)
        acc_ref[...] = jnp.zeros(acc_ref.shape, F32)

    def update(ks, vs, mask):
        s = jnp.concatenate([_dot_nt(q, kp.astype(BF16)) for kp in ks], axis=1) * (A_HD ** -0.5)
        sel = jnp.concatenate([_dot(mask[:, i * PAGE_SIZE:(i + 1) * PAGE_SIZE].astype(BF16), spread)
                               for i in range(len(ks))], axis=1)
        s = jnp.where((sel > 0.5) & _own_head(s.shape, heads), s, NEG_INF)
        m_old = m_ref[...]
        m_new = jnp.maximum(m_old, jnp.max(s, axis=-1, keepdims=True))
        m_safe = jnp.where(m_new == NEG_INF, 0.0, m_new)
        p = jnp.exp(s - m_safe)
        alpha = jnp.exp(m_old - m_safe)
        l_ref[...] = alpha * l_ref[...] + jnp.sum(p, axis=-1, keepdims=True)
        acc = alpha * acc_ref[...]
        for i, vp in enumerate(vs):
            acc += _dot(p[:, i * page_rows:(i + 1) * page_rows].astype(BF16), vp.astype(BF16))
        acc_ref[...] = acc
        m_ref[...] = m_new

    @pl.when(j < last)
    def _():
        update([kp[...] for kp in kpages], [vp[...] for vp in vpages], mask_ref[...])

    @pl.when(j == last)
    def _():
        update([knew_ref[...]], [vnew_ref[...]], mask_ref[:, 0:PAGE_SIZE])
        o_ref[...] = acc_ref[...] / l_ref[...]


def dsa_sample(page_table, q, mask, knew, vnew, cache_k, cache_v, layer, heads):
    db, rows, hd = q.shape
    n_pages = page_table.shape[1]
    pg = SAMPLE_PAGES_PER_STEP
    steps = n_pages // pg + 1
    page_rows = PAGE_SIZE * heads
    per_b = lambda b, j, pt: (b, 0, 0)
    return pl.pallas_call(
        functools.partial(_dsa_sample_kernel, heads=heads),
        grid_spec=pltpu.PrefetchScalarGridSpec(
            num_scalar_prefetch=1,
            grid=(db, steps),
            in_specs=[pl.BlockSpec((None, rows, hd), per_b),
                      pl.BlockSpec((None, rows, pg * PAGE_SIZE), lambda b, j, pt: (b, 0, j)),
                      pl.BlockSpec((None, page_rows, hd), per_b),
                      pl.BlockSpec((None, page_rows, hd), per_b)]
                     + [_page_spec(page_rows, hd, layer, i, n_pages) for i in range(pg)]
                     + [_page_spec(page_rows, hd, layer, i, n_pages) for i in range(pg)],
            out_specs=pl.BlockSpec((None, rows, hd), per_b),
            scratch_shapes=[pltpu.VMEM((rows, 1), F32), pltpu.VMEM((rows, 1), F32),
                            pltpu.VMEM((rows, hd), F32)]),
        out_shape=jax.ShapeDtypeStruct((db, rows, hd), F32),
        compiler_params=_cparams("parallel", "arbitrary"),
    )(page_table, q, mask, knew, vnew, *([cache_k] * pg), *([cache_v] * pg))


def _moba_sample_kernel(pt_ref, q_ref, knew_ref, vnew_ref, *rest, heads, ksel):
    pg = SAMPLE_PAGES_PER_STEP
    ppb = MOBA_BLOCK // PAGE_SIZE
    kpages, vpages = rest[:pg], rest[pg:2 * pg]
    o_ref, g_ref, m_ref, l_ref, ob_ref = rest[2 * pg:]
    j = pl.program_id(1)
    last = pl.num_programs(1) - 1
    q = q_ref[...]
    q16 = q.astype(BF16)
    rows, hd = q.shape
    scale = B_HD ** -0.5
    sublanes = 8
    assert 2 * heads == sublanes and hd == LANES and rows % sublanes == 0
    wide = lambda x: jnp.broadcast_to(x, (rows, LANES))

    def partial_softmax(s, v16):
        m = jnp.max(s, axis=-1, keepdims=True)
        p = jnp.exp(s - m)
        return m, jnp.sum(p, axis=-1, keepdims=True), _dot(p.astype(BF16), v16)

    @pl.when(j < last)
    def _():
        for blk in range(pg // ppb):
            kblk = jnp.concatenate([kpages[blk * ppb + i][...] for i in range(ppb)], axis=0)
            v16 = jnp.concatenate([vpages[blk * ppb + i][...] for i in range(ppb)], axis=0).astype(BF16)
            s = _dot_nt(q16, kblk.astype(BF16)) * scale
            m, l, o = partial_softmax(jnp.where(_own_head(s.shape, heads), s, NEG_INF), v16)
            n = j * (pg // ppb) + blk
            fold = jnp.sum(kblk.reshape(kblk.shape[0] // sublanes, sublanes, hd), axis=0)
            kmean = (fold + pltpu.roll(fold, heads, 0)) * (1.0 / MOBA_BLOCK)
            kmean = jnp.concatenate([kmean] * (rows // sublanes), axis=0)
            g_ref[n] = wide(jnp.sum(q * kmean, axis=-1, keepdims=True))
            m_ref[n] = wide(m)
            l_ref[n] = wide(l)
            ob_ref[n] = o

    @pl.when(j == last)
    def _():
        shift = heads.bit_length() - 1
        s = _dot_nt(q16, knew_ref[...].astype(BF16)) * scale
        causal = (jnp.right_shift(lax.broadcasted_iota(I32, s.shape, 1), shift)
                  <= jnp.right_shift(lax.broadcasted_iota(I32, s.shape, 0), shift))
        s = jnp.where(causal & _own_head(s.shape, heads), s, NEG_INF)
        m_own, l_own, o_own = partial_softmax(s, vnew_ref[...].astype(BF16))
        m_own, l_own = wide(m_own), wide(l_own)
        gates = g_ref[...]
        nidx = lax.broadcasted_iota(I32, gates.shape, 0)
        chosen = jnp.zeros(gates.shape, F32)
        for _ in range(ksel):
            best = jnp.max(gates, axis=0, keepdims=True)
            first = jnp.min(jnp.where(gates == best, nidx, gates.shape[0]), axis=0, keepdims=True)
            pick = nidx == first
            chosen = jnp.where(pick, 1.0, chosen)
            gates = jnp.where(pick, NEG_INF, gates)
        m_blk = m_ref[...]
        m_tot = jnp.maximum(m_own, jnp.max(jnp.where(chosen > 0.5, m_blk, NEG_INF), axis=0))
        wgt = jnp.where(chosen > 0.5, jnp.exp(m_blk - m_tot[None]), 0.0)
        w_own = jnp.exp(m_own - m_tot)
        l_tot = w_own * l_own + jnp.sum(wgt * l_ref[...], axis=0)
        acc = w_own * o_own
        for n in range(gates.shape[0]):
            acc += wgt[n] * ob_ref[n]
        o_ref[...] = acc / l_tot


def moba_sample(page_table, q, knew, vnew, cache_k, cache_v, layer, heads):
    db, rows, hd = q.shape
    n_pages = page_table.shape[1]
    pg = SAMPLE_PAGES_PER_STEP
    steps = n_pages // pg + 1
    nbp = n_pages * PAGE_SIZE // MOBA_BLOCK
    page_rows = PAGE_SIZE * heads
    per_b = lambda b, j, pt: (b, 0, 0)
    return pl.pallas_call(
        functools.partial(_moba_sample_kernel, heads=heads, ksel=max(1, min(MOBA_TOPK, nbp))),
        grid_spec=pltpu.PrefetchScalarGridSpec(
            num_scalar_prefetch=1,
            grid=(db, steps),
            in_specs=[pl.BlockSpec((None, rows, hd), per_b),
                      pl.BlockSpec((None, page_rows, hd), per_b),
                      pl.BlockSpec((None, page_rows, hd), per_b)]
                     + [_page_spec(page_rows, hd, layer, i, n_pages) for i in range(pg)]
                     + [_page_spec(page_rows, hd, layer, i, n_pages) for i in range(pg)],
            out_specs=pl.BlockSpec((None, rows, hd), per_b),
            scratch_shapes=[pltpu.VMEM((nbp, rows, LANES), F32), pltpu.VMEM((nbp, rows, LANES), F32),
                            pltpu.VMEM((nbp, rows, LANES), F32), pltpu.VMEM((nbp, rows, hd), F32)]),
        out_shape=jax.ShapeDtypeStruct((db, rows, hd), F32),
        compiler_params=_cparams("parallel", "arbitrary"),
    )(page_table, q, knew, vnew, *([cache_k] * pg), *([cache_v] * pg))


def _new_page(x, rows):
    return jnp.pad(x, ((0, 0), (0, rows - x.shape[1]), (0, 0)))


def _sample_attention(qa, ka, va, qi, kiwi, qb, kb, vb, caches, page_table, layer, db, t):
    cache_a_k, cache_a_v, cache_a_idx, cache_b_k, cache_b_v = caches
    past_len = page_table.shape[1] * PAGE_SIZE
    topk = min(A_TOPK_MAX, (past_len + t) // 4)
    ki = kiwi[:, :IDX_HD].reshape(db, t, IDX_HD)
    wi = kiwi[:, IDX_HD:IDX_HD + IDX_HEADS]
    knew = _new_page(ki, PAGE_SIZE).transpose(0, 2, 1)
    isc = idx_sample(page_table, qi.reshape(db, t * IDX_HEADS, IDX_HD), wi.reshape(db, t * IDX_HEADS, 1),
                     knew, cache_a_idx, layer)
    n = isc.shape[-1]
    mask = select_sample(isc.reshape(db * t, n), topk, t, past_len).reshape(db, t, n)
    rows_a = lambda z: z.reshape(db, t * A_HEADS, A_HD)
    rows_b = lambda z: z.reshape(db, t * B_HEADS, B_HD)
    oa = dsa_sample(page_table, rows_a(qa).astype(BF16), jnp.repeat(mask, A_HEADS, axis=1),
                    _new_page(rows_a(ka), PAGE_SIZE * A_HEADS), _new_page(rows_a(va), PAGE_SIZE * A_HEADS),
                    cache_a_k, cache_a_v, layer, A_HEADS)
    ob = moba_sample(page_table, rows_b(qb), _new_page(rows_b(kb), PAGE_SIZE * B_HEADS),
                     _new_page(rows_b(vb), PAGE_SIZE * B_HEADS), cache_b_k, cache_b_v, layer, B_HEADS)
    return oa.reshape(db * t, A_WIDTH), ob.reshape(db * t, B_WIDTH)


def _pad_cols(x, width):
    return jnp.pad(x, [(0, 0)] * (x.ndim - 1) + [(0, width - x.shape[-1])])


def _layer_params(l, norm1_g, w_in, a_q_norm, a_k_norm, idx_k_norm, b_q_norm, b_k_norm, c_mu, c_w0, c_w2, c_a0,
                  c_a2, c_g2, c_k_k, c_k_a, c_r_k, c_ln_w, c_ln_b, w_o, norm2_g, w_ff1, w_ff2):
    w = w_in[l]
    a_cols, b_cols, c_cols = w[:, :A_COLS], w[:, A_COLS:A_COLS + B_COLS], w[:, A_COLS + B_COLS:]
    w_pack = jnp.concatenate([_pad_cols(c_cols, A_OFF), _pad_cols(a_cols, B_OFF - A_OFF), b_cols],
                             axis=1).astype(BF16)
    lora_rows = lambda x, off: jnp.pad(x, ((off, LORA_BLOCK - off - x.shape[0]), (0, 0))).astype(BF16)
    vec = dict(c_w0=c_w0[l], c_a0=c_a0[l], c_k_k=c_k_k[l], c_k_a=c_k_a[l], c_ln_w=c_ln_w[l], c_ln_b=c_ln_b[l],
               c_r_k=c_r_k[l].reshape(-1))
    vecs = jnp.stack([vec[name] for name in WKV_VECS] + [jnp.zeros((C_WIDTH,), F32)])
    gains = jnp.stack([a_q_norm[l], a_k_norm[l], b_q_norm[l], b_k_norm[l], _pad_cols(idx_k_norm[l], LANES)]
                      + [jnp.zeros((LANES,), F32)] * 3)
    return dict(norm1_g=norm1_g[l], w_in=w_pack, gains=gains, mu=_pad_cols(c_mu[l], C_BLOCK)[None], vecs=vecs,
                w2=lora_rows(c_w2[l], 0), a2=lora_rows(c_a2[l], W_LORA), g2=lora_rows(c_g2[l], W_LORA + A_LORA),
                w_o=w_o[l].astype(BF16), norm2_g=norm2_g[l], w_ff1=w_ff1[l].astype(BF16),
                w_ff2=w_ff2[l].astype(BF16))


def _layer(x, tabs, prev_shift, wkv0, lw, sample_ctx):
    b, s, d = x.shape
    m = b * s
    p = norm_matmul(x.reshape(m, d), lw['norm1_g'], lw['w_in'], min(m, 1024), PROJ_TILE_N)
    qa, ka, va, qi, kiwi, qb, kb, vb = prep(p, tabs[0], tabs[1], lw['gains'], min(m, 256))
    p3 = p.reshape(b, s, P_PACK)
    if sample_ctx is None:
        seq = lambda z: z.reshape(b, s, z.shape[-1])
        oa = dsa_prompt(seq(qa), seq(ka), seq(va), seq(qi), seq(kiwi)).reshape(m, A_WIDTH)
        ob = moba_prompt(seq(qb), seq(kb), seq(vb)).reshape(m, B_WIDTH)
        p_scan = p3
    else:
        caches, page_table, layer = sample_ctx
        oa, ob = _sample_attention(qa, ka, va, qi, kiwi, qb, kb, vb, caches, page_table, layer, b, s)
        oa, ob = oa.astype(BF16), ob.astype(BF16)
        p_scan = jnp.pad(p3, ((0, 0), (0, -s % 16), (0, 0)))
    sp = p_scan.shape[1]
    chunk = min(WKV_CHUNK, sp)
    oc, wkv_fin = wkv7(p_scan, _pad_cols(prev_shift, C_BLOCK)[:, None, :], lw['mu'], lw['vecs'], lw['w2'],
                       lw['a2'], lw['g2'], wkv0, chunk, min(sp, 4 * chunk), s)
    tm2 = min(m, 512)
    x2 = out_proj(x.reshape(m, d), oa, ob, oc[:, :s].reshape(m, C_WIDTH), lw['w_o'], tm2)
    x3 = mlp(x2, lw['norm2_g'], lw['w_ff1'], lw['w_ff2'], tm2, min(MLP_TILE_F, lw['w_ff1'].shape[1])).reshape(b, s, d)
    rows = (ka.reshape(b, s, A_HEADS, A_HD), va.reshape(b, s, A_HEADS, A_HD),
            kiwi[:, :IDX_HD].reshape(b, s, IDX_HD), kb.reshape(b, s, B_HEADS, B_HD),
            vb.reshape(b, s, B_HEADS, B_HD), wkv_fin, p3[:, s - 1, :C_COLS])
    return x3, rows


def kernel(x_prompt, x_sample, cache_a_k, cache_a_v, cache_a_idx, cache_b_k, cache_b_v, state_wkv, state_shift, page_table, norm1_g, w_in, a_q_norm, a_k_norm, idx_k_norm, b_q_norm, b_k_norm, c_mu, c_w0, c_w2, c_a0, c_a2, c_g2, c_k_k, c_k_a, c_r_k, c_ln_w, c_ln_b, w_o, norm2_g, w_ff1, w_ff2):
    depth = w_in.shape[0]
    bp, sp = x_prompt.shape[:2]
    db, t = x_sample.shape[:2]
    pos_p = jnp.arange(sp)
    pos_s = jnp.tile(page_table.shape[1] * PAGE_SIZE + jnp.arange(t), db)
    tabs_p = (_rope_tables(pos_p, A_HD, 1), _rope_tables(pos_p, IDX_HD, LANES // IDX_HD))
    tabs_s = (_rope_tables(pos_s, A_HD, 1), _rope_tables(pos_s, IDX_HD, LANES // IDX_HD))
    shift0 = jnp.zeros((bp, C_COLS), x_prompt.dtype)
    wkv0 = jnp.zeros((bp, C_HEADS, C_HD, C_HD), F32)
    xp, xs = x_prompt, x_sample
    by_row = lambda c: c.reshape(c.shape[0], c.shape[1], -1, c.shape[-1])
    caches = (by_row(cache_a_k), by_row(cache_a_v), cache_a_idx.transpose(0, 1, 3, 2),
              by_row(cache_b_k), by_row(cache_b_v))
    p_rows, s_rows = [], []
    for l in range(depth):
        lw = _layer_params(l, norm1_g, w_in, a_q_norm, a_k_norm, idx_k_norm, b_q_norm, b_k_norm, c_mu, c_w0, c_w2,
                           c_a0, c_a2, c_g2, c_k_k, c_k_a, c_r_k, c_ln_w, c_ln_b, w_o, norm2_g, w_ff1, w_ff2)
        xp, rows = _layer(xp, tabs_p, shift0, wkv0, lw, None)
        p_rows.append(rows)
        xs, rows = _layer(xs, tabs_s, state_shift[l], state_wkv[l], lw, (caches, page_table, l))
        s_rows.append(rows)
    p_out = [jnp.stack(z) for z in zip(*p_rows)]
    s_out = [jnp.stack(z) for z in zip(*s_rows)]
    return (xp, xs, *p_out, *s_out)
```

```python
import functools

import jax
import jax.numpy as jnp
from jax import lax
from jax.experimental import pallas as pl
from jax.experimental.pallas import tpu as pltpu

F32 = jnp.float32
BF16 = jnp.bfloat16
I32 = jnp.int32

VMEM_LIMIT_BYTES = 56 * 1024 * 1024
LANES = 128

PAGE_SIZE = 128
A_HEADS = 4
A_HD = 128
A_WIDTH = A_HEADS * A_HD
IDX_HEADS = 16
IDX_HD = 64
A_TOPK_MAX = 256
A_Q_BLOCK = 128
B_HEADS = 4
B_HD = 128
B_WIDTH = B_HEADS * B_HD
MOBA_BLOCK = 256
MOBA_TOPK = 3
C_HD = 64
C_HEADS = 16
C_WIDTH = C_HEADS * C_HD
W_LORA = 64
A_LORA = 64
G_LORA = 160
A_COLS = 3 * A_WIDTH + IDX_HEADS * IDX_HD + IDX_HD + IDX_HEADS
B_COLS = 3 * B_WIDTH
C_COLS = 3 * C_WIDTH + W_LORA + A_LORA + G_LORA
ROPE_THETA = 500000.0
NORM_EPS = 1e-6
GN_EPS = 64e-5
IDX_W_SCALE = (IDX_HEADS * IDX_HD) ** -0.5
WKV_CHUNK = 64
SAMPLE_PAGES_PER_STEP = 16
DSA_KEY_BUCKET = 256
PROJ_TILE_N = 1536
MLP_TILE_F = 1024

C_BLOCK = C_COLS + (-C_COLS % LANES)
A_OFF = C_BLOCK
B_OFF = A_OFF + A_COLS + (-A_COLS % LANES)
P_PACK = B_OFF + B_COLS
assert P_PACK % PROJ_TILE_N == 0
LORA_OFF = 3 * C_WIDTH
LORA_BLOCK = C_BLOCK - LORA_OFF
A_QI_OFF = A_OFF + 3 * A_WIDTH
A_KI_OFF = A_QI_OFF + IDX_HEADS * IDX_HD
assert A_KI_OFF + LANES == B_OFF and IDX_HD + IDX_HEADS <= LANES

NEG_INF = float("-inf")
INT_MIN = -2 ** 31


def _cparams(*sem):
    return pltpu.CompilerParams(dimension_semantics=sem, vmem_limit_bytes=VMEM_LIMIT_BYTES)


def _dot_nt(a, b):
    return lax.dot_general(a, b, (((1,), (1,)), ((), ())), preferred_element_type=F32)


def _dot(a, b):
    return jnp.dot(a, b, preferred_element_type=F32)


def _norm_matmul_kernel(x_ref, g_ref, w_ref, o_ref, xn_ref):
    @pl.when(pl.program_id(1) == 0)
    def _():
        x = x_ref[...]
        y = x * lax.rsqrt(jnp.mean(x * x, axis=-1, keepdims=True) + NORM_EPS)
        xn_ref[...] = (y * g_ref[...]).astype(BF16)

    o_ref[...] = _dot(xn_ref[...], w_ref[...])


def norm_matmul(x, g, w, tm, tn):
    m, d = x.shape
    p = w.shape[1]
    return pl.pallas_call(
        _norm_matmul_kernel,
        grid=(m // tm, p // tn),
        in_specs=[pl.BlockSpec((tm, d), lambda i, j: (i, 0)),
                  pl.BlockSpec((1, d), lambda i, j: (0, 0)),
                  pl.BlockSpec((d, tn), lambda i, j: (0, j))],
        out_specs=pl.BlockSpec((tm, tn), lambda i, j: (i, j)),
        out_shape=jax.ShapeDtypeStruct((m, p), F32),
        scratch_shapes=[pltpu.VMEM((tm, d), BF16)],
        compiler_params=_cparams("parallel", "arbitrary"),
    )(x, g.reshape(1, d), w)


def _out_proj_kernel(x_ref, oa_ref, ob_ref, oc_ref, w_ref, o_ref):
    wa = oa_ref.shape[1]
    wb = ob_ref.shape[1]
    acc = _dot(oa_ref[...], w_ref[0:wa, :])
    acc += _dot(ob_ref[...], w_ref[wa:wa + wb, :])
    acc += _dot(oc_ref[...], w_ref[wa + wb:, :])
    o_ref[...] = x_ref[...] + acc


def out_proj(x, oa, ob, oc, w, tm):
    m, d = x.shape
    row = lambda i: (i, 0)
    return pl.pallas_call(
        _out_proj_kernel,
        grid=(m // tm,),
        in_specs=[pl.BlockSpec((tm, d), row),
                  pl.BlockSpec((tm, oa.shape[1]), row),
                  pl.BlockSpec((tm, ob.shape[1]), row),
                  pl.BlockSpec((tm, oc.shape[1]), row),
                  pl.BlockSpec(w.shape, lambda i: (0, 0))],
        out_specs=pl.BlockSpec((tm, d), row),
        out_shape=jax.ShapeDtypeStruct((m, d), F32),
        compiler_params=_cparams("parallel"),
    )(x, oa, ob, oc, w)


def _mlp_kernel(x_ref, g_ref, w1_ref, w2_ref, o_ref, xn_ref):
    @pl.when(pl.program_id(1) == 0)
    def _():
        x = x_ref[...]
        y = x * lax.rsqrt(jnp.mean(x * x, axis=-1, keepdims=True) + NORM_EPS)
        xn_ref[...] = (y * g_ref[...]).astype(BF16)
        o_ref[...] = x

    h = jnp.maximum(_dot(xn_ref[...], w1_ref[...]), 0.0)
    o_ref[...] += _dot((h * h).astype(BF16), w2_ref[...])


def mlp(x, g, w1, w2, tm, tf):
    m, d = x.shape
    f = w1.shape[1]
    return pl.pallas_call(
        _mlp_kernel,
        grid=(m // tm, f // tf),
        in_specs=[pl.BlockSpec((tm, d), lambda i, j: (i, 0)),
                  pl.BlockSpec((1, d), lambda i, j: (0, 0)),
                  pl.BlockSpec((d, tf), lambda i, j: (0, j)),
                  pl.BlockSpec((tf, d), lambda i, j: (j, 0))],
        out_specs=pl.BlockSpec((tm, d), lambda i, j: (i, 0)),
        out_shape=jax.ShapeDtypeStruct((m, d), F32),
        scratch_shapes=[pltpu.VMEM((tm, d), BF16)],
        compiler_params=_cparams("parallel", "arbitrary"),
    )(x, g.reshape(1, d), w1, w2)


def _rope_tables(pos, hd, reps):
    rot = hd // 4
    half = rot // 2
    inv = ROPE_THETA ** (-2.0 * jnp.arange(half, dtype=F32) / rot)
    ang = pos.astype(F32)[:, None] * inv[None, :]
    cos, sin = jnp.cos(ang), jnp.sin(ang)
    zeros = jnp.zeros_like(sin)
    rest = jnp.zeros((pos.shape[0], hd - rot), F32)
    tabs = [jnp.concatenate([cos, cos, rest + 1.0], axis=1),
            jnp.concatenate([zeros, sin, rest], axis=1),
            jnp.concatenate([-sin, zeros, rest], axis=1)]
    return jnp.stack([jnp.tile(z, (1, reps)) for z in tabs])


def _rope(y, tab_ref, half):
    width = y.shape[1]
    return (y * tab_ref[0] + pltpu.roll(y, half, 1) * tab_ref[1]
            + pltpu.roll(y, width - half, 1) * tab_ref[2])


def _prep_kernel(p_ref, ta_ref, ti_ref, g_ref, qa_ref, ka_ref, va_ref, qi_ref, kiwi_ref, qb_ref, kb_ref, vb_ref):
    def normed(x, g):
        return x * lax.rsqrt(jnp.mean(x * x, axis=-1, keepdims=True) + NORM_EPS) * g

    for base, q_ref, k_ref, v_ref, heads, hd, gq, gk in (
            (A_OFF, qa_ref, ka_ref, va_ref, A_HEADS, A_HD, 0, 1),
            (B_OFF, qb_ref, kb_ref, vb_ref, B_HEADS, B_HD, 2, 3)):
        width = heads * hd
        for h in range(heads):
            sl = slice(h * hd, (h + 1) * hd)
            q_ref[:, sl] = _rope(normed(p_ref[:, base + h * hd:base + (h + 1) * hd], g_ref[gq:gq + 1, :]),
                                 ta_ref, hd // 8)
            k_ref[:, sl] = _rope(normed(p_ref[:, base + width + h * hd:base + width + (h + 1) * hd],
                                        g_ref[gk:gk + 1, :]), ta_ref, hd // 8)
        v_ref[...] = p_ref[:, base + 2 * width:base + 3 * width]
    for j in range(IDX_HEADS * IDX_HD // LANES):
        sl = slice(j * LANES, (j + 1) * LANES)
        qi_ref[:, sl] = _rope(p_ref[:, A_QI_OFF + j * LANES:A_QI_OFF + (j + 1) * LANES], ti_ref, IDX_HD // 8)
    x = p_ref[:, A_KI_OFF:A_KI_OFF + LANES]
    is_key = lax.broadcasted_iota(I32, x.shape, 1) < IDX_HD
    ms = jnp.sum(jnp.where(is_key, x * x, 0.0), axis=-1, keepdims=True) * (1.0 / IDX_HD)
    ki = _rope(x * lax.rsqrt(ms + NORM_EPS) * g_ref[4:5, :], ti_ref, IDX_HD // 8)
    kiwi_ref[...] = jnp.where(is_key, ki, x * IDX_W_SCALE)


def prep(p, tab_a, tab_i, gains, tm):
    m = p.shape[0]
    period = tab_a.shape[1] // tm
    row = lambda i: (i, 0)
    tab = pl.BlockSpec((3, tm, LANES), lambda i: (0, i % period, 0))
    widths = (A_WIDTH, A_WIDTH, A_WIDTH, IDX_HEADS * IDX_HD, LANES, B_WIDTH, B_WIDTH, B_WIDTH)
    return pl.pallas_call(
        _prep_kernel,
        grid=(m // tm,),
        in_specs=[pl.BlockSpec((tm, P_PACK), row), tab, tab, pl.BlockSpec(gains.shape, lambda i: (0, 0))],
        out_specs=[pl.BlockSpec((tm, w), row) for w in widths],
        out_shape=[jax.ShapeDtypeStruct((m, w), F32) for w in widths],
        compiler_params=_cparams("parallel"),
    )(p, tab_a, tab_i, gains)


def _sortable_key(x):
    bits = pltpu.bitcast(x, I32)
    return jnp.where(bits >= 0, bits, bits ^ jnp.int32(0x7FFFFFFF))


def _topk_mask(key, pos, topk, n_pos_bits):
    rows = key.shape[0]

    def count(m):
        return jnp.sum(jnp.where(m, 1.0, 0.0), axis=-1, keepdims=True).astype(I32)

    groups = 2 if rows % 16 == 0 else 1
    step = rows // groups
    parts = [key[g * step:(g + 1) * step] for g in range(groups)]

    def value_step(it, tus):
        bit = jnp.left_shift(jnp.int32(1), 31 - it)
        out = []
        for part, tu in zip(parts, tus):
            cand_u = tu | bit
            cnt = count(part >= (cand_u ^ jnp.int32(INT_MIN)))
            out.append(jnp.where(cnt >= topk, cand_u, tu))
        return tuple(out)

    tus = lax.fori_loop(0, 32, value_step, tuple(jnp.zeros((step, 1), I32) for _ in range(groups)))
    thr = jnp.concatenate(tus, axis=0) ^ jnp.int32(INT_MIN)
    gt = key > thr
    eq = key == thr
    need = topk - count(gt)
    excess = count(eq) - need

    def tie_search(_):
        def pos_step(it, x):
            cand = x | jnp.left_shift(jnp.int32(1), n_pos_bits - 1 - it)
            cnt = count(eq & (pos < cand))
            return jnp.where(cnt < need, cand, x)
        return lax.fori_loop(0, n_pos_bits, pos_step, jnp.zeros((rows, 1), I32))

    last = lax.cond(jnp.max(excess) > 0, tie_search,
                    lambda _: jnp.full((rows, 1), 2 ** n_pos_bits, I32), 0)
    return gt | (eq & (pos <= last))


def _masked_attention(q, k, v, mask, scale):
    s = _dot_nt(q, k) * scale
    s = jnp.where(mask, s, NEG_INF)
    m = jnp.max(s, axis=-1, keepdims=True)
    p = jnp.exp(s - m)
    l = jnp.sum(p, axis=-1, keepdims=True)
    return _dot(p.astype(BF16), v) / l


def _dsa_prompt_kernel(q_ref, k_ref, v_ref, qi_ref, kiwi_ref, kiwi_blk_ref, o_ref, k16_ref, v16_ref, ki16_ref,
                       *, topk, bucket):
    tq = q_ref.shape[0]
    s_len = k_ref.shape[0]
    i = pl.program_id(1)

    @pl.when(i == 0)
    def _():
        k16_ref[...] = k_ref[...].astype(BF16)
        v16_ref[...] = v_ref[...].astype(BF16)
        ki16_ref[...] = kiwi_ref[:, 0:IDX_HD].astype(BF16)

    wi = kiwi_blk_ref[:, IDX_HD:IDX_HD + IDX_HEADS]
    qi16 = qi_ref[...].astype(BF16)
    q16 = q_ref[...].astype(BF16)

    def attend(n_keys):
        qpos = i * tq + lax.broadcasted_iota(I32, (tq, n_keys), 0)
        kpos = lax.broadcasted_iota(I32, (tq, n_keys), 1)
        causal = kpos <= qpos
        ki = ki16_ref[0:n_keys, :]
        isc = jnp.zeros((tq, n_keys), F32)
        for h in range(IDX_HEADS):
            s = _dot_nt(qi16[:, h * IDX_HD:(h + 1) * IDX_HD], ki)
            isc += jnp.maximum(s, 0.0) * wi[:, h:h + 1]
        key = _sortable_key(jnp.where(causal, isc, NEG_INF))
        mask = _topk_mask(key, kpos, topk, (n_keys - 1).bit_length()) & causal
        for h in range(A_HEADS):
            sl = slice(h * A_HD, (h + 1) * A_HD)
            o = _masked_attention(q16[:, sl], k16_ref[0:n_keys, sl], v16_ref[0:n_keys, sl], mask, A_HD ** -0.5)
            o_ref[:, sl] = o.astype(o_ref.dtype)

    for bkt in range(s_len // bucket):
        pl.when(i // (bucket // tq) == bkt)(functools.partial(attend, (bkt + 1) * bucket))


def dsa_prompt(q, k, v, qi, kiwi):
    b, s, _ = q.shape
    tq = A_Q_BLOCK
    topk = min(A_TOPK_MAX, s // 4)
    bucket = min(s, DSA_KEY_BUCKET)
    assert s % bucket == 0 and bucket % tq == 0 and bucket >= topk
    blk = lambda bi, i: (bi, i, 0)
    full = lambda bi, i: (bi, 0, 0)
    return pl.pallas_call(
        functools.partial(_dsa_prompt_kernel, topk=topk, bucket=bucket),
        grid=(b, s // tq),
        in_specs=[pl.BlockSpec((None, tq, A_WIDTH), blk),
                  pl.BlockSpec((None, s, A_WIDTH), full),
                  pl.BlockSpec((None, s, A_WIDTH), full),
                  pl.BlockSpec((None, tq, IDX_HEADS * IDX_HD), blk),
                  pl.BlockSpec((None, s, LANES), full),
                  pl.BlockSpec((None, tq, LANES), blk)],
        out_specs=pl.BlockSpec((None, tq, A_WIDTH), blk),
        out_shape=jax.ShapeDtypeStruct((b, s, A_WIDTH), BF16),
        scratch_shapes=[pltpu.VMEM((s, A_WIDTH), BF16), pltpu.VMEM((s, A_WIDTH), BF16),
                        pltpu.VMEM((s, IDX_HD), BF16)],
        compiler_params=_cparams("parallel", "arbitrary"),
    )(q, k, v, qi, kiwi, kiwi)


def _moba_prompt_kernel(q_ref, k_ref, v_ref, o_ref, kmean_ref, k16_ref, v16_ref, *, ksel):
    tq = q_ref.shape[0]
    s_len, width = k_ref.shape
    nb = s_len // MOBA_BLOCK
    own_id = pl.program_id(1)

    @pl.when(own_id == 0)
    def _():
        k = k_ref[...]
        k16_ref[...] = k.astype(BF16)
        v16_ref[...] = v_ref[...].astype(BF16)
        kmean_ref[...] = jnp.mean(k.reshape(nb, MOBA_BLOCK, width), axis=1)

    q16 = q_ref[...].astype(BF16)
    diagonal = jnp.where(lax.broadcasted_iota(I32, (tq, MOBA_BLOCK), 1)
                         <= lax.broadcasted_iota(I32, (tq, MOBA_BLOCK), 0), 1.0, 0.0)

    def attend(own):
        nidx = lax.broadcasted_iota(I32, (tq, nb), 1)
        for h in range(B_HEADS):
            sl = slice(h * B_HD, (h + 1) * B_HD)
            q = q16[:, sl]
            g = jnp.where(nidx < own, _dot_nt(q, kmean_ref[:, sl].astype(BF16)), NEG_INF)
            rank = jnp.zeros((tq, nb), I32)
            for m in range(own):
                gm = g[:, m:m + 1]
                rank += jnp.where((gm > g) | ((gm == g) & (m < nidx)), 1, 0)
            chosen = jnp.where((nidx < own) & (rank < ksel), 1.0, 0.0)
            mask = jnp.concatenate(
                [jnp.broadcast_to(chosen[:, n:n + 1], (tq, MOBA_BLOCK)) for n in range(own)] + [diagonal], axis=1)
            n_keys = (own + 1) * MOBA_BLOCK
            o = _masked_attention(q, k16_ref[0:n_keys, sl], v16_ref[0:n_keys, sl], mask > 0.5, B_HD ** -0.5)
            o_ref[:, sl] = o.astype(o_ref.dtype)

    for own in range(nb):
        pl.when(own_id == own)(functools.partial(attend, own))


def moba_prompt(q, k, v):
    b, s, _ = q.shape
    nb = s // MOBA_BLOCK
    tq = MOBA_BLOCK
    blk = lambda bi, i: (bi, i, 0)
    full = lambda bi, i: (bi, 0, 0)
    return pl.pallas_call(
        functools.partial(_moba_prompt_kernel, ksel=max(1, min(MOBA_TOPK, nb - 1))),
        grid=(b, nb),
        in_specs=[pl.BlockSpec((None, tq, B_WIDTH), blk),
                  pl.BlockSpec((None, s, B_WIDTH), full),
                  pl.BlockSpec((None, s, B_WIDTH), full)],
        out_specs=pl.BlockSpec((None, tq, B_WIDTH), blk),
        out_shape=jax.ShapeDtypeStruct((b, s, B_WIDTH), BF16),
        scratch_shapes=[pltpu.VMEM((nb, B_WIDTH), F32), pltpu.VMEM((s, B_WIDTH), BF16),
                        pltpu.VMEM((s, B_WIDTH), BF16)],
        compiler_params=_cparams("parallel", "arbitrary"),
    )(q, k, v)


def _split3(x):
    hi = x.astype(BF16)
    r1 = x - hi.astype(F32)
    mid = r1.astype(BF16)
    lo = (r1 - mid.astype(F32)).astype(BF16)
    return hi, mid, lo


def _dot_tn(a, b):
    return lax.dot_general(a, b, (((0,), (0,)), ((), ())), preferred_element_type=F32)


def _wkv_chunk(r, ld, k, v, kkr, a, states, c):
    heads = range(len(states))
    hs = lambda x, h: x[:, h * C_HD:(h + 1) * C_HD]
    row = lax.broadcasted_iota(I32, (c, c), 0)
    col = lax.broadcasted_iota(I32, (c, c), 1)
    strict = row > col
    incl = row >= col
    incl16 = jnp.where(incl, 1.0, 0.0).astype(BF16)
    eye = jnp.where(row == col, 1.0, 0.0)
    hi, mid, lo = _split3(ld)
    cum = _dot(incl16, hi) + _dot(incl16, mid) + _dot(incl16, lo)
    g_inc = jnp.exp(cum)
    g_exc = jnp.exp(cum - ld)
    g_inv = jnp.exp(-cum)
    g_end = g_inc[c - 1:c, :]
    rt = r * g_inc
    kt = k * g_inv
    k_end = kt * g_end

    kk = []
    for h in heads:
        x = hs(kkr, h)
        kk.append(x / jnp.maximum(jnp.sqrt(jnp.sum(x * x, axis=-1, keepdims=True)), 1e-12))
    at16 = [(-kk[h] * hs(g_exc, h)).astype(BF16) for h in heads]
    bt = [kk[h] * hs(a, h) * hs(g_inv, h) for h in heads]
    ar16 = [jnp.concatenate([at16[h], hs(rt, h).astype(BF16)], axis=0) for h in heads]
    bk16 = [jnp.concatenate([bt[h], hs(kt, h)], axis=0).astype(BF16) for h in heads]
    gram = [_dot_nt(ar16[h], bk16[h]) for h in heads]
    l_ab = [jnp.where(strict, gram[h][:c, :c], 0.0) for h in heads]
    l_ak16 = [jnp.where(strict, gram[h][:c, c:], 0.0).astype(BF16) for h in heads]
    col2 = lax.broadcasted_iota(I32, (c, 2 * c), 1)
    incl2 = lax.broadcasted_iota(I32, (c, 2 * c), 0) >= jnp.where(col2 >= c, col2 - c, col2)
    l_r16 = [jnp.where(incl2, gram[h][c:, :], 0.0).astype(BF16) for h in heads]
    l_rb16 = [l_r16[h][:, :c] for h in heads]
    inv = [eye + l_ab[h] for h in heads]
    lp16 = [l_ab[h].astype(BF16) for h in heads]
    n = 2
    while n < c:
        lp16 = [_dot(lp16[h], lp16[h]).astype(BF16) for h in heads]
        inv = [inv[h] + _dot(inv[h].astype(BF16), lp16[h]) for h in heads]
        n *= 2
    inv16 = [inv[h].astype(BF16) for h in heads]
    v16 = [hs(v, h).astype(BF16) for h in heads]
    lakv16 = [_dot(l_ak16[h], v16[h]).astype(BF16) for h in heads]
    w16 = [_dot(inv16[h], at16[h]).astype(BF16) for h in heads]
    u0_16 = [_dot(inv16[h], lakv16[h]).astype(BF16) for h in heads]
    r2_16 = [(hs(rt, h) + _dot(l_rb16[h], w16[h])).astype(BF16) for h in heads]
    uv16 = [jnp.concatenate([u0_16[h], v16[h]], axis=0) for h in heads]
    y0 = [_dot(l_r16[h], uv16[h]) for h in heads]
    bend16 = [(bt[h] * hs(g_end, h)).astype(BF16) for h in heads]
    bkend16 = [jnp.concatenate([bend16[h], hs(k_end, h).astype(BF16)], axis=0) for h in heads]
    m2_16 = [_dot_tn(w16[h], bend16[h]).astype(BF16) for h in heads]
    nmat = [_dot_tn(uv16[h], bkend16[h]) for h in heads]
    s16 = [states[h].astype(BF16) for h in heads]
    ys = [_dot_nt(r2_16[h], s16[h]) + y0[h] for h in heads]
    new_states = [states[h] * hs(g_end, h) + _dot(s16[h], m2_16[h]) + nmat[h] for h in heads]
    return ys, new_states


WKV_VECS = ('c_w0', 'c_a0', 'c_k_k', 'c_k_a', 'c_ln_w', 'c_ln_b', 'c_r_k')


def _wkv_kernel(p_ref, prev_ref, mu_ref, vec_ref, w2_ref, a2_ref, g2_ref, s0_ref,
                o_ref, sfin_ref, state_ref, last_ref, *, chunk, valid_len):
    tb = p_ref.shape[0]
    t_blk = pl.program_id(1)
    hs = lambda x, h: x[:, h * C_HD:(h + 1) * C_HD]
    w0, a0, k_k, k_a, ln_w, ln_b, r_k = (vec_ref[i:i + 1, :] for i in range(len(WKV_VECS)))

    @pl.when(t_blk == 0)
    def _():
        state_ref[...] = s0_ref[...]
        last_ref[...] = prev_ref[...]

    def chunk_step(ci, prev_row):
        start = pl.multiple_of(ci * chunk, chunk)
        rows = pl.ds(start, chunk)
        pc = p_ref[rows, :]
        first = lax.broadcasted_iota(I32, pc.shape, 0) == 0
        shifted = jnp.where(first, prev_row, pltpu.roll(pc, 1, 0))
        xs = pc + mu_ref[...] * (shifted - pc)
        r = xs[:, 0:C_WIDTH]
        k = xs[:, C_WIDTH:2 * C_WIDTH]
        v = xs[:, 2 * C_WIDTH:3 * C_WIDTH]
        lora = xs[:, LORA_OFF:C_BLOCK]
        z = w0 + _dot(jnp.tanh(lora).astype(BF16), w2_ref[...])
        ld = -jnp.exp(jnp.minimum(z, 0.0) - jnp.log1p(jnp.exp(-jnp.abs(z))) - 0.5)
        a = 1.0 / (1.0 + jnp.exp(-(a0 + _dot(lora.astype(BF16), a2_ref[...]))))
        g = _dot((1.0 / (1.0 + jnp.exp(-lora))).astype(BF16), g2_ref[...])
        kkr = k * k_k
        k = k * (1.0 + (a - 1.0) * k_a)
        if valid_len is not None:
            live = (t_blk * tb + start + lax.broadcasted_iota(I32, k.shape, 0)) < valid_len
            ld, kkr, k, v = (jnp.where(live, x, 0.0) for x in (ld, kkr, k, v))
        states = [state_ref[h] for h in range(C_HEADS)]
        ys, new_states = _wkv_chunk(r, ld, k, v, kkr, a, states, chunk)
        for h in range(C_HEADS):
            state_ref[h] = new_states[h]
        rkr = r * k * r_k
        outs = []
        for h in range(C_HEADS):
            y = ys[h]
            mean = jnp.mean(y, axis=-1, keepdims=True)
            var = jnp.mean(jnp.square(y - mean), axis=-1, keepdims=True)
            yn = (y - mean) * lax.rsqrt(var + GN_EPS)
            bonus = jnp.sum(hs(rkr, h), axis=-1, keepdims=True) * hs(v, h)
            outs.append(yn * hs(ln_w, h) + hs(ln_b, h) + bonus)
        o_ref[rows, :] = (jnp.concatenate(outs, axis=1) * g).astype(o_ref.dtype)
        return pc[chunk - 1:chunk, :]

    last_ref[...] = lax.fori_loop(0, tb // chunk, chunk_step, last_ref[...])

    @pl.when(t_blk == pl.num_programs(1) - 1)
    def _():
        sfin_ref[...] = state_ref[...]


def wkv7(p, prev, mu, vecs, w2, a2, g2, s0, chunk, tb, valid_len):
    b, s, _ = p.shape
    const = lambda shape: pl.BlockSpec(shape, lambda bi, i: (0,) * len(shape))
    st = pl.BlockSpec((None, C_HEADS, C_HD, C_HD), lambda bi, i: (bi, 0, 0, 0))
    return pl.pallas_call(
        functools.partial(_wkv_kernel, chunk=chunk, valid_len=valid_len if valid_len < s else None),
        grid=(b, s // tb),
        in_specs=[pl.BlockSpec((None, tb, C_BLOCK), lambda bi, i: (bi, i, 0)),
                  pl.BlockSpec((None, 1, C_BLOCK), lambda bi, i: (bi, 0, 0)),
                  const(mu.shape), const(vecs.shape), const(w2.shape), const(a2.shape), const(g2.shape), st],
        out_specs=[pl.BlockSpec((None, tb, C_WIDTH), lambda bi, i: (bi, i, 0)), st],
        out_shape=[jax.ShapeDtypeStruct((b, s, C_WIDTH), BF16),
                   jax.ShapeDtypeStruct((b, C_HEADS, C_HD, C_HD), F32)],
        scratch_shapes=[pltpu.VMEM((C_HEADS, C_HD, C_HD), F32), pltpu.VMEM((1, C_BLOCK), F32)],
        compiler_params=_cparams("parallel", "arbitrary"),
    )(p, prev, mu, vecs, w2, a2, g2, s0)


def _split2(x):
    hi = x.astype(BF16)
    return hi, (x - hi.astype(F32)).astype(BF16)


def _page_spec(rows, width, layer, slot, n_pages):
    last_group = n_pages // SAMPLE_PAGES_PER_STEP - 1

    def index_map(b, j, pt):
        return (layer, pt[b, jnp.minimum(j, last_group) * SAMPLE_PAGES_PER_STEP + slot], 0, 0)
    return pl.BlockSpec((None, None, rows, width), index_map)


def _idx_sample_kernel(pt_ref, qi_ref, w_ref, knew_ref, *rest, t):
    pages, o_ref = rest[:-1], rest[-1]
    j = pl.program_id(1)
    last = pl.num_programs(1) - 1
    q_hi, q_lo = _split2(qi_ref[...])
    w = w_ref[...]

    def scores(kpages):
        splits = [_split2(kp) for kp in kpages]
        s = [_dot(q_hi, k_hi) + _dot(q_hi, k_lo) + _dot(q_lo, k_hi) for k_hi, k_lo in splits]
        s = [jnp.maximum(x, 0.0) * w for x in s]
        return [jnp.sum(x.reshape(t, IDX_HEADS, PAGE_SIZE), axis=1) for x in s]

    @pl.when(j < last)
    def _():
        o_ref[...] = jnp.concatenate(scores([page[...] for page in pages]), axis=1)

    @pl.when(j == last)
    def _():
        o_ref[...] = jnp.zeros(o_ref.shape, F32)
        o_ref[:, 0:PAGE_SIZE] = scores([knew_ref[...]])[0]


def idx_sample(page_table, qi, w, knew, cache_idx, layer):
    db, rows, _ = qi.shape
    t = rows // IDX_HEADS
    n_pages = page_table.shape[1]
    pg = SAMPLE_PAGES_PER_STEP
    steps = n_pages // pg + 1
    per_b = lambda b, j, pt: (b, 0, 0)
    return pl.pallas_call(
        functools.partial(_idx_sample_kernel, t=t),
        grid_spec=pltpu.PrefetchScalarGridSpec(
            num_scalar_prefetch=1,
            grid=(db, steps),
            in_specs=[pl.BlockSpec((None, rows, IDX_HD), per_b),
                      pl.BlockSpec((None, rows, 1), per_b),
                      pl.BlockSpec((None, IDX_HD, PAGE_SIZE), per_b)]
                     + [_page_spec(IDX_HD, PAGE_SIZE, layer, i, n_pages) for i in range(pg)],
            out_specs=pl.BlockSpec((None, t, pg * PAGE_SIZE), lambda b, j, pt: (b, 0, j))),
        out_shape=jax.ShapeDtypeStruct((db, t, steps * pg * PAGE_SIZE), F32),
        compiler_params=_cparams("parallel", "arbitrary"),
    )(page_table, qi, w, knew, *([cache_idx] * pg))


def _select_sample_kernel(isc_ref, o_ref, *, topk, t, past_len):
    rows, n = isc_ref.shape
    pos = lax.broadcasted_iota(I32, (rows, n), 1)
    r = pl.program_id(0) * rows + lax.broadcasted_iota(I32, (rows, n), 0)
    causal = pos <= past_len + (r & (t - 1))
    key = _sortable_key(jnp.where(causal, isc_ref[...], NEG_INF))
    mask = _topk_mask(key, pos, topk, n.bit_length()) & causal
    o_ref[...] = jnp.where(mask, 1.0, 0.0)


def select_sample(isc, topk, t, past_len):
    rows, n = isc.shape
    tr = 16
    assert t & (t - 1) == 0 and rows % tr == 0
    return pl.pallas_call(
        functools.partial(_select_sample_kernel, topk=topk, t=t, past_len=past_len),
        grid=(rows // tr,),
        in_specs=[pl.BlockSpec((tr, n), lambda i: (i, 0))],
        out_specs=pl.BlockSpec((tr, n), lambda i: (i, 0)),
        out_shape=jax.ShapeDtypeStruct((rows, n), F32),
        compiler_params=_cparams("parallel"),
    )(isc)


def _own_head(shape, heads):
    return ((lax.broadcasted_iota(I32, shape, 1) & (heads - 1))
            == (lax.broadcasted_iota(I32, shape, 0) & (heads - 1)))


def _dsa_sample_kernel(pt_ref, q_ref, mask_ref, knew_ref, vnew_ref, *rest, heads):
    pg = SAMPLE_PAGES_PER_STEP
    kpages, vpages = rest[:pg], rest[pg:2 * pg]
    o_ref, m_ref, l_ref, acc_ref = rest[2 * pg:]
    j = pl.program_id(1)
    last = pl.num_programs(1) - 1
    q = q_ref[...]
    page_rows = PAGE_SIZE * heads
    shift = heads.bit_length() - 1
    spread = jnp.where(
        jnp.right_shift(lax.broadcasted_iota(I32, (PAGE_SIZE, page_rows), 1), shift)
        == lax.broadcasted_iota(I32, (PAGE_SIZE, page_rows), 0), 1.0, 0.0).astype(BF16)

    @pl.when(j == 0)
    def _():
        m_ref[...] = jnp.full(m_ref.shape, NEG_INF, F32)
        l_ref[...] = jnp.zeros(l_ref.shape, F32)
        acc_ref[...] = jnp.zeros(acc_ref.shape, F32)

    def update(ks, vs, mask):
        s = jnp.concatenate([_dot_nt(q, kp.astype(BF16)) for kp in ks], axis=1) * (A_HD ** -0.5)
        sel = jnp.concatenate([_dot(mask[:, i * PAGE_SIZE:(i + 1) * PAGE_SIZE].astype(BF16), spread)
                               for i in range(len(ks))], axis=1)
        s = jnp.where((sel > 0.5) & _own_head(s.shape, heads), s, NEG_INF)
        m_old = m_ref[...]
        m_new = jnp.maximum(m_old, jnp.max(s, axis=-1, keepdims=True))
        m_safe = jnp.where(m_new == NEG_INF, 0.0, m_new)
        p = jnp.exp(s - m_safe)
        alpha = jnp.exp(m_old - m_safe)
        l_ref[...] = alpha * l_ref[...] + jnp.sum(p, axis=-1, keepdims=True)
        acc = alpha * acc_ref[...]
        for i, vp in enumerate(vs):
            acc += _dot(p[:, i * page_rows:(i + 1) * page_rows].astype(BF16), vp.astype(BF16))
        acc_ref[...] = acc
        m_ref[...] = m_new

    @pl.when(j < last)
    def _():
        update([kp[...] for kp in kpages], [vp[...] for vp in vpages], mask_ref[...])

    @pl.when(j == last)
    def _():
        update([knew_ref[...]], [vnew_ref[...]], mask_ref[:, 0:PAGE_SIZE])
        o_ref[...] = acc_ref[...] / l_ref[...]


def dsa_sample(page_table, q, mask, knew, vnew, cache_k, cache_v, layer, heads):
    db, rows, hd = q.shape
    n_pages = page_table.shape[1]
    pg = SAMPLE_PAGES_PER_STEP
    steps = n_pages // pg + 1
    page_rows = PAGE_SIZE * heads
    per_b = lambda b, j, pt: (b, 0, 0)
    return pl.pallas_call(
        functools.partial(_dsa_sample_kernel, heads=heads),
        grid_spec=pltpu.PrefetchScalarGridSpec(
            num_scalar_prefetch=1,
            grid=(db, steps),
            in_specs=[pl.BlockSpec((None, rows, hd), per_b),
                      pl.BlockSpec((None, rows, pg * PAGE_SIZE), lambda b, j, pt: (b, 0, j)),
                      pl.BlockSpec((None, page_rows, hd), per_b),
                      pl.BlockSpec((None, page_rows, hd), per_b)]
                     + [_page_spec(page_rows, hd, layer, i, n_pages) for i in range(pg)]
                     + [_page_spec(page_rows, hd, layer, i, n_pages) for i in range(pg)],
            out_specs=pl.BlockSpec((None, rows, hd), per_b),
            scratch_shapes=[pltpu.VMEM((rows, 1), F32), pltpu.VMEM((rows, 1), F32),
                            pltpu.VMEM((rows, hd), F32)]),
        out_shape=jax.ShapeDtypeStruct((db, rows, hd), F32),
        compiler_params=_cparams("parallel", "arbitrary"),
    )(page_table, q, mask, knew, vnew, *([cache_k] * pg), *([cache_v] * pg))


def _moba_sample_kernel(pt_ref, q_ref, knew_ref, vnew_ref, *rest, heads, ksel):
    pg = SAMPLE_PAGES_PER_STEP
    ppb = MOBA_BLOCK // PAGE_SIZE
    kpages, vpages = rest[:pg], rest[pg:2 * pg]
    o_ref, g_ref, m_ref, l_ref, ob_ref = rest[2 * pg:]
    j = pl.program_id(1)
    last = pl.num_programs(1) - 1
    q = q_ref[...]
    q16 = q.astype(BF16)
    rows, hd = q.shape
    scale = B_HD ** -0.5
    sublanes = 8
    assert 2 * heads == sublanes and hd == LANES and rows % sublanes == 0
    wide = lambda x: jnp.broadcast_to(x, (rows, LANES))

    def partial_softmax(s, v16):
        m = jnp.max(s, axis=-1, keepdims=True)
        p = jnp.exp(s - m)
        return m, jnp.sum(p, axis=-1, keepdims=True), _dot(p.astype(BF16), v16)

    @pl.when(j < last)
    def _():
        blocks = range(pg // ppb)
        kblk = [jnp.concatenate([kpages[b * ppb + i][...] for i in range(ppb)], axis=0) for b in blocks]
        v16 = [jnp.concatenate([vpages[b * ppb + i][...] for i in range(ppb)], axis=0).astype(BF16)
               for b in blocks]
        s = [_dot_nt(q16, kblk[b].astype(BF16)) * scale for b in blocks]
        own = _own_head(s[0].shape, heads)
        s = [jnp.where(own, s[b], NEG_INF) for b in blocks]
        m = [jnp.max(s[b], axis=-1, keepdims=True) for b in blocks]
        p = [jnp.exp(s[b] - m[b]) for b in blocks]
        l = [jnp.sum(p[b], axis=-1, keepdims=True) for b in blocks]
        o = [_dot(p[b].astype(BF16), v16[b]) for b in blocks]
        fold = [jnp.sum(kblk[b].reshape(kblk[b].shape[0] // sublanes, sublanes, hd), axis=0) for b in blocks]
        kmean = [(fold[b] + pltpu.roll(fold[b], heads, 0)) * (1.0 / MOBA_BLOCK) for b in blocks]
        gate = [jnp.sum(q * jnp.concatenate([kmean[b]] * (rows // sublanes), axis=0), axis=-1, keepdims=True)
                for b in blocks]
        for b in blocks:
            n = j * (pg // ppb) + b
            g_ref[n] = wide(gate[b])
            m_ref[n] = wide(m[b])
            l_ref[n] = wide(l[b])
            ob_ref[n] = o[b]

    @pl.when(j == last)
    def _():
        shift = heads.bit_length() - 1
        s = _dot_nt(q16, knew_ref[...].astype(BF16)) * scale
        causal = (jnp.right_shift(lax.broadcasted_iota(I32, s.shape, 1), shift)
                  <= jnp.right_shift(lax.broadcasted_iota(I32, s.shape, 0), shift))
        s = jnp.where(causal & _own_head(s.shape, heads), s, NEG_INF)
        m_own, l_own, o_own = partial_softmax(s, vnew_ref[...].astype(BF16))
        m_own, l_own = wide(m_own), wide(l_own)
        gates = g_ref[...]
        nidx = lax.broadcasted_iota(I32, gates.shape, 0)
        chosen = jnp.zeros(gates.shape, F32)
        for _ in range(ksel):
            best = jnp.max(gates, axis=0, keepdims=True)
            first = jnp.min(jnp.where(gates == best, nidx, gates.shape[0]), axis=0, keepdims=True)
            pick = nidx == first
            chosen = jnp.where(pick, 1.0, chosen)
            gates = jnp.where(pick, NEG_INF, gates)
        m_blk = m_ref[...]
        m_tot = jnp.maximum(m_own, jnp.max(jnp.where(chosen > 0.5, m_blk, NEG_INF), axis=0))
        wgt = jnp.where(chosen > 0.5, jnp.exp(m_blk - m_tot[None]), 0.0)
        w_own = jnp.exp(m_own - m_tot)
        l_tot = w_own * l_own + jnp.sum(wgt * l_ref[...], axis=0)
        acc = w_own * o_own
        for n in range(gates.shape[0]):
            acc += wgt[n] * ob_ref[n]
        o_ref[...] = acc / l_tot


def moba_sample(page_table, q, knew, vnew, cache_k, cache_v, layer, heads):
    db, rows, hd = q.shape
    n_pages = page_table.shape[1]
    pg = SAMPLE_PAGES_PER_STEP
    steps = n_pages // pg + 1
    nbp = n_pages * PAGE_SIZE // MOBA_BLOCK
    page_rows = PAGE_SIZE * heads
    per_b = lambda b, j, pt: (b, 0, 0)
    return pl.pallas_call(
        functools.partial(_moba_sample_kernel, heads=heads, ksel=max(1, min(MOBA_TOPK, nbp))),
        grid_spec=pltpu.PrefetchScalarGridSpec(
            num_scalar_prefetch=1,
            grid=(db, steps),
            in_specs=[pl.BlockSpec((None, rows, hd), per_b),
                      pl.BlockSpec((None, page_rows, hd), per_b),
                      pl.BlockSpec((None, page_rows, hd), per_b)]
                     + [_page_spec(page_rows, hd, layer, i, n_pages) for i in range(pg)]
                     + [_page_spec(page_rows, hd, layer, i, n_pages) for i in range(pg)],
            out_specs=pl.BlockSpec((None, rows, hd), per_b),
            scratch_shapes=[pltpu.VMEM((nbp, rows, LANES), F32), pltpu.VMEM((nbp, rows, LANES), F32),
                            pltpu.VMEM((nbp, rows, LANES), F32), pltpu.VMEM((nbp, rows, hd), F32)]),
        out_shape=jax.ShapeDtypeStruct((db, rows, hd), F32),
        compiler_params=_cparams("parallel", "arbitrary"),
    )(page_table, q, knew, vnew, *([cache_k] * pg), *([cache_v] * pg))


def _new_page(x, rows):
    return jnp.pad(x, ((0, 0), (0, rows - x.shape[1]), (0, 0)))


def _sample_attention(qa, ka, va, qi, kiwi, qb, kb, vb, caches, page_table, layer, db, t):
    cache_a_k, cache_a_v, cache_a_idx, cache_b_k, cache_b_v = caches
    past_len = page_table.shape[1] * PAGE_SIZE
    topk = min(A_TOPK_MAX, (past_len + t) // 4)
    ki = kiwi[:, :IDX_HD].reshape(db, t, IDX_HD)
    wi = kiwi[:, IDX_HD:IDX_HD + IDX_HEADS]
    knew = _new_page(ki, PAGE_SIZE).transpose(0, 2, 1)
    isc = idx_sample(page_table, qi.reshape(db, t * IDX_HEADS, IDX_HD), wi.reshape(db, t * IDX_HEADS, 1),
                     knew, cache_a_idx, layer)
    n = isc.shape[-1]
    mask = select_sample(isc.reshape(db * t, n), topk, t, past_len).reshape(db, t, n)
    rows_a = lambda z: z.reshape(db, t * A_HEADS, A_HD)
    rows_b = lambda z: z.reshape(db, t * B_HEADS, B_HD)
    oa = dsa_sample(page_table, rows_a(qa).astype(BF16), jnp.repeat(mask, A_HEADS, axis=1),
                    _new_page(rows_a(ka), PAGE_SIZE * A_HEADS), _new_page(rows_a(va), PAGE_SIZE * A_HEADS),
                    cache_a_k, cache_a_v, layer, A_HEADS)
    ob = moba_sample(page_table, rows_b(qb), _new_page(rows_b(kb), PAGE_SIZE * B_HEADS),
                     _new_page(rows_b(vb), PAGE_SIZE * B_HEADS), cache_b_k, cache_b_v, layer, B_HEADS)
    return oa.reshape(db * t, A_WIDTH), ob.reshape(db * t, B_WIDTH)


def _pad_cols(x, width):
    return jnp.pad(x, [(0, 0)] * (x.ndim - 1) + [(0, width - x.shape[-1])])


def _layer_params(l, norm1_g, w_in, a_q_norm, a_k_norm, idx_k_norm, b_q_norm, b_k_norm, c_mu, c_w0, c_w2, c_a0,
                  c_a2, c_g2, c_k_k, c_k_a, c_r_k, c_ln_w, c_ln_b, w_o, norm2_g, w_ff1, w_ff2):
    w = w_in[l]
    a_cols, b_cols, c_cols = w[:, :A_COLS], w[:, A_COLS:A_COLS + B_COLS], w[:, A_COLS + B_COLS:]
    w_pack = jnp.concatenate([_pad_cols(c_cols, A_OFF), _pad_cols(a_cols, B_OFF - A_OFF), b_cols],
                             axis=1).astype(BF16)
    lora_rows = lambda x, off: jnp.pad(x, ((off, LORA_BLOCK - off - x.shape[0]), (0, 0))).astype(BF16)
    vec = dict(c_w0=c_w0[l], c_a0=c_a0[l], c_k_k=c_k_k[l], c_k_a=c_k_a[l], c_ln_w=c_ln_w[l], c_ln_b=c_ln_b[l],
               c_r_k=c_r_k[l].reshape(-1))
    vecs = jnp.stack([vec[name] for name in WKV_VECS] + [jnp.zeros((C_WIDTH,), F32)])
    gains = jnp.stack([a_q_norm[l], a_k_norm[l], b_q_norm[l], b_k_norm[l], _pad_cols(idx_k_norm[l], LANES)]
                      + [jnp.zeros((LANES,), F32)] * 3)
    return dict(norm1_g=norm1_g[l], w_in=w_pack, gains=gains, mu=_pad_cols(c_mu[l], C_BLOCK)[None], vecs=vecs,
                w2=lora_rows(c_w2[l], 0), a2=lora_rows(c_a2[l], W_LORA), g2=lora_rows(c_g2[l], W_LORA + A_LORA),
                w_o=w_o[l].astype(BF16), norm2_g=norm2_g[l], w_ff1=w_ff1[l].astype(BF16),
                w_ff2=w_ff2[l].astype(BF16))


def _layer(x, tabs, prev_shift, wkv0, lw, sample_ctx):
    b, s, d = x.shape
    m = b * s
    p = norm_matmul(x.reshape(m, d), lw['norm1_g'], lw['w_in'], min(m, 1024), PROJ_TILE_N)
    qa, ka, va, qi, kiwi, qb, kb, vb = prep(p, tabs[0], tabs[1], lw['gains'], min(m, 256))
    p3 = p.reshape(b, s, P_PACK)
    if sample_ctx is None:
        seq = lambda z: z.reshape(b, s, z.shape[-1])
        oa = dsa_prompt(seq(qa), seq(ka), seq(va), seq(qi), seq(kiwi)).reshape(m, A_WIDTH)
        ob = moba_prompt(seq(qb), seq(kb), seq(vb)).reshape(m, B_WIDTH)
        p_scan = p3
    else:
        caches, page_table, layer = sample_ctx
        oa, ob = _sample_attention(qa, ka, va, qi, kiwi, qb, kb, vb, caches, page_table, layer, b, s)
        oa, ob = oa.astype(BF16), ob.astype(BF16)
        p_scan = jnp.pad(p3, ((0, 0), (0, -s % 16), (0, 0)))
    sp = p_scan.shape[1]
    chunk = min(WKV_CHUNK, sp)
    oc, wkv_fin = wkv7(p_scan, _pad_cols(prev_shift, C_BLOCK)[:, None, :], lw['mu'], lw['vecs'], lw['w2'],
                       lw['a2'], lw['g2'], wkv0, chunk, min(sp, 4 * chunk), s)
    tm2 = min(m, 512)
    x2 = out_proj(x.reshape(m, d), oa, ob, oc[:, :s].reshape(m, C_WIDTH), lw['w_o'], tm2)
    x3 = mlp(x2, lw['norm2_g'], lw['w_ff1'], lw['w_ff2'], tm2, min(MLP_TILE_F, lw['w_ff1'].shape[1])).reshape(b, s, d)
    rows = (ka.reshape(b, s, A_HEADS, A_HD), va.reshape(b, s, A_HEADS, A_HD),
            kiwi[:, :IDX_HD].reshape(b, s, IDX_HD), kb.reshape(b, s, B_HEADS, B_HD),
            vb.reshape(b, s, B_HEADS, B_HD), wkv_fin, p3[:, s - 1, :C_COLS])
    return x3, rows


def kernel(x_prompt, x_sample, cache_a_k, cache_a_v, cache_a_idx, cache_b_k, cache_b_v, state_wkv, state_shift, page_table, norm1_g, w_in, a_q_norm, a_k_norm, idx_k_norm, b_q_norm, b_k_norm, c_mu, c_w0, c_w2, c_a0, c_a2, c_g2, c_k_k, c_k_a, c_r_k, c_ln_w, c_ln_b, w_o, norm2_g, w_ff1, w_ff2):
    depth = w_in.shape[0]
    bp, sp = x_prompt.shape[:2]
    db, t = x_sample.shape[:2]
    pos_p = jnp.arange(sp)
    pos_s = jnp.tile(page_table.shape[1] * PAGE_SIZE + jnp.arange(t), db)
    tabs_p = (_rope_tables(pos_p, A_HD, 1), _rope_tables(pos_p, IDX_HD, LANES // IDX_HD))
    tabs_s = (_rope_tables(pos_s, A_HD, 1), _rope_tables(pos_s, IDX_HD, LANES // IDX_HD))
    shift0 = jnp.zeros((bp, C_COLS), x_prompt.dtype)
    wkv0 = jnp.zeros((bp, C_HEADS, C_HD, C_HD), F32)
    xp, xs = x_prompt, x_sample
    by_row = lambda c: c.reshape(c.shape[0], c.shape[1], -1, c.shape[-1])
    caches = (by_row(cache_a_k), by_row(cache_a_v), cache_a_idx.transpose(0, 1, 3, 2),
              by_row(cache_b_k), by_row(cache_b_v))
    p_rows, s_rows = [], []
    for l in range(depth):
        lw = _layer_params(l, norm1_g, w_in, a_q_norm, a_k_norm, idx_k_norm, b_q_norm, b_k_norm, c_mu, c_w0, c_w2,
                           c_a0, c_a2, c_g2, c_k_k, c_k_a, c_r_k, c_ln_w, c_ln_b, w_o, norm2_g, w_ff1, w_ff2)
        xp, rows = _layer(xp, tabs_p, shift0, wkv0, lw, None)
        p_rows.append(rows)
        xs, rows = _layer(xs, tabs_s, state_shift[l], state_wkv[l], lw, (caches, page_table, l))
        s_rows.append(rows)
    p_out = [jnp.stack(z) for z in zip(*p_rows)]
    s_out = [jnp.stack(z) for z in zip(*s_rows)]
    return (xp, xs, *p_out, *s_out)
```

```python
import functools

import jax
import jax.numpy as jnp
from jax import lax
from jax.experimental import pallas as pl
from jax.experimental.pallas import tpu as pltpu

F32 = jnp.float32
BF16 = jnp.bfloat16
I32 = jnp.int32

VMEM_LIMIT_BYTES = 56 * 1024 * 1024
LANES = 128

PAGE_SIZE = 128
A_HEADS = 4
A_HD = 128
A_WIDTH = A_HEADS * A_HD
IDX_HEADS = 16
IDX_HD = 64
A_TOPK_MAX = 256
A_Q_BLOCK = 128
B_HEADS = 4
B_HD = 128
B_WIDTH = B_HEADS * B_HD
MOBA_BLOCK = 256
MOBA_TOPK = 3
C_HD = 64
C_HEADS = 16
C_WIDTH = C_HEADS * C_HD
W_LORA = 64
A_LORA = 64
G_LORA = 160
A_COLS = 3 * A_WIDTH + IDX_HEADS * IDX_HD + IDX_HD + IDX_HEADS
B_COLS = 3 * B_WIDTH
C_COLS = 3 * C_WIDTH + W_LORA + A_LORA + G_LORA
ROPE_THETA = 500000.0
NORM_EPS = 1e-6
GN_EPS = 64e-5
IDX_W_SCALE = (IDX_HEADS * IDX_HD) ** -0.5
WKV_CHUNK = 64
SAMPLE_PAGES_PER_STEP = 32
IDX_PAGES_PER_STEP = 64
DSA_KEY_BUCKET = 256
PROJ_TILE_N = 1536
MLP_TILE_F = 1024

C_BLOCK = C_COLS + (-C_COLS % LANES)
A_OFF = C_BLOCK
B_OFF = A_OFF + A_COLS + (-A_COLS % LANES)
P_PACK = B_OFF + B_COLS
assert P_PACK % PROJ_TILE_N == 0
LORA_OFF = 3 * C_WIDTH
LORA_BLOCK = C_BLOCK - LORA_OFF
A_QI_OFF = A_OFF + 3 * A_WIDTH
A_KI_OFF = A_QI_OFF + IDX_HEADS * IDX_HD
assert A_KI_OFF + LANES == B_OFF and IDX_HD + IDX_HEADS <= LANES

NEG_INF = float("-inf")
INT_MIN = -2 ** 31


def _cparams(*sem):
    return pltpu.CompilerParams(dimension_semantics=sem, vmem_limit_bytes=VMEM_LIMIT_BYTES)


def _dot_nt(a, b):
    return lax.dot_general(a, b, (((1,), (1,)), ((), ())), preferred_element_type=F32)


def _dot(a, b):
    return jnp.dot(a, b, preferred_element_type=F32)


def _norm_matmul_kernel(x_ref, g_ref, w_ref, o_ref, xn_ref):
    @pl.when(pl.program_id(1) == 0)
    def _():
        x = x_ref[...]
        y = x * lax.rsqrt(jnp.mean(x * x, axis=-1, keepdims=True) + NORM_EPS)
        xn_ref[...] = (y * g_ref[...]).astype(BF16)

    o_ref[...] = _dot(xn_ref[...], w_ref[...])


def norm_matmul(x, g, w, tm, tn):
    m, d = x.shape
    p = w.shape[1]
    return pl.pallas_call(
        _norm_matmul_kernel,
        grid=(m // tm, p // tn),
        in_specs=[pl.BlockSpec((tm, d), lambda i, j: (i, 0)),
                  pl.BlockSpec((1, d), lambda i, j: (0, 0)),
                  pl.BlockSpec((d, tn), lambda i, j: (0, j))],
        out_specs=pl.BlockSpec((tm, tn), lambda i, j: (i, j)),
        out_shape=jax.ShapeDtypeStruct((m, p), F32),
        scratch_shapes=[pltpu.VMEM((tm, d), BF16)],
        compiler_params=_cparams("parallel", "arbitrary"),
    )(x, g.reshape(1, d), w)


def _out_proj_kernel(x_ref, oa_ref, ob_ref, oc_ref, w_ref, o_ref):
    wa = oa_ref.shape[1]
    wb = ob_ref.shape[1]
    acc = _dot(oa_ref[...], w_ref[0:wa, :])
    acc += _dot(ob_ref[...], w_ref[wa:wa + wb, :])
    acc += _dot(oc_ref[...], w_ref[wa + wb:, :])
    o_ref[...] = x_ref[...] + acc


def out_proj(x, oa, ob, oc, w, tm):
    m, d = x.shape
    row = lambda i: (i, 0)
    return pl.pallas_call(
        _out_proj_kernel,
        grid=(m // tm,),
        in_specs=[pl.BlockSpec((tm, d), row),
                  pl.BlockSpec((tm, oa.shape[1]), row),
                  pl.BlockSpec((tm, ob.shape[1]), row),
                  pl.BlockSpec((tm, oc.shape[1]), row),
                  pl.BlockSpec(w.shape, lambda i: (0, 0))],
        out_specs=pl.BlockSpec((tm, d), row),
        out_shape=jax.ShapeDtypeStruct((m, d), F32),
        compiler_params=_cparams("parallel"),
    )(x, oa, ob, oc, w)


def _mlp_kernel(x_ref, g_ref, w1_ref, w2_ref, o_ref, xn_ref):
    @pl.when(pl.program_id(1) == 0)
    def _():
        x = x_ref[...]
        y = x * lax.rsqrt(jnp.mean(x * x, axis=-1, keepdims=True) + NORM_EPS)
        xn_ref[...] = (y * g_ref[...]).astype(BF16)
        o_ref[...] = x

    h = jnp.maximum(_dot(xn_ref[...], w1_ref[...]), 0.0)
    o_ref[...] += _dot((h * h).astype(BF16), w2_ref[...])


def mlp(x, g, w1, w2, tm, tf):
    m, d = x.shape
    f = w1.shape[1]
    return pl.pallas_call(
        _mlp_kernel,
        grid=(m // tm, f // tf),
        in_specs=[pl.BlockSpec((tm, d), lambda i, j: (i, 0)),
                  pl.BlockSpec((1, d), lambda i, j: (0, 0)),
                  pl.BlockSpec((d, tf), lambda i, j: (0, j)),
                  pl.BlockSpec((tf, d), lambda i, j: (j, 0))],
        out_specs=pl.BlockSpec((tm, d), lambda i, j: (i, 0)),
        out_shape=jax.ShapeDtypeStruct((m, d), F32),
        scratch_shapes=[pltpu.VMEM((tm, d), BF16)],
        compiler_params=_cparams("parallel", "arbitrary"),
    )(x, g.reshape(1, d), w1, w2)


def _rope_tables(pos, hd, reps):
    rot = hd // 4
    half = rot // 2
    inv = ROPE_THETA ** (-2.0 * jnp.arange(half, dtype=F32) / rot)
    ang = pos.astype(F32)[:, None] * inv[None, :]
    cos, sin = jnp.cos(ang), jnp.sin(ang)
    zeros = jnp.zeros_like(sin)
    rest = jnp.zeros((pos.shape[0], hd - rot), F32)
    tabs = [jnp.concatenate([cos, cos, rest + 1.0], axis=1),
            jnp.concatenate([zeros, sin, rest], axis=1),
            jnp.concatenate([-sin, zeros, rest], axis=1)]
    return jnp.stack([jnp.tile(z, (1, reps)) for z in tabs])


def _rope(y, tab_ref, half):
    width = y.shape[1]
    return (y * tab_ref[0] + pltpu.roll(y, half, 1) * tab_ref[1]
            + pltpu.roll(y, width - half, 1) * tab_ref[2])


def _prep_kernel(p_ref, ta_ref, ti_ref, g_ref, qa_ref, ka_ref, va_ref, qi_ref, kiwi_ref, qb_ref, kb_ref, vb_ref,
                 ka_rows_ref, va_rows_ref, kb_rows_ref, vb_rows_ref):
    tm = p_ref.shape[0]

    def normed(x, g):
        return x * lax.rsqrt(jnp.mean(x * x, axis=-1, keepdims=True) + NORM_EPS) * g

    for base, q_ref, k_ref, v_ref, k_rows_ref, v_rows_ref, heads, hd, gq, gk in (
            (A_OFF, qa_ref, ka_ref, va_ref, ka_rows_ref, va_rows_ref, A_HEADS, A_HD, 0, 1),
            (B_OFF, qb_ref, kb_ref, vb_ref, kb_rows_ref, vb_rows_ref, B_HEADS, B_HD, 2, 3)):
        width = heads * hd
        for h in range(heads):
            sl = slice(h * hd, (h + 1) * hd)
            q_ref[:, sl] = _rope(normed(p_ref[:, base + h * hd:base + (h + 1) * hd], g_ref[gq:gq + 1, :]),
                                 ta_ref, hd // 8)
            k = _rope(normed(p_ref[:, base + width + h * hd:base + width + (h + 1) * hd],
                             g_ref[gk:gk + 1, :]), ta_ref, hd // 8)
            v = p_ref[:, base + 2 * width + h * hd:base + 2 * width + (h + 1) * hd]
            k_ref[:, sl] = k
            v_ref[:, sl] = v
            k_rows_ref[pl.ds(h, tm, stride=heads), :] = k
            v_rows_ref[pl.ds(h, tm, stride=heads), :] = v
    for j in range(IDX_HEADS * IDX_HD // LANES):
        sl = slice(j * LANES, (j + 1) * LANES)
        qi_ref[:, sl] = _rope(p_ref[:, A_QI_OFF + j * LANES:A_QI_OFF + (j + 1) * LANES], ti_ref, IDX_HD // 8)
    x = p_ref[:, A_KI_OFF:A_KI_OFF + LANES]
    is_key = lax.broadcasted_iota(I32, x.shape, 1) < IDX_HD
    ms = jnp.sum(jnp.where(is_key, x * x, 0.0), axis=-1, keepdims=True) * (1.0 / IDX_HD)
    ki = _rope(x * lax.rsqrt(ms + NORM_EPS) * g_ref[4:5, :], ti_ref, IDX_HD // 8)
    kiwi_ref[...] = jnp.where(is_key, ki, x * IDX_W_SCALE)


def prep(p, tab_a, tab_i, gains, tm):
    m = p.shape[0]
    period = tab_a.shape[1] // tm
    row = lambda i: (i, 0)
    tab = pl.BlockSpec((3, tm, LANES), lambda i: (0, i % period, 0))
    widths = (A_WIDTH, A_WIDTH, A_WIDTH, IDX_HEADS * IDX_HD, LANES, B_WIDTH, B_WIDTH, B_WIDTH)
    by_head = ((A_HEADS, A_HD), (A_HEADS, A_HD), (B_HEADS, B_HD), (B_HEADS, B_HD))
    return pl.pallas_call(
        _prep_kernel,
        grid=(m // tm,),
        in_specs=[pl.BlockSpec((tm, P_PACK), row), tab, tab, pl.BlockSpec(gains.shape, lambda i: (0, 0))],
        out_specs=[pl.BlockSpec((tm, w), row) for w in widths]
                  + [pl.BlockSpec((tm * h, hd), row) for h, hd in by_head],
        out_shape=[jax.ShapeDtypeStruct((m, w), F32) for w in widths]
                  + [jax.ShapeDtypeStruct((m * h, hd), F32) for h, hd in by_head],
        compiler_params=_cparams("parallel"),
    )(p, tab_a, tab_i, gains)


def _sortable_key(x):
    bits = pltpu.bitcast(x, I32)
    return jnp.where(bits >= 0, bits, bits ^ jnp.int32(0x7FFFFFFF))


def _topk_mask(key, pos, topk, n_pos_bits):
    rows = key.shape[0]

    def count(m):
        return jnp.sum(jnp.where(m, 1.0, 0.0), axis=-1, keepdims=True).astype(I32)

    groups = 2 if rows % 16 == 0 else 1
    step = rows // groups
    parts = [key[g * step:(g + 1) * step] for g in range(groups)]

    def value_step(it, tus):
        bit = jnp.left_shift(jnp.int32(1), 31 - it)
        out = []
        for part, tu in zip(parts, tus):
            cand_u = tu | bit
            cnt = count(part >= (cand_u ^ jnp.int32(INT_MIN)))
            out.append(jnp.where(cnt >= topk, cand_u, tu))
        return tuple(out)

    tus = lax.fori_loop(0, 32, value_step, tuple(jnp.zeros((step, 1), I32) for _ in range(groups)))
    thr = jnp.concatenate(tus, axis=0) ^ jnp.int32(INT_MIN)
    gt = key > thr
    eq = key == thr
    need = topk - count(gt)
    excess = count(eq) - need

    def tie_search(_):
        def pos_step(it, x):
            cand = x | jnp.left_shift(jnp.int32(1), n_pos_bits - 1 - it)
            cnt = count(eq & (pos < cand))
            return jnp.where(cnt < need, cand, x)
        return lax.fori_loop(0, n_pos_bits, pos_step, jnp.zeros((rows, 1), I32))

    last = lax.cond(jnp.max(excess) > 0, tie_search,
                    lambda _: jnp.full((rows, 1), 2 ** n_pos_bits, I32), 0)
    return gt | (eq & (pos <= last))


def _masked_attention(q, k, v, mask, scale):
    s = _dot_nt(q, k) * scale
    s = jnp.where(mask, s, NEG_INF)
    m = jnp.max(s, axis=-1, keepdims=True)
    p = jnp.exp(s - m)
    l = jnp.sum(p, axis=-1, keepdims=True)
    return _dot(p.astype(BF16), v) / l


def _dsa_prompt_kernel(q_ref, k_ref, v_ref, qi_ref, kiwi_ref, kiwi_blk_ref, o_ref, k16_ref, v16_ref, ki16_ref,
                       *, topk, bucket):
    tq = q_ref.shape[0]
    s_len = k_ref.shape[0]
    i = pl.program_id(1)

    @pl.when(i == 0)
    def _():
        k16_ref[...] = k_ref[...].astype(BF16)
        v16_ref[...] = v_ref[...].astype(BF16)
        ki16_ref[...] = kiwi_ref[:, 0:IDX_HD].astype(BF16)

    wi = kiwi_blk_ref[:, IDX_HD:IDX_HD + IDX_HEADS]
    qi16 = qi_ref[...].astype(BF16)
    q16 = q_ref[...].astype(BF16)

    def attend(n_keys):
        qpos = i * tq + lax.broadcasted_iota(I32, (tq, n_keys), 0)
        kpos = lax.broadcasted_iota(I32, (tq, n_keys), 1)
        causal = kpos <= qpos
        ki = ki16_ref[0:n_keys, :]
        isc = jnp.zeros((tq, n_keys), F32)
        for h in range(IDX_HEADS):
            s = _dot_nt(qi16[:, h * IDX_HD:(h + 1) * IDX_HD], ki)
            isc += jnp.maximum(s, 0.0) * wi[:, h:h + 1]
        key = _sortable_key(jnp.where(causal, isc, NEG_INF))
        mask = _topk_mask(key, kpos, topk, (n_keys - 1).bit_length()) & causal
        for h in range(A_HEADS):
            sl = slice(h * A_HD, (h + 1) * A_HD)
            o = _masked_attention(q16[:, sl], k16_ref[0:n_keys, sl], v16_ref[0:n_keys, sl], mask, A_HD ** -0.5)
            o_ref[:, sl] = o.astype(o_ref.dtype)

    for bkt in range(s_len // bucket):
        pl.when(i // (bucket // tq) == bkt)(functools.partial(attend, (bkt + 1) * bucket))


def dsa_prompt(q, k, v, qi, kiwi):
    b, s, _ = q.shape
    tq = A_Q_BLOCK
    topk = min(A_TOPK_MAX, s // 4)
    bucket = min(s, DSA_KEY_BUCKET)
    assert s % bucket == 0 and bucket % tq == 0 and bucket >= topk
    blk = lambda bi, i: (bi, i, 0)
    full = lambda bi, i: (bi, 0, 0)
    return pl.pallas_call(
        functools.partial(_dsa_prompt_kernel, topk=topk, bucket=bucket),
        grid=(b, s // tq),
        in_specs=[pl.BlockSpec((None, tq, A_WIDTH), blk),
                  pl.BlockSpec((None, s, A_WIDTH), full),
                  pl.BlockSpec((None, s, A_WIDTH), full),
                  pl.BlockSpec((None, tq, IDX_HEADS * IDX_HD), blk),
                  pl.BlockSpec((None, s, LANES), full),
                  pl.BlockSpec((None, tq, LANES), blk)],
        out_specs=pl.BlockSpec((None, tq, A_WIDTH), blk),
        out_shape=jax.ShapeDtypeStruct((b, s, A_WIDTH), BF16),
        scratch_shapes=[pltpu.VMEM((s, A_WIDTH), BF16), pltpu.VMEM((s, A_WIDTH), BF16),
                        pltpu.VMEM((s, IDX_HD), BF16)],
        compiler_params=_cparams("parallel", "arbitrary"),
    )(q, k, v, qi, kiwi, kiwi)


def _moba_prompt_kernel(q_ref, k_ref, v_ref, o_ref, kmean_ref, k16_ref, v16_ref, *, ksel):
    tq = q_ref.shape[0]
    s_len, width = k_ref.shape
    nb = s_len // MOBA_BLOCK
    own_id = pl.program_id(1)

    @pl.when(own_id == 0)
    def _():
        k = k_ref[...]
        k16_ref[...] = k.astype(BF16)
        v16_ref[...] = v_ref[...].astype(BF16)
        kmean_ref[...] = jnp.mean(k.reshape(nb, MOBA_BLOCK, width), axis=1)

    q16 = q_ref[...].astype(BF16)
    diagonal = jnp.where(lax.broadcasted_iota(I32, (tq, MOBA_BLOCK), 1)
                         <= lax.broadcasted_iota(I32, (tq, MOBA_BLOCK), 0), 1.0, 0.0)

    def attend(own):
        nidx = lax.broadcasted_iota(I32, (tq, nb), 1)
        for h in range(B_HEADS):
            sl = slice(h * B_HD, (h + 1) * B_HD)
            q = q16[:, sl]
            g = jnp.where(nidx < own, _dot_nt(q, kmean_ref[:, sl].astype(BF16)), NEG_INF)
            rank = jnp.zeros((tq, nb), I32)
            for m in range(own):
                gm = g[:, m:m + 1]
                rank += jnp.where((gm > g) | ((gm == g) & (m < nidx)), 1, 0)
            chosen = jnp.where((nidx < own) & (rank < ksel), 1.0, 0.0)
            mask = jnp.concatenate(
                [jnp.broadcast_to(chosen[:, n:n + 1], (tq, MOBA_BLOCK)) for n in range(own)] + [diagonal], axis=1)
            n_keys = (own + 1) * MOBA_BLOCK
            o = _masked_attention(q, k16_ref[0:n_keys, sl], v16_ref[0:n_keys, sl], mask > 0.5, B_HD ** -0.5)
            o_ref[:, sl] = o.astype(o_ref.dtype)

    for own in range(nb):
        pl.when(own_id == own)(functools.partial(attend, own))


def moba_prompt(q, k, v):
    b, s, _ = q.shape
    nb = s // MOBA_BLOCK
    tq = MOBA_BLOCK
    blk = lambda bi, i: (bi, i, 0)
    full = lambda bi, i: (bi, 0, 0)
    return pl.pallas_call(
        functools.partial(_moba_prompt_kernel, ksel=max(1, min(MOBA_TOPK, nb - 1))),
        grid=(b, nb),
        in_specs=[pl.BlockSpec((None, tq, B_WIDTH), blk),
                  pl.BlockSpec((None, s, B_WIDTH), full),
                  pl.BlockSpec((None, s, B_WIDTH), full)],
        out_specs=pl.BlockSpec((None, tq, B_WIDTH), blk),
        out_shape=jax.ShapeDtypeStruct((b, s, B_WIDTH), BF16),
        scratch_shapes=[pltpu.VMEM((nb, B_WIDTH), F32), pltpu.VMEM((s, B_WIDTH), BF16),
                        pltpu.VMEM((s, B_WIDTH), BF16)],
        compiler_params=_cparams("parallel", "arbitrary"),
    )(q, k, v)


def _split3(x):
    hi = x.astype(BF16)
    r1 = x - hi.astype(F32)
    mid = r1.astype(BF16)
    lo = (r1 - mid.astype(F32)).astype(BF16)
    return hi, mid, lo


def _dot_tn(a, b):
    return lax.dot_general(a, b, (((0,), (0,)), ((), ())), preferred_element_type=F32)


def _wkv_chunk(r, ld, k, v, kkr, a, states, c):
    heads = range(len(states))
    hs = lambda x, h: x[:, h * C_HD:(h + 1) * C_HD]
    row = lax.broadcasted_iota(I32, (c, c), 0)
    col = lax.broadcasted_iota(I32, (c, c), 1)
    strict = row > col
    incl = row >= col
    incl16 = jnp.where(incl, 1.0, 0.0).astype(BF16)
    eye = jnp.where(row == col, 1.0, 0.0)
    hi, mid, lo = _split3(ld)
    cum = _dot(incl16, hi) + _dot(incl16, mid) + _dot(incl16, lo)
    g_inc = jnp.exp(cum)
    g_exc = jnp.exp(cum - ld)
    g_inv = jnp.exp(-cum)
    g_end = g_inc[c - 1:c, :]
    rt = r * g_inc
    kt = k * g_inv
    k_end = kt * g_end

    kk = []
    for h in heads:
        x = hs(kkr, h)
        kk.append(x / jnp.maximum(jnp.sqrt(jnp.sum(x * x, axis=-1, keepdims=True)), 1e-12))
    at16 = [(-kk[h] * hs(g_exc, h)).astype(BF16) for h in heads]
    bt = [kk[h] * hs(a, h) * hs(g_inv, h) for h in heads]
    ar16 = [jnp.concatenate([at16[h], hs(rt, h).astype(BF16)], axis=0) for h in heads]
    bk16 = [jnp.concatenate([bt[h], hs(kt, h)], axis=0).astype(BF16) for h in heads]
    gram = [_dot_nt(ar16[h], bk16[h]) for h in heads]
    l_ab = [jnp.where(strict, gram[h][:c, :c], 0.0) for h in heads]
    l_ak16 = [jnp.where(strict, gram[h][:c, c:], 0.0).astype(BF16) for h in heads]
    col2 = lax.broadcasted_iota(I32, (c, 2 * c), 1)
    incl2 = lax.broadcasted_iota(I32, (c, 2 * c), 0) >= jnp.where(col2 >= c, col2 - c, col2)
    l_r16 = [jnp.where(incl2, gram[h][c:, :], 0.0).astype(BF16) for h in heads]
    l_rb16 = [l_r16[h][:, :c] for h in heads]
    inv = [eye + l_ab[h] for h in heads]
    lp16 = [l_ab[h].astype(BF16) for h in heads]
    n = 2
    while n < c:
        lp16 = [_dot(lp16[h], lp16[h]).astype(BF16) for h in heads]
        inv = [inv[h] + _dot(inv[h].astype(BF16), lp16[h]) for h in heads]
        n *= 2
    inv16 = [inv[h].astype(BF16) for h in heads]
    v16 = [hs(v, h).astype(BF16) for h in heads]
    lakv16 = [_dot(l_ak16[h], v16[h]).astype(BF16) for h in heads]
    w16 = [_dot(inv16[h], at16[h]).astype(BF16) for h in heads]
    u0_16 = [_dot(inv16[h], lakv16[h]).astype(BF16) for h in heads]
    r2_16 = [(hs(rt, h) + _dot(l_rb16[h], w16[h])).astype(BF16) for h in heads]
    uv16 = [jnp.concatenate([u0_16[h], v16[h]], axis=0) for h in heads]
    y0 = [_dot(l_r16[h], uv16[h]) for h in heads]
    bend16 = [(bt[h] * hs(g_end, h)).astype(BF16) for h in heads]
    bkend16 = [jnp.concatenate([bend16[h], hs(k_end, h).astype(BF16)], axis=0) for h in heads]
    m2_16 = [_dot_tn(w16[h], bend16[h]).astype(BF16) for h in heads]
    nmat = [_dot_tn(uv16[h], bkend16[h]) for h in heads]
    s16 = [states[h].astype(BF16) for h in heads]
    ys = [_dot_nt(r2_16[h], s16[h]) + y0[h] for h in heads]
    new_states = [states[h] * hs(g_end, h) + _dot(s16[h], m2_16[h]) + nmat[h] for h in heads]
    return ys, new_states


WKV_VECS = ('c_w0', 'c_a0', 'c_k_k', 'c_k_a', 'c_ln_w', 'c_ln_b', 'c_r_k')


def _wkv_kernel(p_ref, prev_ref, mu_ref, vec_ref, w2_ref, a2_ref, g2_ref, s0_ref,
                o_ref, sfin_ref, state_ref, last_ref, *, chunk, valid_len):
    tb = p_ref.shape[0]
    t_blk = pl.program_id(1)
    hs = lambda x, h: x[:, h * C_HD:(h + 1) * C_HD]
    w0, a0, k_k, k_a, ln_w, ln_b, r_k = (vec_ref[i:i + 1, :] for i in range(len(WKV_VECS)))

    @pl.when(t_blk == 0)
    def _():
        state_ref[...] = s0_ref[...]
        last_ref[...] = prev_ref[...]

    def chunk_step(ci, prev_row):
        start = pl.multiple_of(ci * chunk, chunk)
        rows = pl.ds(start, chunk)
        pc = p_ref[rows, :]
        first = lax.broadcasted_iota(I32, pc.shape, 0) == 0
        shifted = jnp.where(first, prev_row, pltpu.roll(pc, 1, 0))
        xs = pc + mu_ref[...] * (shifted - pc)
        r = xs[:, 0:C_WIDTH]
        k = xs[:, C_WIDTH:2 * C_WIDTH]
        v = xs[:, 2 * C_WIDTH:3 * C_WIDTH]
        lora = xs[:, LORA_OFF:C_BLOCK]
        z = w0 + _dot(jnp.tanh(lora).astype(BF16), w2_ref[...])
        ld = -jnp.exp(jnp.minimum(z, 0.0) - jnp.log1p(jnp.exp(-jnp.abs(z))) - 0.5)
        a = 1.0 / (1.0 + jnp.exp(-(a0 + _dot(lora.astype(BF16), a2_ref[...]))))
        g = _dot((1.0 / (1.0 + jnp.exp(-lora))).astype(BF16), g2_ref[...])
        kkr = k * k_k
        k = k * (1.0 + (a - 1.0) * k_a)
        if valid_len is not None:
            live = (t_blk * tb + start + lax.broadcasted_iota(I32, k.shape, 0)) < valid_len
            ld, kkr, k, v = (jnp.where(live, x, 0.0) for x in (ld, kkr, k, v))
        states = [state_ref[h] for h in range(C_HEADS)]
        ys, new_states = _wkv_chunk(r, ld, k, v, kkr, a, states, chunk)
        for h in range(C_HEADS):
            state_ref[h] = new_states[h]
        rkr = r * k * r_k
        outs = []
        for h in range(C_HEADS):
            y = ys[h]
            mean = jnp.mean(y, axis=-1, keepdims=True)
            var = jnp.mean(jnp.square(y - mean), axis=-1, keepdims=True)
            yn = (y - mean) * lax.rsqrt(var + GN_EPS)
            bonus = jnp.sum(hs(rkr, h), axis=-1, keepdims=True) * hs(v, h)
            outs.append(yn * hs(ln_w, h) + hs(ln_b, h) + bonus)
        o_ref[rows, :] = (jnp.concatenate(outs, axis=1) * g).astype(o_ref.dtype)
        return pc[chunk - 1:chunk, :]

    last_ref[...] = lax.fori_loop(0, tb // chunk, chunk_step, last_ref[...])

    @pl.when(t_blk == pl.num_programs(1) - 1)
    def _():
        sfin_ref[...] = state_ref[...]


def wkv7(p, prev, mu, vecs, w2, a2, g2, s0, chunk, tb, valid_len):
    b, s, _ = p.shape
    const = lambda shape: pl.BlockSpec(shape, lambda bi, i: (0,) * len(shape))
    st = pl.BlockSpec((None, C_HEADS, C_HD, C_HD), lambda bi, i: (bi, 0, 0, 0))
    return pl.pallas_call(
        functools.partial(_wkv_kernel, chunk=chunk, valid_len=valid_len if valid_len < s else None),
        grid=(b, s // tb),
        in_specs=[pl.BlockSpec((None, tb, C_BLOCK), lambda bi, i: (bi, i, 0)),
                  pl.BlockSpec((None, 1, C_BLOCK), lambda bi, i: (bi, 0, 0)),
                  const(mu.shape), const(vecs.shape), const(w2.shape), const(a2.shape), const(g2.shape), st],
        out_specs=[pl.BlockSpec((None, tb, C_WIDTH), lambda bi, i: (bi, i, 0)), st],
        out_shape=[jax.ShapeDtypeStruct((b, s, C_WIDTH), BF16),
                   jax.ShapeDtypeStruct((b, C_HEADS, C_HD, C_HD), F32)],
        scratch_shapes=[pltpu.VMEM((C_HEADS, C_HD, C_HD), F32), pltpu.VMEM((1, C_BLOCK), F32)],
        compiler_params=_cparams("parallel", "arbitrary"),
    )(p, prev, mu, vecs, w2, a2, g2, s0)


def _split2(x):
    hi = x.astype(BF16)
    return hi, (x - hi.astype(F32)).astype(BF16)


def _page_specs(rows, width, layer, n_pages, pg):
    assert n_pages % pg == 0
    last_group = n_pages // pg - 1

    def spec(slot):
        def index_map(b, j, pt):
            return (layer, pt[b, jnp.minimum(j, last_group) * pg + slot], 0, 0)
        return pl.BlockSpec((None, None, rows, width), index_map)
    return [spec(slot) for slot in range(pg)]


def _idx_sample_kernel(pt_ref, qi_ref, w_ref, knew_ref, *rest, t):
    pages, o_ref = rest[:-1], rest[-1]
    j = pl.program_id(1)
    last = pl.num_programs(1) - 1
    q_hi, q_lo = _split2(qi_ref[...])
    w = w_ref[...]

    def scores(kpages):
        splits = [_split2(kp) for kp in kpages]
        s = [_dot(q_hi, k_hi) + _dot(q_hi, k_lo) + _dot(q_lo, k_hi) for k_hi, k_lo in splits]
        s = [jnp.maximum(x, 0.0) * w for x in s]
        return [jnp.sum(x.reshape(t, IDX_HEADS, PAGE_SIZE), axis=1) for x in s]

    @pl.when(j < last)
    def _():
        o_ref[...] = jnp.concatenate(scores([page[...] for page in pages]), axis=1)

    @pl.when(j == last)
    def _():
        o_ref[...] = jnp.zeros(o_ref.shape, F32)
        o_ref[:, 0:PAGE_SIZE] = scores([knew_ref[...]])[0]


def idx_sample(page_table, qi, w, knew, cache_idx, layer):
    db, rows, _ = qi.shape
    t = rows // IDX_HEADS
    n_pages = page_table.shape[1]
    pg = min(IDX_PAGES_PER_STEP, n_pages)
    steps = n_pages // pg + 1
    per_b = lambda b, j, pt: (b, 0, 0)
    return pl.pallas_call(
        functools.partial(_idx_sample_kernel, t=t),
        grid_spec=pltpu.PrefetchScalarGridSpec(
            num_scalar_prefetch=1,
            grid=(db, steps),
            in_specs=[pl.BlockSpec((None, rows, IDX_HD), per_b),
                      pl.BlockSpec((None, rows, 1), per_b),
                      pl.BlockSpec((None, IDX_HD, PAGE_SIZE), per_b)]
                     + _page_specs(IDX_HD, PAGE_SIZE, layer, n_pages, pg),
            out_specs=pl.BlockSpec((None, t, pg * PAGE_SIZE), lambda b, j, pt: (b, 0, j))),
        out_shape=jax.ShapeDtypeStruct((db, t, steps * pg * PAGE_SIZE), F32),
        compiler_params=_cparams("parallel", "arbitrary"),
    )(page_table, qi, w, knew, *([cache_idx] * pg))


def _select_sample_kernel(isc_ref, o_ref, *, topk, t, past_len):
    rows, n = isc_ref.shape
    pos = lax.broadcasted_iota(I32, (rows, n), 1)
    r = pl.program_id(0) * rows + lax.broadcasted_iota(I32, (rows, n), 0)
    causal = pos <= past_len + (r & (t - 1))
    key = _sortable_key(jnp.where(causal, isc_ref[...], NEG_INF))
    mask = _topk_mask(key, pos, topk, n.bit_length()) & causal
    o_ref[...] = jnp.where(mask, 1.0, 0.0)


def select_sample(isc, topk, t, past_len):
    rows, n = isc.shape
    tr = 16
    assert t & (t - 1) == 0 and rows % tr == 0
    return pl.pallas_call(
        functools.partial(_select_sample_kernel, topk=topk, t=t, past_len=past_len),
        grid=(rows // tr,),
        in_specs=[pl.BlockSpec((tr, n), lambda i: (i, 0))],
        out_specs=pl.BlockSpec((tr, n), lambda i: (i, 0)),
        out_shape=jax.ShapeDtypeStruct((rows, n), F32),
        compiler_params=_cparams("parallel"),
    )(isc)


def _own_head(shape, heads):
    return ((lax.broadcasted_iota(I32, shape, 1) & (heads - 1))
            == (lax.broadcasted_iota(I32, shape, 0) & (heads - 1)))


def _dsa_sample_kernel(pt_ref, q_ref, mask_ref, knew_ref, vnew_ref, *rest, heads, pg):
    kpages, vpages = rest[:pg], rest[pg:2 * pg]
    o_ref, m_ref, l_ref, acc_ref = rest[2 * pg:]
    j = pl.program_id(1)
    last = pl.num_programs(1) - 1
    q = q_ref[...]
    page_rows = PAGE_SIZE * heads
    shift = heads.bit_length() - 1
    spread = jnp.where(
        jnp.right_shift(lax.broadcasted_iota(I32, (PAGE_SIZE, page_rows), 1), shift)
        == lax.broadcasted_iota(I32, (PAGE_SIZE, page_rows), 0), 1.0, 0.0).astype(BF16)

    @pl.when(j == 0)
    def _():
        m_ref[...] = jnp.full(m_ref.shape, NEG_INF, F32)
        l_ref[...] = jnp.zeros(l_ref.shape, F32)
        acc_ref[...] = jnp.zeros(acc_ref.shape, F32)

    def update(ks, vs, mask):
        s = jnp.concatenate([_dot_nt(q, kp.astype(BF16)) for kp in ks], axis=1) * (A_HD ** -0.5)
        sel = jnp.concatenate([_dot(mask[:, i * PAGE_SIZE:(i + 1) * PAGE_SIZE].astype(BF16), spread)
                               for i in range(len(ks))], axis=1)
        s = jnp.where((sel > 0.5) & _own_head(s.shape, heads), s, NEG_INF)
        m_old = m_ref[...]
        m_new = jnp.maximum(m_old, jnp.max(s, axis=-1, keepdims=True))
        m_safe = jnp.where(m_new == NEG_INF, 0.0, m_new)
        p = jnp.exp(s - m_safe)
        alpha = jnp.exp(m_old - m_safe)
        l_ref[...] = alpha * l_ref[...] + jnp.sum(p, axis=-1, keepdims=True)
        acc = alpha * acc_ref[...]
        for i, vp in enumerate(vs):
            acc += _dot(p[:, i * page_rows:(i + 1) * page_rows].astype(BF16), vp.astype(BF16))
        acc_ref[...] = acc
        m_ref[...] = m_new

    @pl.when(j < last)
    def _():
        update([kp[...] for kp in kpages], [vp[...] for vp in vpages], mask_ref[...])

    @pl.when(j == last)
    def _():
        update([knew_ref[...]], [vnew_ref[...]], mask_ref[:, 0:PAGE_SIZE])
        o_ref[...] = acc_ref[...] / l_ref[...]


def dsa_sample(page_table, q, mask, knew, vnew, cache_k, cache_v, layer, heads):
    db, rows, hd = q.shape
    n_pages = page_table.shape[1]
    pg = min(SAMPLE_PAGES_PER_STEP, n_pages)
    steps = n_pages // pg + 1
    page_rows = PAGE_SIZE * heads
    per_b = lambda b, j, pt: (b, 0, 0)
    return pl.pallas_call(
        functools.partial(_dsa_sample_kernel, heads=heads, pg=pg),
        grid_spec=pltpu.PrefetchScalarGridSpec(
            num_scalar_prefetch=1,
            grid=(db, steps),
            in_specs=[pl.BlockSpec((None, rows, hd), per_b),
                      pl.BlockSpec((None, rows, pg * PAGE_SIZE), lambda b, j, pt: (b, 0, j)),
                      pl.BlockSpec((None, page_rows, hd), per_b),
                      pl.BlockSpec((None, page_rows, hd), per_b)]
                     + _page_specs(page_rows, hd, layer, n_pages, pg) + _page_specs(page_rows, hd, layer, n_pages, pg),
            out_specs=pl.BlockSpec((None, rows, hd), per_b),
            scratch_shapes=[pltpu.VMEM((rows, 1), F32), pltpu.VMEM((rows, 1), F32),
                            pltpu.VMEM((rows, hd), F32)]),
        out_shape=jax.ShapeDtypeStruct((db, rows, hd), F32),
        compiler_params=_cparams("parallel", "arbitrary"),
    )(page_table, q, mask, knew, vnew, *([cache_k] * pg), *([cache_v] * pg))


def _moba_sample_kernel(pt_ref, q_ref, knew_ref, vnew_ref, *rest, heads, ksel, pg):
    ppb = MOBA_BLOCK // PAGE_SIZE
    kpages, vpages = rest[:pg], rest[pg:2 * pg]
    o_ref, g_ref, m_ref, l_ref, ob_ref = rest[2 * pg:]
    j = pl.program_id(1)
    last = pl.num_programs(1) - 1
    q = q_ref[...]
    q16 = q.astype(BF16)
    rows, hd = q.shape
    scale = B_HD ** -0.5
    sublanes = 8
    assert 2 * heads == sublanes and hd == LANES and rows % sublanes == 0
    wide = lambda x: jnp.broadcast_to(x, (rows, LANES))

    def partial_softmax(s, v16):
        m = jnp.max(s, axis=-1, keepdims=True)
        p = jnp.exp(s - m)
        return m, jnp.sum(p, axis=-1, keepdims=True), _dot(p.astype(BF16), v16)

    @pl.when(j < last)
    def _():
        blocks = range(pg // ppb)
        kblk = [jnp.concatenate([kpages[b * ppb + i][...] for i in range(ppb)], axis=0) for b in blocks]
        v16 = [jnp.concatenate([vpages[b * ppb + i][...] for i in range(ppb)], axis=0).astype(BF16)
               for b in blocks]
        s = [_dot_nt(q16, kblk[b].astype(BF16)) * scale for b in blocks]
        own = _own_head(s[0].shape, heads)
        s = [jnp.where(own, s[b], NEG_INF) for b in blocks]
        m = [jnp.max(s[b], axis=-1, keepdims=True) for b in blocks]
        p = [jnp.exp(s[b] - m[b]) for b in blocks]
        l = [jnp.sum(p[b], axis=-1, keepdims=True) for b in blocks]
        o = [_dot(p[b].astype(BF16), v16[b]) for b in blocks]
        fold = [jnp.sum(kblk[b].reshape(kblk[b].shape[0] // sublanes, sublanes, hd), axis=0) for b in blocks]
        kmean = [(fold[b] + pltpu.roll(fold[b], heads, 0)) * (1.0 / MOBA_BLOCK) for b in blocks]
        gate = [jnp.sum(q * jnp.concatenate([kmean[b]] * (rows // sublanes), axis=0), axis=-1, keepdims=True)
                for b in blocks]
        for b in blocks:
            n = j * (pg // ppb) + b
            g_ref[n] = wide(gate[b])
            m_ref[n] = wide(m[b])
            l_ref[n] = wide(l[b])
            ob_ref[n] = o[b]

    @pl.when(j == last)
    def _():
        shift = heads.bit_length() - 1
        s = _dot_nt(q16, knew_ref[...].astype(BF16)) * scale
        causal = (jnp.right_shift(lax.broadcasted_iota(I32, s.shape, 1), shift)
                  <= jnp.right_shift(lax.broadcasted_iota(I32, s.shape, 0), shift))
        s = jnp.where(causal & _own_head(s.shape, heads), s, NEG_INF)
        m_own, l_own, o_own = partial_softmax(s, vnew_ref[...].astype(BF16))
        m_own, l_own = wide(m_own), wide(l_own)
        gates = g_ref[...]
        nidx = lax.broadcasted_iota(I32, gates.shape, 0)
        chosen = jnp.zeros(gates.shape, F32)
        for _ in range(ksel):
            best = jnp.max(gates, axis=0, keepdims=True)
            first = jnp.min(jnp.where(gates == best, nidx, gates.shape[0]), axis=0, keepdims=True)
            pick = nidx == first
            chosen = jnp.where(pick, 1.0, chosen)
            gates = jnp.where(pick, NEG_INF, gates)
        m_blk = m_ref[...]
        m_tot = jnp.maximum(m_own, jnp.max(jnp.where(chosen > 0.5, m_blk, NEG_INF), axis=0))
        wgt = jnp.where(chosen > 0.5, jnp.exp(m_blk - m_tot[None]), 0.0)
        w_own = jnp.exp(m_own - m_tot)
        l_tot = w_own * l_own + jnp.sum(wgt * l_ref[...], axis=0)
        acc = w_own * o_own
        for n in range(gates.shape[0]):
            acc += wgt[n] * ob_ref[n]
        o_ref[...] = acc / l_tot


def moba_sample(page_table, q, knew, vnew, cache_k, cache_v, layer, heads):
    db, rows, hd = q.shape
    n_pages = page_table.shape[1]
    pg = min(SAMPLE_PAGES_PER_STEP, n_pages)
    steps = n_pages // pg + 1
    nbp = n_pages * PAGE_SIZE // MOBA_BLOCK
    page_rows = PAGE_SIZE * heads
    per_b = lambda b, j, pt: (b, 0, 0)
    return pl.pallas_call(
        functools.partial(_moba_sample_kernel, heads=heads, ksel=max(1, min(MOBA_TOPK, nbp)), pg=pg),
        grid_spec=pltpu.PrefetchScalarGridSpec(
            num_scalar_prefetch=1,
            grid=(db, steps),
            in_specs=[pl.BlockSpec((None, rows, hd), per_b),
                      pl.BlockSpec((None, page_rows, hd), per_b),
                      pl.BlockSpec((None, page_rows, hd), per_b)]
                     + _page_specs(page_rows, hd, layer, n_pages, pg) + _page_specs(page_rows, hd, layer, n_pages, pg),
            out_specs=pl.BlockSpec((None, rows, hd), per_b),
            scratch_shapes=[pltpu.VMEM((nbp, rows, LANES), F32), pltpu.VMEM((nbp, rows, LANES), F32),
                            pltpu.VMEM((nbp, rows, LANES), F32), pltpu.VMEM((nbp, rows, hd), F32)]),
        out_shape=jax.ShapeDtypeStruct((db, rows, hd), F32),
        compiler_params=_cparams("parallel", "arbitrary"),
    )(page_table, q, knew, vnew, *([cache_k] * pg), *([cache_v] * pg))


def _new_page(x, rows):
    return jnp.pad(x, ((0, 0), (0, rows - x.shape[1]), (0, 0)))


def _sample_attention(qa, ka, va, qi, kiwi, qb, kb, vb, caches, page_table, layer, db, t):
    cache_a_k, cache_a_v, cache_a_idx, cache_b_k, cache_b_v = caches
    past_len = page_table.shape[1] * PAGE_SIZE
    topk = min(A_TOPK_MAX, (past_len + t) // 4)
    ki = kiwi[:, :IDX_HD].reshape(db, t, IDX_HD)
    wi = kiwi[:, IDX_HD:IDX_HD + IDX_HEADS]
    knew = _new_page(ki, PAGE_SIZE).transpose(0, 2, 1)
    isc = idx_sample(page_table, qi.reshape(db, t * IDX_HEADS, IDX_HD), wi.reshape(db, t * IDX_HEADS, 1),
                     knew, cache_a_idx, layer)
    n = isc.shape[-1]
    mask = select_sample(isc.reshape(db * t, n), topk, t, past_len).reshape(db, t, n)
    rows_a = lambda z: z.reshape(db, t * A_HEADS, A_HD)
    rows_b = lambda z: z.reshape(db, t * B_HEADS, B_HD)
    oa = dsa_sample(page_table, rows_a(qa).astype(BF16), jnp.repeat(mask, A_HEADS, axis=1),
                    _new_page(rows_a(ka), PAGE_SIZE * A_HEADS), _new_page(rows_a(va), PAGE_SIZE * A_HEADS),
                    cache_a_k, cache_a_v, layer, A_HEADS)
    ob = moba_sample(page_table, rows_b(qb), _new_page(rows_b(kb), PAGE_SIZE * B_HEADS),
                     _new_page(rows_b(vb), PAGE_SIZE * B_HEADS), cache_b_k, cache_b_v, layer, B_HEADS)
    return oa.reshape(db * t, A_WIDTH), ob.reshape(db * t, B_WIDTH)


def _pad_cols(x, width):
    return jnp.pad(x, [(0, 0)] * (x.ndim - 1) + [(0, width - x.shape[-1])])


def _layer_params(l, norm1_g, w_in, a_q_norm, a_k_norm, idx_k_norm, b_q_norm, b_k_norm, c_mu, c_w0, c_w2, c_a0,
                  c_a2, c_g2, c_k_k, c_k_a, c_r_k, c_ln_w, c_ln_b, w_o, norm2_g, w_ff1, w_ff2):
    w = w_in[l]
    a_cols, b_cols, c_cols = w[:, :A_COLS], w[:, A_COLS:A_COLS + B_COLS], w[:, A_COLS + B_COLS:]
    w_pack = jnp.concatenate([_pad_cols(c_cols, A_OFF), _pad_cols(a_cols, B_OFF - A_OFF), b_cols],
                             axis=1).astype(BF16)
    lora_rows = lambda x, off: jnp.pad(x, ((off, LORA_BLOCK - off - x.shape[0]), (0, 0))).astype(BF16)
    vec = dict(c_w0=c_w0[l], c_a0=c_a0[l], c_k_k=c_k_k[l], c_k_a=c_k_a[l], c_ln_w=c_ln_w[l], c_ln_b=c_ln_b[l],
               c_r_k=c_r_k[l].reshape(-1))
    vecs = jnp.stack([vec[name] for name in WKV_VECS] + [jnp.zeros((C_WIDTH,), F32)])
    gains = jnp.stack([a_q_norm[l], a_k_norm[l], b_q_norm[l], b_k_norm[l], _pad_cols(idx_k_norm[l], LANES)]
                      + [jnp.zeros((LANES,), F32)] * 3)
    return dict(norm1_g=norm1_g[l], w_in=w_pack, gains=gains, mu=_pad_cols(c_mu[l], C_BLOCK)[None], vecs=vecs,
                w2=lora_rows(c_w2[l], 0), a2=lora_rows(c_a2[l], W_LORA), g2=lora_rows(c_g2[l], W_LORA + A_LORA),
                w_o=w_o[l].astype(BF16), norm2_g=norm2_g[l], w_ff1=w_ff1[l].astype(BF16),
                w_ff2=w_ff2[l].astype(BF16))


def _layer(x, tabs, prev_shift, wkv0, lw, sample_ctx):
    b, s, d = x.shape
    m = b * s
    p = norm_matmul(x.reshape(m, d), lw['norm1_g'], lw['w_in'], min(m, 1024), PROJ_TILE_N)
    qa, ka, va, qi, kiwi, qb, kb, vb, ka_rows, va_rows, kb_rows, vb_rows = prep(
        p, tabs[0], tabs[1], lw['gains'], min(m, 256))
    p3 = p.reshape(b, s, P_PACK)
    if sample_ctx is None:
        seq = lambda z: z.reshape(b, s, z.shape[-1])
        oa = dsa_prompt(seq(qa), seq(ka), seq(va), seq(qi), seq(kiwi)).reshape(m, A_WIDTH)
        ob = moba_prompt(seq(qb), seq(kb), seq(vb)).reshape(m, B_WIDTH)
        p_scan = p3
    else:
        caches, page_table, layer = sample_ctx
        oa, ob = _sample_attention(qa, ka_rows, va_rows, qi, kiwi, qb, kb_rows, vb_rows, caches, page_table,
                                   layer, b, s)
        oa, ob = oa.astype(BF16), ob.astype(BF16)
        p_scan = jnp.pad(p3, ((0, 0), (0, -s % 16), (0, 0)))
    sp = p_scan.shape[1]
    chunk = min(WKV_CHUNK, sp)
    oc, wkv_fin = wkv7(p_scan, _pad_cols(prev_shift, C_BLOCK)[:, None, :], lw['mu'], lw['vecs'], lw['w2'],
                       lw['a2'], lw['g2'], wkv0, chunk, min(sp, 4 * chunk), s)
    tm2 = min(m, 512)
    x2 = out_proj(x.reshape(m, d), oa, ob, oc[:, :s].reshape(m, C_WIDTH), lw['w_o'], tm2)
    x3 = mlp(x2, lw['norm2_g'], lw['w_ff1'], lw['w_ff2'], tm2, min(MLP_TILE_F, lw['w_ff1'].shape[1])).reshape(b, s, d)
    rows = (ka_rows.reshape(b, s, A_HEADS, A_HD), va_rows.reshape(b, s, A_HEADS, A_HD),
            kiwi[:, :IDX_HD].reshape(b, s, IDX_HD), kb_rows.reshape(b, s, B_HEADS, B_HD),
            vb_rows.reshape(b, s, B_HEADS, B_HD), wkv_fin, p3[:, s - 1, :C_COLS])
    return x3, rows


def kernel(x_prompt, x_sample, cache_a_k, cache_a_v, cache_a_idx, cache_b_k, cache_b_v, state_wkv, state_shift, page_table, norm1_g, w_in, a_q_norm, a_k_norm, idx_k_norm, b_q_norm, b_k_norm, c_mu, c_w0, c_w2, c_a0, c_a2, c_g2, c_k_k, c_k_a, c_r_k, c_ln_w, c_ln_b, w_o, norm2_g, w_ff1, w_ff2):
    depth = w_in.shape[0]
    bp, sp = x_prompt.shape[:2]
    db, t = x_sample.shape[:2]
    pos_p = jnp.arange(sp)
    pos_s = jnp.tile(page_table.shape[1] * PAGE_SIZE + jnp.arange(t), db)
    tabs_p = (_rope_tables(pos_p, A_HD, 1), _rope_tables(pos_p, IDX_HD, LANES // IDX_HD))
    tabs_s = (_rope_tables(pos_s, A_HD, 1), _rope_tables(pos_s, IDX_HD, LANES // IDX_HD))
    shift0 = jnp.zeros((bp, C_COLS), x_prompt.dtype)
    wkv0 = jnp.zeros((bp, C_HEADS, C_HD, C_HD), F32)
    xp, xs = x_prompt, x_sample
    by_row = lambda c: c.reshape(c.shape[0], c.shape[1], -1, c.shape[-1])
    caches = (by_row(cache_a_k), by_row(cache_a_v), cache_a_idx.transpose(0, 1, 3, 2),
              by_row(cache_b_k), by_row(cache_b_v))
    p_rows, s_rows = [], []
    for l in range(depth):
        lw = _layer_params(l, norm1_g, w_in, a_q_norm, a_k_norm, idx_k_norm, b_q_norm, b_k_norm, c_mu, c_w0, c_w2,
                           c_a0, c_a2, c_g2, c_k_k, c_k_a, c_r_k, c_ln_w, c_ln_b, w_o, norm2_g, w_ff1, w_ff2)
        xp, rows = _layer(xp, tabs_p, shift0, wkv0, lw, None)
        p_rows.append(rows)
        xs, rows = _layer(xs, tabs_s, state_shift[l], state_wkv[l], lw, (caches, page_table, l))
        s_rows.append(rows)
    p_out = [jnp.stack(z) for z in zip(*p_rows)]
    s_out = [jnp.stack(z) for z in zip(*s_rows)]
    return (xp, xs, *p_out, *s_out)
```

```python
import functools

import jax
import jax.numpy as jnp
from jax import lax
from jax.experimental import pallas as pl
from jax.experimental.pallas import tpu as pltpu

F32 = jnp.float32
BF16 = jnp.bfloat16
I32 = jnp.int32

VMEM_LIMIT_BYTES = 56 * 1024 * 1024
LANES = 128

PAGE_SIZE = 128
A_HEADS = 4
A_HD = 128
A_WIDTH = A_HEADS * A_HD
IDX_HEADS = 16
IDX_HD = 64
A_TOPK_MAX = 256
A_Q_BLOCK = 128
B_HEADS = 4
B_HD = 128
B_WIDTH = B_HEADS * B_HD
MOBA_BLOCK = 256
MOBA_TOPK = 3
C_HD = 64
C_HEADS = 16
C_WIDTH = C_HEADS * C_HD
W_LORA = 64
A_LORA = 64
G_LORA = 160
A_COLS = 3 * A_WIDTH + IDX_HEADS * IDX_HD + IDX_HD + IDX_HEADS
B_COLS = 3 * B_WIDTH
C_COLS = 3 * C_WIDTH + W_LORA + A_LORA + G_LORA
ROPE_THETA = 500000.0
NORM_EPS = 1e-6
GN_EPS = 64e-5
IDX_W_SCALE = (IDX_HEADS * IDX_HD) ** -0.5
WKV_CHUNK = 64
SAMPLE_PAGES_PER_STEP = 16
IDX_PAGES_PER_STEP = 64
DSA_KEY_BUCKET = 256
PROJ_TILE_N = 1536
MLP_TILE_F = 1024

C_BLOCK = C_COLS + (-C_COLS % LANES)
A_OFF = C_BLOCK
B_OFF = A_OFF + A_COLS + (-A_COLS % LANES)
P_PACK = B_OFF + B_COLS
assert P_PACK % PROJ_TILE_N == 0
LORA_OFF = 3 * C_WIDTH
LORA_BLOCK = C_BLOCK - LORA_OFF
A_QI_OFF = A_OFF + 3 * A_WIDTH
A_KI_OFF = A_QI_OFF + IDX_HEADS * IDX_HD
assert A_KI_OFF + LANES == B_OFF and IDX_HD + IDX_HEADS <= LANES

NEG_INF = float("-inf")
INT_MIN = -2 ** 31


def _cparams(*sem):
    return pltpu.CompilerParams(dimension_semantics=sem, vmem_limit_bytes=VMEM_LIMIT_BYTES)


def _dot_nt(a, b):
    return lax.dot_general(a, b, (((1,), (1,)), ((), ())), preferred_element_type=F32)


def _dot(a, b):
    return jnp.dot(a, b, preferred_element_type=F32)


def _norm_matmul_kernel(x_ref, g_ref, w_ref, o_ref, xn_ref):
    @pl.when(pl.program_id(1) == 0)
    def _():
        x = x_ref[...]
        y = x * lax.rsqrt(jnp.mean(x * x, axis=-1, keepdims=True) + NORM_EPS)
        xn_ref[...] = (y * g_ref[...]).astype(BF16)

    o_ref[...] = _dot(xn_ref[...], w_ref[...])


def norm_matmul(x, g, w, tm, tn):
    m, d = x.shape
    p = w.shape[1]
    return pl.pallas_call(
        _norm_matmul_kernel,
        grid=(m // tm, p // tn),
        in_specs=[pl.BlockSpec((tm, d), lambda i, j: (i, 0)),
                  pl.BlockSpec((1, d), lambda i, j: (0, 0)),
                  pl.BlockSpec((d, tn), lambda i, j: (0, j))],
        out_specs=pl.BlockSpec((tm, tn), lambda i, j: (i, j)),
        out_shape=jax.ShapeDtypeStruct((m, p), F32),
        scratch_shapes=[pltpu.VMEM((tm, d), BF16)],
        compiler_params=_cparams("parallel", "arbitrary"),
    )(x, g.reshape(1, d), w)


def _out_proj_kernel(x_ref, oa_ref, ob_ref, oc_ref, w_ref, o_ref):
    wa = oa_ref.shape[1]
    wb = ob_ref.shape[1]
    acc = _dot(oa_ref[...], w_ref[0:wa, :])
    acc += _dot(ob_ref[...], w_ref[wa:wa + wb, :])
    acc += _dot(oc_ref[...], w_ref[wa + wb:, :])
    o_ref[...] = x_ref[...] + acc


def out_proj(x, oa, ob, oc, w, tm):
    m, d = x.shape
    row = lambda i: (i, 0)
    return pl.pallas_call(
        _out_proj_kernel,
        grid=(m // tm,),
        in_specs=[pl.BlockSpec((tm, d), row),
                  pl.BlockSpec((tm, oa.shape[1]), row),
                  pl.BlockSpec((tm, ob.shape[1]), row),
                  pl.BlockSpec((tm, oc.shape[1]), row),
                  pl.BlockSpec(w.shape, lambda i: (0, 0))],
        out_specs=pl.BlockSpec((tm, d), row),
        out_shape=jax.ShapeDtypeStruct((m, d), F32),
        compiler_params=_cparams("parallel"),
    )(x, oa, ob, oc, w)


def _mlp_kernel(x_ref, g_ref, w1_ref, w2_ref, o_ref, xn_ref):
    @pl.when(pl.program_id(1) == 0)
    def _():
        x = x_ref[...]
        y = x * lax.rsqrt(jnp.mean(x * x, axis=-1, keepdims=True) + NORM_EPS)
        xn_ref[...] = (y * g_ref[...]).astype(BF16)
        o_ref[...] = x

    h = jnp.maximum(_dot(xn_ref[...], w1_ref[...]), 0.0)
    o_ref[...] += _dot((h * h).astype(BF16), w2_ref[...])


def mlp(x, g, w1, w2, tm, tf):
    m, d = x.shape
    f = w1.shape[1]
    return pl.pallas_call(
        _mlp_kernel,
        grid=(m // tm, f // tf),
        in_specs=[pl.BlockSpec((tm, d), lambda i, j: (i, 0)),
                  pl.BlockSpec((1, d), lambda i, j: (0, 0)),
                  pl.BlockSpec((d, tf), lambda i, j: (0, j)),
                  pl.BlockSpec((tf, d), lambda i, j: (j, 0))],
        out_specs=pl.BlockSpec((tm, d), lambda i, j: (i, 0)),
        out_shape=jax.ShapeDtypeStruct((m, d), F32),
        scratch_shapes=[pltpu.VMEM((tm, d), BF16)],
        compiler_params=_cparams("parallel", "arbitrary"),
    )(x, g.reshape(1, d), w1, w2)


def _rope_tables(pos, hd, reps):
    rot = hd // 4
    half = rot // 2
    inv = ROPE_THETA ** (-2.0 * jnp.arange(half, dtype=F32) / rot)
    ang = pos.astype(F32)[:, None] * inv[None, :]
    cos, sin = jnp.cos(ang), jnp.sin(ang)
    zeros = jnp.zeros_like(sin)
    rest = jnp.zeros((pos.shape[0], hd - rot), F32)
    tabs = [jnp.concatenate([cos, cos, rest + 1.0], axis=1),
            jnp.concatenate([zeros, sin, rest], axis=1),
            jnp.concatenate([-sin, zeros, rest], axis=1)]
    return jnp.stack([jnp.tile(z, (1, reps)) for z in tabs])


def _rope(y, tab_ref, half):
    width = y.shape[1]
    return (y * tab_ref[0] + pltpu.roll(y, half, 1) * tab_ref[1]
            + pltpu.roll(y, width - half, 1) * tab_ref[2])


def _prep_kernel(p_ref, ta_ref, ti_ref, g_ref, qa_ref, ka_ref, va_ref, qi_ref, kiwi_ref, qb_ref, kb_ref, vb_ref,
                 ka_rows_ref, va_rows_ref, kb_rows_ref, vb_rows_ref):
    tm = p_ref.shape[0]

    def normed(x, g):
        return x * lax.rsqrt(jnp.mean(x * x, axis=-1, keepdims=True) + NORM_EPS) * g

    for base, q_ref, k_ref, v_ref, k_rows_ref, v_rows_ref, heads, hd, gq, gk in (
            (A_OFF, qa_ref, ka_ref, va_ref, ka_rows_ref, va_rows_ref, A_HEADS, A_HD, 0, 1),
            (B_OFF, qb_ref, kb_ref, vb_ref, kb_rows_ref, vb_rows_ref, B_HEADS, B_HD, 2, 3)):
        width = heads * hd
        for h in range(heads):
            sl = slice(h * hd, (h + 1) * hd)
            q_ref[:, sl] = _rope(normed(p_ref[:, base + h * hd:base + (h + 1) * hd], g_ref[gq:gq + 1, :]),
                                 ta_ref, hd // 8)
            k = _rope(normed(p_ref[:, base + width + h * hd:base + width + (h + 1) * hd],
                             g_ref[gk:gk + 1, :]), ta_ref, hd // 8)
            v = p_ref[:, base + 2 * width + h * hd:base + 2 * width + (h + 1) * hd]
            k_ref[:, sl] = k
            v_ref[:, sl] = v
            k_rows_ref[pl.ds(h, tm, stride=heads), :] = k
            v_rows_ref[pl.ds(h, tm, stride=heads), :] = v
    for j in range(IDX_HEADS * IDX_HD // LANES):
        sl = slice(j * LANES, (j + 1) * LANES)
        qi_ref[:, sl] = _rope(p_ref[:, A_QI_OFF + j * LANES:A_QI_OFF + (j + 1) * LANES], ti_ref, IDX_HD // 8)
    x = p_ref[:, A_KI_OFF:A_KI_OFF + LANES]
    is_key = lax.broadcasted_iota(I32, x.shape, 1) < IDX_HD
    ms = jnp.sum(jnp.where(is_key, x * x, 0.0), axis=-1, keepdims=True) * (1.0 / IDX_HD)
    ki = _rope(x * lax.rsqrt(ms + NORM_EPS) * g_ref[4:5, :], ti_ref, IDX_HD // 8)
    kiwi_ref[...] = jnp.where(is_key, ki, x * IDX_W_SCALE)


def prep(p, tab_a, tab_i, gains, tm):
    m = p.shape[0]
    period = tab_a.shape[1] // tm
    row = lambda i: (i, 0)
    tab = pl.BlockSpec((3, tm, LANES), lambda i: (0, i % period, 0))
    widths = (A_WIDTH, A_WIDTH, A_WIDTH, IDX_HEADS * IDX_HD, LANES, B_WIDTH, B_WIDTH, B_WIDTH)
    by_head = ((A_HEADS, A_HD), (A_HEADS, A_HD), (B_HEADS, B_HD), (B_HEADS, B_HD))
    return pl.pallas_call(
        _prep_kernel,
        grid=(m // tm,),
        in_specs=[pl.BlockSpec((tm, P_PACK), row), tab, tab, pl.BlockSpec(gains.shape, lambda i: (0, 0))],
        out_specs=[pl.BlockSpec((tm, w), row) for w in widths]
                  + [pl.BlockSpec((tm * h, hd), row) for h, hd in by_head],
        out_shape=[jax.ShapeDtypeStruct((m, w), F32) for w in widths]
                  + [jax.ShapeDtypeStruct((m * h, hd), F32) for h, hd in by_head],
        compiler_params=_cparams("parallel"),
    )(p, tab_a, tab_i, gains)


def _sortable_key(x):
    bits = pltpu.bitcast(x, I32)
    return jnp.where(bits >= 0, bits, bits ^ jnp.int32(0x7FFFFFFF))


def _topk_mask(score, pos, topk, n_pos_bits):
    rows, n = score.shape
    key = _sortable_key(score)

    def count(m):
        return jnp.sum(jnp.where(m, 1.0, 0.0), axis=-1, keepdims=True).astype(I32)

    if n % topk == 0 and topk % LANES == 0:
        gmax = score[:, 0:topk]
        for j in range(1, n // topk):
            gmax = jnp.maximum(gmax, score[:, j * topk:(j + 1) * topk])
        lo_u = _sortable_key(jnp.min(gmax, axis=-1, keepdims=True)) ^ jnp.int32(INT_MIN)
        hi_u = _sortable_key(jnp.max(gmax, axis=-1, keepdims=True)) ^ jnp.int32(INT_MIN)
        nbits = jnp.max((32 - lax.clz(lo_u ^ hi_u)).astype(F32)).astype(I32)
        keep = jnp.where(nbits >= 32, 0, jnp.left_shift(jnp.int32(-1), jnp.minimum(nbits, 31)))
        start = hi_u & keep
    else:
        nbits = 32
        start = jnp.zeros((rows, 1), I32)

    groups = 2 if rows % 16 == 0 else 1
    step = rows // groups
    parts = [key[g * step:(g + 1) * step] for g in range(groups)]

    def value_step(it, tus):
        bit = jnp.left_shift(jnp.int32(1), nbits - 1 - it)
        out = []
        for part, tu in zip(parts, tus):
            cand_u = tu | bit
            cnt = count(part >= (cand_u ^ jnp.int32(INT_MIN)))
            out.append(jnp.where(cnt >= topk, cand_u, tu))
        return tuple(out)

    tus = lax.fori_loop(0, nbits, value_step, tuple(start[g * step:(g + 1) * step] for g in range(groups)))
    thr = jnp.concatenate(tus, axis=0) ^ jnp.int32(INT_MIN)
    gt = key > thr
    eq = key == thr
    need = topk - count(gt)
    excess = count(eq) - need

    def tie_search(_):
        def pos_step(it, x):
            cand = x | jnp.left_shift(jnp.int32(1), n_pos_bits - 1 - it)
            cnt = count(eq & (pos < cand))
            return jnp.where(cnt < need, cand, x)
        return lax.fori_loop(0, n_pos_bits, pos_step, jnp.zeros((rows, 1), I32))

    last = lax.cond(jnp.max(excess) > 0, tie_search,
                    lambda _: jnp.full((rows, 1), 2 ** n_pos_bits, I32), 0)
    return gt | (eq & (pos <= last))


def _masked_attention(q, k, v, mask, scale):
    s = _dot_nt(q, k) * scale
    s = jnp.where(mask, s, NEG_INF)
    m = jnp.max(s, axis=-1, keepdims=True)
    p = jnp.exp(s - m)
    l = jnp.sum(p, axis=-1, keepdims=True)
    return _dot(p.astype(BF16), v) / l


def _dsa_prompt_kernel(q_ref, k_ref, v_ref, qi_ref, kiwi_ref, kiwi_blk_ref, o_ref, k16_ref, v16_ref, ki16_ref,
                       *, topk, bucket):
    tq = q_ref.shape[0]
    s_len = k_ref.shape[0]
    i = pl.program_id(1)

    @pl.when(i == 0)
    def _():
        k16_ref[...] = k_ref[...].astype(BF16)
        v16_ref[...] = v_ref[...].astype(BF16)
        ki16_ref[...] = kiwi_ref[:, 0:IDX_HD].astype(BF16)

    wi = kiwi_blk_ref[:, IDX_HD:IDX_HD + IDX_HEADS]
    qi16 = qi_ref[...].astype(BF16)
    q16 = q_ref[...].astype(BF16)

    def attend(n_keys):
        qpos = i * tq + lax.broadcasted_iota(I32, (tq, n_keys), 0)
        kpos = lax.broadcasted_iota(I32, (tq, n_keys), 1)
        causal = kpos <= qpos
        ki = ki16_ref[0:n_keys, :]
        isc = jnp.zeros((tq, n_keys), F32)
        for h in range(IDX_HEADS):
            s = _dot_nt(qi16[:, h * IDX_HD:(h + 1) * IDX_HD], ki)
            isc += jnp.maximum(s, 0.0) * wi[:, h:h + 1]
        mask = _topk_mask(jnp.where(causal, isc, NEG_INF), kpos, topk, (n_keys - 1).bit_length()) & causal
        for h in range(A_HEADS):
            sl = slice(h * A_HD, (h + 1) * A_HD)
            o = _masked_attention(q16[:, sl], k16_ref[0:n_keys, sl], v16_ref[0:n_keys, sl], mask, A_HD ** -0.5)
            o_ref[:, sl] = o.astype(o_ref.dtype)

    for bkt in range(s_len // bucket):
        pl.when(i // (bucket // tq) == bkt)(functools.partial(attend, (bkt + 1) * bucket))


def dsa_prompt(q, k, v, qi, kiwi):
    b, s, _ = q.shape
    tq = A_Q_BLOCK
    topk = min(A_TOPK_MAX, s // 4)
    bucket = min(s, DSA_KEY_BUCKET)
    assert s % bucket == 0 and bucket % tq == 0 and bucket >= topk
    blk = lambda bi, i: (bi, i, 0)
    full = lambda bi, i: (bi, 0, 0)
    return pl.pallas_call(
        functools.partial(_dsa_prompt_kernel, topk=topk, bucket=bucket),
        grid=(b, s // tq),
        in_specs=[pl.BlockSpec((None, tq, A_WIDTH), blk),
                  pl.BlockSpec((None, s, A_WIDTH), full),
                  pl.BlockSpec((None, s, A_WIDTH), full),
                  pl.BlockSpec((None, tq, IDX_HEADS * IDX_HD), blk),
                  pl.BlockSpec((None, s, LANES), full),
                  pl.BlockSpec((None, tq, LANES), blk)],
        out_specs=pl.BlockSpec((None, tq, A_WIDTH), blk),
        out_shape=jax.ShapeDtypeStruct((b, s, A_WIDTH), BF16),
        scratch_shapes=[pltpu.VMEM((s, A_WIDTH), BF16), pltpu.VMEM((s, A_WIDTH), BF16),
                        pltpu.VMEM((s, IDX_HD), BF16)],
        compiler_params=_cparams("parallel", "arbitrary"),
    )(q, k, v, qi, kiwi, kiwi)


def _moba_prompt_kernel(q_ref, k_ref, v_ref, o_ref, kmean_ref, k16_ref, v16_ref, *, ksel):
    tq = q_ref.shape[0]
    s_len, width = k_ref.shape
    nb = s_len // MOBA_BLOCK
    own_id = pl.program_id(1)

    @pl.when(own_id == 0)
    def _():
        k = k_ref[...]
        k16_ref[...] = k.astype(BF16)
        v16_ref[...] = v_ref[...].astype(BF16)
        kmean_ref[...] = jnp.mean(k.reshape(nb, MOBA_BLOCK, width), axis=1)

    q16 = q_ref[...].astype(BF16)
    diagonal = jnp.where(lax.broadcasted_iota(I32, (tq, MOBA_BLOCK), 1)
                         <= lax.broadcasted_iota(I32, (tq, MOBA_BLOCK), 0), 1.0, 0.0)

    def attend(own):
        nidx = lax.broadcasted_iota(I32, (tq, nb), 1)
        for h in range(B_HEADS):
            sl = slice(h * B_HD, (h + 1) * B_HD)
            q = q16[:, sl]
            g = jnp.where(nidx < own, _dot_nt(q, kmean_ref[:, sl].astype(BF16)), NEG_INF)
            rank = jnp.zeros((tq, nb), I32)
            for m in range(own):
                gm = g[:, m:m + 1]
                rank += jnp.where((gm > g) | ((gm == g) & (m < nidx)), 1, 0)
            chosen = jnp.where((nidx < own) & (rank < ksel), 1.0, 0.0)
            mask = jnp.concatenate(
                [jnp.broadcast_to(chosen[:, n:n + 1], (tq, MOBA_BLOCK)) for n in range(own)] + [diagonal], axis=1)
            n_keys = (own + 1) * MOBA_BLOCK
            o = _masked_attention(q, k16_ref[0:n_keys, sl], v16_ref[0:n_keys, sl], mask > 0.5, B_HD ** -0.5)
            o_ref[:, sl] = o.astype(o_ref.dtype)

    for own in range(nb):
        pl.when(own_id == own)(functools.partial(attend, own))


def moba_prompt(q, k, v):
    b, s, _ = q.shape
    nb = s // MOBA_BLOCK
    tq = MOBA_BLOCK
    blk = lambda bi, i: (bi, i, 0)
    full = lambda bi, i: (bi, 0, 0)
    return pl.pallas_call(
        functools.partial(_moba_prompt_kernel, ksel=max(1, min(MOBA_TOPK, nb - 1))),
        grid=(b, nb),
        in_specs=[pl.BlockSpec((None, tq, B_WIDTH), blk),
                  pl.BlockSpec((None, s, B_WIDTH), full),
                  pl.BlockSpec((None, s, B_WIDTH), full)],
        out_specs=pl.BlockSpec((None, tq, B_WIDTH), blk),
        out_shape=jax.ShapeDtypeStruct((b, s, B_WIDTH), BF16),
        scratch_shapes=[pltpu.VMEM((nb, B_WIDTH), F32), pltpu.VMEM((s, B_WIDTH), BF16),
                        pltpu.VMEM((s, B_WIDTH), BF16)],
        compiler_params=_cparams("parallel", "arbitrary"),
    )(q, k, v)


def _split3(x):
    hi = x.astype(BF16)
    r1 = x - hi.astype(F32)
    mid = r1.astype(BF16)
    lo = (r1 - mid.astype(F32)).astype(BF16)
    return hi, mid, lo


def _dot_tn(a, b):
    return lax.dot_general(a, b, (((0,), (0,)), ((), ())), preferred_element_type=F32)


def _wkv_chunk(r, ld, k, v, kkr, a, states, c):
    heads = range(len(states))
    hs = lambda x, h: x[:, h * C_HD:(h + 1) * C_HD]
    row = lax.broadcasted_iota(I32, (c, c), 0)
    col = lax.broadcasted_iota(I32, (c, c), 1)
    strict = row > col
    incl = row >= col
    incl16 = jnp.where(incl, 1.0, 0.0).astype(BF16)
    eye = jnp.where(row == col, 1.0, 0.0)
    hi, mid, lo = _split3(ld)
    cum = _dot(incl16, hi) + _dot(incl16, mid) + _dot(incl16, lo)
    g_inc = jnp.exp(cum)
    g_exc = jnp.exp(cum - ld)
    g_inv = jnp.exp(-cum)
    g_end = g_inc[c - 1:c, :]
    rt = r * g_inc
    kt = k * g_inv
    k_end = kt * g_end

    kk = []
    for h in heads:
        x = hs(kkr, h)
        kk.append(x / jnp.maximum(jnp.sqrt(jnp.sum(x * x, axis=-1, keepdims=True)), 1e-12))
    at16 = [(-kk[h] * hs(g_exc, h)).astype(BF16) for h in heads]
    bt = [kk[h] * hs(a, h) * hs(g_inv, h) for h in heads]
    ar16 = [jnp.concatenate([at16[h], hs(rt, h).astype(BF16)], axis=0) for h in heads]
    bk16 = [jnp.concatenate([bt[h], hs(kt, h)], axis=0).astype(BF16) for h in heads]
    gram = [_dot_nt(ar16[h], bk16[h]) for h in heads]
    l_ab = [jnp.where(strict, gram[h][:c, :c], 0.0) for h in heads]
    l_ak16 = [jnp.where(strict, gram[h][:c, c:], 0.0).astype(BF16) for h in heads]
    col2 = lax.broadcasted_iota(I32, (c, 2 * c), 1)
    incl2 = lax.broadcasted_iota(I32, (c, 2 * c), 0) >= jnp.where(col2 >= c, col2 - c, col2)
    l_r16 = [jnp.where(incl2, gram[h][c:, :], 0.0).astype(BF16) for h in heads]
    l_rb16 = [l_r16[h][:, :c] for h in heads]
    inv = [eye + l_ab[h] for h in heads]
    lp16 = [l_ab[h].astype(BF16) for h in heads]
    n = 2
    while n < c:
        lp16 = [_dot(lp16[h], lp16[h]).astype(BF16) for h in heads]
        inv = [inv[h] + _dot(inv[h].astype(BF16), lp16[h]) for h in heads]
        n *= 2
    inv16 = [inv[h].astype(BF16) for h in heads]
    v16 = [hs(v, h).astype(BF16) for h in heads]
    lakv16 = [_dot(l_ak16[h], v16[h]).astype(BF16) for h in heads]
    w16 = [_dot(inv16[h], at16[h]).astype(BF16) for h in heads]
    u0_16 = [_dot(inv16[h], lakv16[h]).astype(BF16) for h in heads]
    r2_16 = [(hs(rt, h) + _dot(l_rb16[h], w16[h])).astype(BF16) for h in heads]
    uv16 = [jnp.concatenate([u0_16[h], v16[h]], axis=0) for h in heads]
    y0 = [_dot(l_r16[h], uv16[h]) for h in heads]
    bend16 = [(bt[h] * hs(g_end, h)).astype(BF16) for h in heads]
    bkend16 = [jnp.concatenate([bend16[h], hs(k_end, h).astype(BF16)], axis=0) for h in heads]
    m2_16 = [_dot_tn(w16[h], bend16[h]).astype(BF16) for h in heads]
    nmat = [_dot_tn(uv16[h], bkend16[h]) for h in heads]
    s16 = [states[h].astype(BF16) for h in heads]
    ys = [_dot_nt(r2_16[h], s16[h]) + y0[h] for h in heads]
    new_states = [states[h] * hs(g_end, h) + _dot(s16[h], m2_16[h]) + nmat[h] for h in heads]
    return ys, new_states


WKV_VECS = ('c_w0', 'c_a0', 'c_k_k', 'c_k_a', 'c_ln_w', 'c_ln_b', 'c_r_k')


def _wkv_kernel(p_ref, prev_ref, mu_ref, vec_ref, w2_ref, a2_ref, g2_ref, s0_ref,
                o_ref, sfin_ref, state_ref, last_ref, *, chunk, valid_len):
    tb = p_ref.shape[0]
    t_blk = pl.program_id(1)
    hs = lambda x, h: x[:, h * C_HD:(h + 1) * C_HD]
    w0, a0, k_k, k_a, ln_w, ln_b, r_k = (vec_ref[i:i + 1, :] for i in range(len(WKV_VECS)))

    @pl.when(t_blk == 0)
    def _():
        state_ref[...] = s0_ref[...]
        last_ref[...] = prev_ref[...]

    def chunk_step(ci, prev_row):
        start = pl.multiple_of(ci * chunk, chunk)
        rows = pl.ds(start, chunk)
        pc = p_ref[rows, :]
        first = lax.broadcasted_iota(I32, pc.shape, 0) == 0
        shifted = jnp.where(first, prev_row, pltpu.roll(pc, 1, 0))
        xs = pc + mu_ref[...] * (shifted - pc)
        r = xs[:, 0:C_WIDTH]
        k = xs[:, C_WIDTH:2 * C_WIDTH]
        v = xs[:, 2 * C_WIDTH:3 * C_WIDTH]
        lora = xs[:, LORA_OFF:C_BLOCK]
        z = w0 + _dot(jnp.tanh(lora).astype(BF16), w2_ref[...])
        ld = -jnp.exp(jnp.minimum(z, 0.0) - jnp.log1p(jnp.exp(-jnp.abs(z))) - 0.5)
        a = 1.0 / (1.0 + jnp.exp(-(a0 + _dot(lora.astype(BF16), a2_ref[...]))))
        g = _dot((1.0 / (1.0 + jnp.exp(-lora))).astype(BF16), g2_ref[...])
        kkr = k * k_k
        k = k * (1.0 + (a - 1.0) * k_a)
        if valid_len is not None:
            live = (t_blk * tb + start + lax.broadcasted_iota(I32, k.shape, 0)) < valid_len
            ld, kkr, k, v = (jnp.where(live, x, 0.0) for x in (ld, kkr, k, v))
        states = [state_ref[h] for h in range(C_HEADS)]
        ys, new_states = _wkv_chunk(r, ld, k, v, kkr, a, states, chunk)
        for h in range(C_HEADS):
            state_ref[h] = new_states[h]
        rkr = r * k * r_k
        outs = []
        for h in range(C_HEADS):
            y = ys[h]
            mean = jnp.mean(y, axis=-1, keepdims=True)
            var = jnp.mean(jnp.square(y - mean), axis=-1, keepdims=True)
            yn = (y - mean) * lax.rsqrt(var + GN_EPS)
            bonus = jnp.sum(hs(rkr, h), axis=-1, keepdims=True) * hs(v, h)
            outs.append(yn * hs(ln_w, h) + hs(ln_b, h) + bonus)
        o_ref[rows, :] = (jnp.concatenate(outs, axis=1) * g).astype(o_ref.dtype)
        return pc[chunk - 1:chunk, :]

    last_ref[...] = lax.fori_loop(0, tb // chunk, chunk_step, last_ref[...])

    @pl.when(t_blk == pl.num_programs(1) - 1)
    def _():
        sfin_ref[...] = state_ref[...]


def wkv7(p, prev, mu, vecs, w2, a2, g2, s0, chunk, tb, valid_len):
    b, s, _ = p.shape
    const = lambda shape: pl.BlockSpec(shape, lambda bi, i: (0,) * len(shape))
    st = pl.BlockSpec((None, C_HEADS, C_HD, C_HD), lambda bi, i: (bi, 0, 0, 0))
    return pl.pallas_call(
        functools.partial(_wkv_kernel, chunk=chunk, valid_len=valid_len if valid_len < s else None),
        grid=(b, s // tb),
        in_specs=[pl.BlockSpec((None, tb, C_BLOCK), lambda bi, i: (bi, i, 0)),
                  pl.BlockSpec((None, 1, C_BLOCK), lambda bi, i: (bi, 0, 0)),
                  const(mu.shape), const(vecs.shape), const(w2.shape), const(a2.shape), const(g2.shape), st],
        out_specs=[pl.BlockSpec((None, tb, C_WIDTH), lambda bi, i: (bi, i, 0)), st],
        out_shape=[jax.ShapeDtypeStruct((b, s, C_WIDTH), BF16),
                   jax.ShapeDtypeStruct((b, C_HEADS, C_HD, C_HD), F32)],
        scratch_shapes=[pltpu.VMEM((C_HEADS, C_HD, C_HD), F32), pltpu.VMEM((1, C_BLOCK), F32)],
        compiler_params=_cparams("parallel", "arbitrary"),
    )(p, prev, mu, vecs, w2, a2, g2, s0)


def _split2(x):
    hi = x.astype(BF16)
    return hi, (x - hi.astype(F32)).astype(BF16)


def _page_specs(rows, width, layer, n_pages, pg):
    assert n_pages % pg == 0
    last_group = n_pages // pg - 1

    def spec(slot):
        def index_map(b, j, pt):
            return (layer, pt[b, jnp.minimum(j, last_group) * pg + slot], 0, 0)
        return pl.BlockSpec((None, None, rows, width), index_map)
    return [spec(slot) for slot in range(pg)]


def _idx_sample_kernel(pt_ref, qi_ref, w_ref, knew_ref, *rest, t):
    pages, o_ref = rest[:-1], rest[-1]
    j = pl.program_id(1)
    last = pl.num_programs(1) - 1
    q_hi, q_lo = _split2(qi_ref[...])
    w = w_ref[...]

    def scores(kpages):
        splits = [_split2(kp) for kp in kpages]
        s = [_dot(q_hi, k_hi) + _dot(q_hi, k_lo) + _dot(q_lo, k_hi) for k_hi, k_lo in splits]
        s = [jnp.maximum(x, 0.0) * w for x in s]
        return [jnp.sum(x.reshape(t, IDX_HEADS, PAGE_SIZE), axis=1) for x in s]

    @pl.when(j < last)
    def _():
        o_ref[...] = jnp.concatenate(scores([page[...] for page in pages]), axis=1)

    @pl.when(j == last)
    def _():
        o_ref[...] = jnp.zeros(o_ref.shape, F32)
        o_ref[:, 0:PAGE_SIZE] = scores([knew_ref[...]])[0]


def idx_sample(page_table, qi, w, knew, cache_idx, layer):
    db, rows, _ = qi.shape
    t = rows // IDX_HEADS
    n_pages = page_table.shape[1]
    pg = min(IDX_PAGES_PER_STEP, n_pages)
    steps = n_pages // pg + 1
    per_b = lambda b, j, pt: (b, 0, 0)
    return pl.pallas_call(
        functools.partial(_idx_sample_kernel, t=t),
        grid_spec=pltpu.PrefetchScalarGridSpec(
            num_scalar_prefetch=1,
            grid=(db, steps),
            in_specs=[pl.BlockSpec((None, rows, IDX_HD), per_b),
                      pl.BlockSpec((None, rows, 1), per_b),
                      pl.BlockSpec((None, IDX_HD, PAGE_SIZE), per_b)]
                     + _page_specs(IDX_HD, PAGE_SIZE, layer, n_pages, pg),
            out_specs=pl.BlockSpec((None, t, pg * PAGE_SIZE), lambda b, j, pt: (b, 0, j))),
        out_shape=jax.ShapeDtypeStruct((db, t, steps * pg * PAGE_SIZE), F32),
        compiler_params=_cparams("parallel", "arbitrary"),
    )(page_table, qi, w, knew, *([cache_idx] * pg))


def _select_sample_kernel(isc_ref, o_ref, *, topk, t, past_len):
    rows, n = isc_ref.shape
    pos = lax.broadcasted_iota(I32, (rows, n), 1)
    r = pl.program_id(0) * rows + lax.broadcasted_iota(I32, (rows, n), 0)
    causal = pos <= past_len + (r & (t - 1))
    mask = _topk_mask(jnp.where(causal, isc_ref[...], NEG_INF), pos, topk, n.bit_length()) & causal
    o_ref[...] = jnp.where(mask, 1.0, 0.0)


def select_sample(isc, topk, t, past_len):
    rows, n = isc.shape
    tr = 16
    assert t & (t - 1) == 0 and rows % tr == 0
    return pl.pallas_call(
        functools.partial(_select_sample_kernel, topk=topk, t=t, past_len=past_len),
        grid=(rows // tr,),
        in_specs=[pl.BlockSpec((tr, n), lambda i: (i, 0))],
        out_specs=pl.BlockSpec((tr, n), lambda i: (i, 0)),
        out_shape=jax.ShapeDtypeStruct((rows, n), F32),
        compiler_params=_cparams("parallel"),
    )(isc)


def _own_head(shape, heads):
    return ((lax.broadcasted_iota(I32, shape, 1) & (heads - 1))
            == (lax.broadcasted_iota(I32, shape, 0) & (heads - 1)))


def _dsa_sample_kernel(pt_ref, q_ref, mask_ref, knew_ref, vnew_ref, *rest, heads, pg):
    kpages, vpages = rest[:pg], rest[pg:2 * pg]
    o_ref, m_ref, l_ref, acc_ref = rest[2 * pg:]
    j = pl.program_id(1)
    last = pl.num_programs(1) - 1
    q = q_ref[...]
    page_rows = PAGE_SIZE * heads
    shift = heads.bit_length() - 1
    spread = jnp.where(
        jnp.right_shift(lax.broadcasted_iota(I32, (PAGE_SIZE, page_rows), 1), shift)
        == lax.broadcasted_iota(I32, (PAGE_SIZE, page_rows), 0), 1.0, 0.0).astype(BF16)

    @pl.when(j == 0)
    def _():
        m_ref[...] = jnp.full(m_ref.shape, NEG_INF, F32)
        l_ref[...] = jnp.zeros(l_ref.shape, F32)
        acc_ref[...] = jnp.zeros(acc_ref.shape, F32)

    def update(ks, vs, mask):
        s = jnp.concatenate([_dot_nt(q, kp.astype(BF16)) for kp in ks], axis=1) * (A_HD ** -0.5)
        sel = jnp.concatenate([_dot(mask[:, i * PAGE_SIZE:(i + 1) * PAGE_SIZE].astype(BF16), spread)
                               for i in range(len(ks))], axis=1)
        s = jnp.where((sel > 0.5) & _own_head(s.shape, heads), s, NEG_INF)
        m_old = m_ref[...]
        m_new = jnp.maximum(m_old, jnp.max(s, axis=-1, keepdims=True))
        m_safe = jnp.where(m_new == NEG_INF, 0.0, m_new)
        p = jnp.exp(s - m_safe)
        alpha = jnp.exp(m_old - m_safe)
        l_ref[...] = alpha * l_ref[...] + jnp.sum(p, axis=-1, keepdims=True)
        acc = alpha * acc_ref[...]
        for i, vp in enumerate(vs):
            acc += _dot(p[:, i * page_rows:(i + 1) * page_rows].astype(BF16), vp.astype(BF16))
        acc_ref[...] = acc
        m_ref[...] = m_new

    @pl.when(j < last)
    def _():
        update([kp[...] for kp in kpages], [vp[...] for vp in vpages], mask_ref[...])

    @pl.when(j == last)
    def _():
        update([knew_ref[...]], [vnew_ref[...]], mask_ref[:, 0:PAGE_SIZE])
        o_ref[...] = acc_ref[...] / l_ref[...]


def dsa_sample(page_table, q, mask, knew, vnew, cache_k, cache_v, layer, heads):
    db, rows, hd = q.shape
    n_pages = page_table.shape[1]
    pg = min(SAMPLE_PAGES_PER_STEP, n_pages)
    steps = n_pages // pg + 1
    page_rows = PAGE_SIZE * heads
    per_b = lambda b, j, pt: (b, 0, 0)
    return pl.pallas_call(
        functools.partial(_dsa_sample_kernel, heads=heads, pg=pg),
        grid_spec=pltpu.PrefetchScalarGridSpec(
            num_scalar_prefetch=1,
            grid=(db, steps),
            in_specs=[pl.BlockSpec((None, rows, hd), per_b),
                      pl.BlockSpec((None, rows, pg * PAGE_SIZE), lambda b, j, pt: (b, 0, j)),
                      pl.BlockSpec((None, page_rows, hd), per_b),
                      pl.BlockSpec((None, page_rows, hd), per_b)]
                     + _page_specs(page_rows, hd, layer, n_pages, pg) + _page_specs(page_rows, hd, layer, n_pages, pg),
            out_specs=pl.BlockSpec((None, rows, hd), per_b),
            scratch_shapes=[pltpu.VMEM((rows, 1), F32), pltpu.VMEM((rows, 1), F32),
                            pltpu.VMEM((rows, hd), F32)]),
        out_shape=jax.ShapeDtypeStruct((db, rows, hd), F32),
        compiler_params=_cparams("parallel", "arbitrary"),
    )(page_table, q, mask, knew, vnew, *([cache_k] * pg), *([cache_v] * pg))


def _moba_sample_kernel(pt_ref, q_ref, knew_ref, vnew_ref, *rest, heads, ksel, pg):
    ppb = MOBA_BLOCK // PAGE_SIZE
    kpages, vpages = rest[:pg], rest[pg:2 * pg]
    o_ref, g_ref, m_ref, l_ref, ob_ref = rest[2 * pg:]
    j = pl.program_id(1)
    last = pl.num_programs(1) - 1
    q = q_ref[...]
    q16 = q.astype(BF16)
    rows, hd = q.shape
    scale = B_HD ** -0.5
    sublanes = 8
    assert 2 * heads == sublanes and hd == LANES and rows % sublanes == 0
    wide = lambda x: jnp.broadcast_to(x, (rows, LANES))

    def partial_softmax(s, v16):
        m = jnp.max(s, axis=-1, keepdims=True)
        p = jnp.exp(s - m)
        return m, jnp.sum(p, axis=-1, keepdims=True), _dot(p.astype(BF16), v16)

    @pl.when(j < last)
    def _():
        blocks = range(pg // ppb)
        kblk = [jnp.concatenate([kpages[b * ppb + i][...] for i in range(ppb)], axis=0) for b in blocks]
        v16 = [jnp.concatenate([vpages[b * ppb + i][...] for i in range(ppb)], axis=0).astype(BF16)
               for b in blocks]
        s = [_dot_nt(q16, kblk[b].astype(BF16)) * scale for b in blocks]
        own = _own_head(s[0].shape, heads)
        s = [jnp.where(own, s[b], NEG_INF) for b in blocks]
        m = [jnp.max(s[b], axis=-1, keepdims=True) for b in blocks]
        p = [jnp.exp(s[b] - m[b]) for b in blocks]
        l = [jnp.sum(p[b], axis=-1, keepdims=True) for b in blocks]
        o = [_dot(p[b].astype(BF16), v16[b]) for b in blocks]
        fold = [jnp.sum(kblk[b].reshape(kblk[b].shape[0] // sublanes, sublanes, hd), axis=0) for b in blocks]
        kmean = [(fold[b] + pltpu.roll(fold[b], heads, 0)) * (1.0 / MOBA_BLOCK) for b in blocks]
        gate = [jnp.sum(q * jnp.concatenate([kmean[b]] * (rows // sublanes), axis=0), axis=-1, keepdims=True)
                for b in blocks]
        for b in blocks:
            n = j * (pg // ppb) + b
            g_ref[n] = wide(gate[b])
            m_ref[n] = wide(m[b])
            l_ref[n] = wide(l[b])
            ob_ref[n] = o[b]

    @pl.when(j == last)
    def _():
        shift = heads.bit_length() - 1
        s = _dot_nt(q16, knew_ref[...].astype(BF16)) * scale
        causal = (jnp.right_shift(lax.broadcasted_iota(I32, s.shape, 1), shift)
                  <= jnp.right_shift(lax.broadcasted_iota(I32, s.shape, 0), shift))
        s = jnp.where(causal & _own_head(s.shape, heads), s, NEG_INF)
        m_own, l_own, o_own = partial_softmax(s, vnew_ref[...].astype(BF16))
        m_own, l_own = wide(m_own), wide(l_own)
        gates = g_ref[...]
        nidx = lax.broadcasted_iota(I32, gates.shape, 0)
        chosen = jnp.zeros(gates.shape, F32)
        for _ in range(ksel):
            best = jnp.max(gates, axis=0, keepdims=True)
            first = jnp.min(jnp.where(gates == best, nidx, gates.shape[0]), axis=0, keepdims=True)
            pick = nidx == first
            chosen = jnp.where(pick, 1.0, chosen)
            gates = jnp.where(pick, NEG_INF, gates)
        m_blk = m_ref[...]
        m_tot = jnp.maximum(m_own, jnp.max(jnp.where(chosen > 0.5, m_blk, NEG_INF), axis=0))
        wgt = jnp.where(chosen > 0.5, jnp.exp(m_blk - m_tot[None]), 0.0)
        w_own = jnp.exp(m_own - m_tot)
        l_tot = w_own * l_own + jnp.sum(wgt * l_ref[...], axis=0)
        acc = w_own * o_own
        for n in range(gates.shape[0]):
            acc += wgt[n] * ob_ref[n]
        o_ref[...] = acc / l_tot


def moba_sample(page_table, q, knew, vnew, cache_k, cache_v, layer, heads):
    db, rows, hd = q.shape
    n_pages = page_table.shape[1]
    pg = min(SAMPLE_PAGES_PER_STEP, n_pages)
    steps = n_pages // pg + 1
    nbp = n_pages * PAGE_SIZE // MOBA_BLOCK
    page_rows = PAGE_SIZE * heads
    per_b = lambda b, j, pt: (b, 0, 0)
    return pl.pallas_call(
        functools.partial(_moba_sample_kernel, heads=heads, ksel=max(1, min(MOBA_TOPK, nbp)), pg=pg),
        grid_spec=pltpu.PrefetchScalarGridSpec(
            num_scalar_prefetch=1,
            grid=(db, steps),
            in_specs=[pl.BlockSpec((None, rows, hd), per_b),
                      pl.BlockSpec((None, page_rows, hd), per_b),
                      pl.BlockSpec((None, page_rows, hd), per_b)]
                     + _page_specs(page_rows, hd, layer, n_pages, pg) + _page_specs(page_rows, hd, layer, n_pages, pg),
            out_specs=pl.BlockSpec((None, rows, hd), per_b),
            scratch_shapes=[pltpu.VMEM((nbp, rows, LANES), F32), pltpu.VMEM((nbp, rows, LANES), F32),
                            pltpu.VMEM((nbp, rows, LANES), F32), pltpu.VMEM((nbp, rows, hd), F32)]),
        out_shape=jax.ShapeDtypeStruct((db, rows, hd), F32),
        compiler_params=_cparams("parallel", "arbitrary"),
    )(page_table, q, knew, vnew, *([cache_k] * pg), *([cache_v] * pg))


def _new_page(x, rows):
    return jnp.pad(x, ((0, 0), (0, rows - x.shape[1]), (0, 0)))


def _sample_attention(qa, ka, va, qi, kiwi, qb, kb, vb, caches, page_table, layer, db, t):
    cache_a_k, cache_a_v, cache_a_idx, cache_b_k, cache_b_v = caches
    past_len = page_table.shape[1] * PAGE_SIZE
    topk = min(A_TOPK_MAX, (past_len + t) // 4)
    ki = kiwi[:, :IDX_HD].reshape(db, t, IDX_HD)
    wi = kiwi[:, IDX_HD:IDX_HD + IDX_HEADS]
    knew = _new_page(ki, PAGE_SIZE).transpose(0, 2, 1)
    isc = idx_sample(page_table, qi.reshape(db, t * IDX_HEADS, IDX_HD), wi.reshape(db, t * IDX_HEADS, 1),
                     knew, cache_a_idx, layer)
    n = isc.shape[-1]
    mask = select_sample(isc.reshape(db * t, n), topk, t, past_len).reshape(db, t, n)
    rows_a = lambda z: z.reshape(db, t * A_HEADS, A_HD)
    rows_b = lambda z: z.reshape(db, t * B_HEADS, B_HD)
    oa = dsa_sample(page_table, rows_a(qa).astype(BF16), jnp.repeat(mask, A_HEADS, axis=1),
                    _new_page(rows_a(ka), PAGE_SIZE * A_HEADS), _new_page(rows_a(va), PAGE_SIZE * A_HEADS),
                    cache_a_k, cache_a_v, layer, A_HEADS)
    ob = moba_sample(page_table, rows_b(qb), _new_page(rows_b(kb), PAGE_SIZE * B_HEADS),
                     _new_page(rows_b(vb), PAGE_SIZE * B_HEADS), cache_b_k, cache_b_v, layer, B_HEADS)
    return oa.reshape(db * t, A_WIDTH), ob.reshape(db * t, B_WIDTH)


def _pad_cols(x, width):
    return jnp.pad(x, [(0, 0)] * (x.ndim - 1) + [(0, width - x.shape[-1])])


def _layer_params(l, norm1_g, w_in, a_q_norm, a_k_norm, idx_k_norm, b_q_norm, b_k_norm, c_mu, c_w0, c_w2, c_a0,
                  c_a2, c_g2, c_k_k, c_k_a, c_r_k, c_ln_w, c_ln_b, w_o, norm2_g, w_ff1, w_ff2):
    w = w_in[l]
    a_cols, b_cols, c_cols = w[:, :A_COLS], w[:, A_COLS:A_COLS + B_COLS], w[:, A_COLS + B_COLS:]
    w_pack = jnp.concatenate([_pad_cols(c_cols, A_OFF), _pad_cols(a_cols, B_OFF - A_OFF), b_cols],
                             axis=1).astype(BF16)
    lora_rows = lambda x, off: jnp.pad(x, ((off, LORA_BLOCK - off - x.shape[0]), (0, 0))).astype(BF16)
    vec = dict(c_w0=c_w0[l], c_a0=c_a0[l], c_k_k=c_k_k[l], c_k_a=c_k_a[l], c_ln_w=c_ln_w[l], c_ln_b=c_ln_b[l],
               c_r_k=c_r_k[l].reshape(-1))
    vecs = jnp.stack([vec[name] for name in WKV_VECS] + [jnp.zeros((C_WIDTH,), F32)])
    gains = jnp.stack([a_q_norm[l], a_k_norm[l], b_q_norm[l], b_k_norm[l], _pad_cols(idx_k_norm[l], LANES)]
                      + [jnp.zeros((LANES,), F32)] * 3)
    return dict(norm1_g=norm1_g[l], w_in=w_pack, gains=gains, mu=_pad_cols(c_mu[l], C_BLOCK)[None], vecs=vecs,
                w2=lora_rows(c_w2[l], 0), a2=lora_rows(c_a2[l], W_LORA), g2=lora_rows(c_g2[l], W_LORA + A_LORA),
                w_o=w_o[l].astype(BF16), norm2_g=norm2_g[l], w_ff1=w_ff1[l].astype(BF16),
                w_ff2=w_ff2[l].astype(BF16))


def _layer(x, tabs, prev_shift, wkv0, lw, sample_ctx):
    b, s, d = x.shape
    m = b * s
    p = norm_matmul(x.reshape(m, d), lw['norm1_g'], lw['w_in'], min(m, 1024), PROJ_TILE_N)
    qa, ka, va, qi, kiwi, qb, kb, vb, ka_rows, va_rows, kb_rows, vb_rows = prep(
        p, tabs[0], tabs[1], lw['gains'], min(m, 256))
    p3 = p.reshape(b, s, P_PACK)
    if sample_ctx is None:
        seq = lambda z: z.reshape(b, s, z.shape[-1])
        oa = dsa_prompt(seq(qa), seq(ka), seq(va), seq(qi), seq(kiwi)).reshape(m, A_WIDTH)
        ob = moba_prompt(seq(qb), seq(kb), seq(vb)).reshape(m, B_WIDTH)
        p_scan = p3
    else:
        caches, page_table, layer = sample_ctx
        oa, ob = _sample_attention(qa, ka_rows, va_rows, qi, kiwi, qb, kb_rows, vb_rows, caches, page_table,
                                   layer, b, s)
        oa, ob = oa.astype(BF16), ob.astype(BF16)
        p_scan = jnp.pad(p3, ((0, 0), (0, -s % 16), (0, 0)))
    sp = p_scan.shape[1]
    chunk = min(WKV_CHUNK, sp)
    oc, wkv_fin = wkv7(p_scan, _pad_cols(prev_shift, C_BLOCK)[:, None, :], lw['mu'], lw['vecs'], lw['w2'],
                       lw['a2'], lw['g2'], wkv0, chunk, min(sp, 4 * chunk), s)
    tm2 = min(m, 512)
    x2 = out_proj(x.reshape(m, d), oa, ob, oc[:, :s].reshape(m, C_WIDTH), lw['w_o'], tm2)
    x3 = mlp(x2, lw['norm2_g'], lw['w_ff1'], lw['w_ff2'], tm2, min(MLP_TILE_F, lw['w_ff1'].shape[1])).reshape(b, s, d)
    rows = (ka_rows.reshape(b, s, A_HEADS, A_HD), va_rows.reshape(b, s, A_HEADS, A_HD),
            kiwi[:, :IDX_HD].reshape(b, s, IDX_HD), kb_rows.reshape(b, s, B_HEADS, B_HD),
            vb_rows.reshape(b, s, B_HEADS, B_HD), wkv_fin, p3[:, s - 1, :C_COLS])
    return x3, rows


def kernel(x_prompt, x_sample, cache_a_k, cache_a_v, cache_a_idx, cache_b_k, cache_b_v, state_wkv, state_shift, page_table, norm1_g, w_in, a_q_norm, a_k_norm, idx_k_norm, b_q_norm, b_k_norm, c_mu, c_w0, c_w2, c_a0, c_a2, c_g2, c_k_k, c_k_a, c_r_k, c_ln_w, c_ln_b, w_o, norm2_g, w_ff1, w_ff2):
    depth = w_in.shape[0]
    bp, sp = x_prompt.shape[:2]
    db, t = x_sample.shape[:2]
    pos_p = jnp.arange(sp)
    pos_s = jnp.tile(page_table.shape[1] * PAGE_SIZE + jnp.arange(t), db)
    tabs_p = (_rope_tables(pos_p, A_HD, 1), _rope_tables(pos_p, IDX_HD, LANES // IDX_HD))
    tabs_s = (_rope_tables(pos_s, A_HD, 1), _rope_tables(pos_s, IDX_HD, LANES // IDX_HD))
    shift0 = jnp.zeros((bp, C_COLS), x_prompt.dtype)
    wkv0 = jnp.zeros((bp, C_HEADS, C_HD, C_HD), F32)
    xp, xs = x_prompt, x_sample
    by_row = lambda c: c.reshape(c.shape[0], c.shape[1], -1, c.shape[-1])
    caches = (by_row(cache_a_k), by_row(cache_a_v), cache_a_idx.transpose(0, 1, 3, 2),
              by_row(cache_b_k), by_row(cache_b_v))
    p_rows, s_rows = [], []
    for l in range(depth):
        lw = _layer_params(l, norm1_g, w_in, a_q_norm, a_k_norm, idx_k_norm, b_q_norm, b_k_norm, c_mu, c_w0, c_w2,
                           c_a0, c_a2, c_g2, c_k_k, c_k_a, c_r_k, c_ln_w, c_ln_b, w_o, norm2_g, w_ff1, w_ff2)
        xp, rows = _layer(xp, tabs_p, shift0, wkv0, lw, None)
        p_rows.append(rows)
        xs, rows = _layer(xs, tabs_s, state_shift[l], state_wkv[l], lw, (caches, page_table, l))
        s_rows.append(rows)
    p_out = [jnp.stack(z) for z in zip(*p_rows)]
    s_out = [jnp.stack(z) for z in zip(*s_rows)]
    return (xp, xs, *p_out, *s_out)
```

```python
import functools

import jax
import jax.numpy as jnp
from jax import lax
from jax.experimental import pallas as pl
from jax.experimental.pallas import tpu as pltpu

F32 = jnp.float32
BF16 = jnp.bfloat16
I32 = jnp.int32

VMEM_LIMIT_BYTES = 56 * 1024 * 1024
LANES = 128

PAGE_SIZE = 128
A_HEADS = 4
A_HD = 128
A_WIDTH = A_HEADS * A_HD
IDX_HEADS = 16
IDX_HD = 64
A_TOPK_MAX = 256
A_Q_BLOCK = 128
B_HEADS = 4
B_HD = 128
B_WIDTH = B_HEADS * B_HD
MOBA_BLOCK = 256
MOBA_TOPK = 3
C_HD = 64
C_HEADS = 16
C_WIDTH = C_HEADS * C_HD
W_LORA = 64
A_LORA = 64
G_LORA = 160
A_COLS = 3 * A_WIDTH + IDX_HEADS * IDX_HD + IDX_HD + IDX_HEADS
B_COLS = 3 * B_WIDTH
C_COLS = 3 * C_WIDTH + W_LORA + A_LORA + G_LORA
ROPE_THETA = 500000.0
NORM_EPS = 1e-6
GN_EPS = 64e-5
IDX_W_SCALE = (IDX_HEADS * IDX_HD) ** -0.5
WKV_CHUNK = 64
SAMPLE_PAGES_PER_STEP = 16
IDX_PAGES_PER_STEP = 64
DSA_KEY_BUCKET = 256
DSA_QUERY_TILE = 256
PROJ_TILE_N = 1536
MLP_TILE_F = 1024

C_BLOCK = C_COLS + (-C_COLS % LANES)
A_OFF = C_BLOCK
B_OFF = A_OFF + A_COLS + (-A_COLS % LANES)
P_PACK = B_OFF + B_COLS
assert P_PACK % PROJ_TILE_N == 0
LORA_OFF = 3 * C_WIDTH
LORA_BLOCK = C_BLOCK - LORA_OFF
A_QI_OFF = A_OFF + 3 * A_WIDTH
A_KI_OFF = A_QI_OFF + IDX_HEADS * IDX_HD
assert A_KI_OFF + LANES == B_OFF and IDX_HD + IDX_HEADS <= LANES

NEG_INF = float("-inf")
INT_MIN = -2 ** 31


def _cparams(*sem):
    return pltpu.CompilerParams(dimension_semantics=sem, vmem_limit_bytes=VMEM_LIMIT_BYTES)


def _dot_nt(a, b):
    return lax.dot_general(a, b, (((1,), (1,)), ((), ())), preferred_element_type=F32)


def _dot(a, b):
    return jnp.dot(a, b, preferred_element_type=F32)


def _norm_matmul_kernel(x_ref, g_ref, w_ref, o_ref, xn_ref):
    @pl.when(pl.program_id(1) == 0)
    def _():
        x = x_ref[...]
        y = x * lax.rsqrt(jnp.mean(x * x, axis=-1, keepdims=True) + NORM_EPS)
        xn_ref[...] = (y * g_ref[...]).astype(BF16)

    o_ref[...] = _dot(xn_ref[...], w_ref[...])


def norm_matmul(x, g, w, tm, tn):
    m, d = x.shape
    p = w.shape[1]
    return pl.pallas_call(
        _norm_matmul_kernel,
        grid=(m // tm, p // tn),
        in_specs=[pl.BlockSpec((tm, d), lambda i, j: (i, 0)),
                  pl.BlockSpec((1, d), lambda i, j: (0, 0)),
                  pl.BlockSpec((d, tn), lambda i, j: (0, j))],
        out_specs=pl.BlockSpec((tm, tn), lambda i, j: (i, j)),
        out_shape=jax.ShapeDtypeStruct((m, p), F32),
        scratch_shapes=[pltpu.VMEM((tm, d), BF16)],
        compiler_params=_cparams("parallel", "arbitrary"),
    )(x, g.reshape(1, d), w)


def _out_proj_kernel(x_ref, oa_ref, ob_ref, oc_ref, w_ref, o_ref):
    wa = oa_ref.shape[1]
    wb = ob_ref.shape[1]
    acc = _dot(oa_ref[...], w_ref[0:wa, :])
    acc += _dot(ob_ref[...], w_ref[wa:wa + wb, :])
    acc += _dot(oc_ref[...], w_ref[wa + wb:, :])
    o_ref[...] = x_ref[...] + acc


def out_proj(x, oa, ob, oc, w, tm):
    m, d = x.shape
    row = lambda i: (i, 0)
    return pl.pallas_call(
        _out_proj_kernel,
        grid=(m // tm,),
        in_specs=[pl.BlockSpec((tm, d), row),
                  pl.BlockSpec((tm, oa.shape[1]), row),
                  pl.BlockSpec((tm, ob.shape[1]), row),
                  pl.BlockSpec((tm, oc.shape[1]), row),
                  pl.BlockSpec(w.shape, lambda i: (0, 0))],
        out_specs=pl.BlockSpec((tm, d), row),
        out_shape=jax.ShapeDtypeStruct((m, d), F32),
        compiler_params=_cparams("parallel"),
    )(x, oa, ob, oc, w)


def _mlp_kernel(x_ref, g_ref, w1_ref, w2_ref, o_ref, xn_ref):
    @pl.when(pl.program_id(1) == 0)
    def _():
        x = x_ref[...]
        y = x * lax.rsqrt(jnp.mean(x * x, axis=-1, keepdims=True) + NORM_EPS)
        xn_ref[...] = (y * g_ref[...]).astype(BF16)
        o_ref[...] = x

    h = jnp.maximum(_dot(xn_ref[...], w1_ref[...]), 0.0)
    o_ref[...] += _dot((h * h).astype(BF16), w2_ref[...])


def mlp(x, g, w1, w2, tm, tf):
    m, d = x.shape
    f = w1.shape[1]
    return pl.pallas_call(
        _mlp_kernel,
        grid=(m // tm, f // tf),
        in_specs=[pl.BlockSpec((tm, d), lambda i, j: (i, 0)),
                  pl.BlockSpec((1, d), lambda i, j: (0, 0)),
                  pl.BlockSpec((d, tf), lambda i, j: (0, j)),
                  pl.BlockSpec((tf, d), lambda i, j: (j, 0))],
        out_specs=pl.BlockSpec((tm, d), lambda i, j: (i, 0)),
        out_shape=jax.ShapeDtypeStruct((m, d), F32),
        scratch_shapes=[pltpu.VMEM((tm, d), BF16)],
        compiler_params=_cparams("parallel", "arbitrary"),
    )(x, g.reshape(1, d), w1, w2)


def _rope_tables(pos, hd, reps):
    rot = hd // 4
    half = rot // 2
    inv = ROPE_THETA ** (-2.0 * jnp.arange(half, dtype=F32) / rot)
    ang = pos.astype(F32)[:, None] * inv[None, :]
    cos, sin = jnp.cos(ang), jnp.sin(ang)
    zeros = jnp.zeros_like(sin)
    rest = jnp.zeros((pos.shape[0], hd - rot), F32)
    tabs = [jnp.concatenate([cos, cos, rest + 1.0], axis=1),
            jnp.concatenate([zeros, sin, rest], axis=1),
            jnp.concatenate([-sin, zeros, rest], axis=1)]
    return jnp.stack([jnp.tile(z, (1, reps)) for z in tabs])


def _rope(y, tab_ref, half):
    width = y.shape[1]
    return (y * tab_ref[0] + pltpu.roll(y, half, 1) * tab_ref[1]
            + pltpu.roll(y, width - half, 1) * tab_ref[2])


def _prep_kernel(p_ref, ta_ref, ti_ref, g_ref, qa_ref, ka_ref, va_ref, qi_ref, kiwi_ref, qb_ref, kb_ref, vb_ref,
                 ka_rows_ref, va_rows_ref, kb_rows_ref, vb_rows_ref):
    tm = p_ref.shape[0]

    def normed(x, g):
        return x * lax.rsqrt(jnp.mean(x * x, axis=-1, keepdims=True) + NORM_EPS) * g

    for base, q_ref, k_ref, v_ref, k_rows_ref, v_rows_ref, heads, hd, gq, gk in (
            (A_OFF, qa_ref, ka_ref, va_ref, ka_rows_ref, va_rows_ref, A_HEADS, A_HD, 0, 1),
            (B_OFF, qb_ref, kb_ref, vb_ref, kb_rows_ref, vb_rows_ref, B_HEADS, B_HD, 2, 3)):
        width = heads * hd
        for h in range(heads):
            sl = slice(h * hd, (h + 1) * hd)
            q_ref[:, sl] = _rope(normed(p_ref[:, base + h * hd:base + (h + 1) * hd], g_ref[gq:gq + 1, :]),
                                 ta_ref, hd // 8)
            k = _rope(normed(p_ref[:, base + width + h * hd:base + width + (h + 1) * hd],
                             g_ref[gk:gk + 1, :]), ta_ref, hd // 8)
            v = p_ref[:, base + 2 * width + h * hd:base + 2 * width + (h + 1) * hd]
            k_ref[:, sl] = k
            v_ref[:, sl] = v
            k_rows_ref[pl.ds(h, tm, stride=heads), :] = k
            v_rows_ref[pl.ds(h, tm, stride=heads), :] = v
    for j in range(IDX_HEADS * IDX_HD // LANES):
        sl = slice(j * LANES, (j + 1) * LANES)
        qi_ref[:, sl] = _rope(p_ref[:, A_QI_OFF + j * LANES:A_QI_OFF + (j + 1) * LANES], ti_ref, IDX_HD // 8)
    x = p_ref[:, A_KI_OFF:A_KI_OFF + LANES]
    is_key = lax.broadcasted_iota(I32, x.shape, 1) < IDX_HD
    ms = jnp.sum(jnp.where(is_key, x * x, 0.0), axis=-1, keepdims=True) * (1.0 / IDX_HD)
    ki = _rope(x * lax.rsqrt(ms + NORM_EPS) * g_ref[4:5, :], ti_ref, IDX_HD // 8)
    kiwi_ref[...] = jnp.where(is_key, ki, x * IDX_W_SCALE)


def prep(p, tab_a, tab_i, gains, tm):
    m = p.shape[0]
    period = tab_a.shape[1] // tm
    row = lambda i: (i, 0)
    tab = pl.BlockSpec((3, tm, LANES), lambda i: (0, i % period, 0))
    widths = (A_WIDTH, A_WIDTH, A_WIDTH, IDX_HEADS * IDX_HD, LANES, B_WIDTH, B_WIDTH, B_WIDTH)
    by_head = ((A_HEADS, A_HD), (A_HEADS, A_HD), (B_HEADS, B_HD), (B_HEADS, B_HD))
    return pl.pallas_call(
        _prep_kernel,
        grid=(m // tm,),
        in_specs=[pl.BlockSpec((tm, P_PACK), row), tab, tab, pl.BlockSpec(gains.shape, lambda i: (0, 0))],
        out_specs=[pl.BlockSpec((tm, w), row) for w in widths]
                  + [pl.BlockSpec((tm * h, hd), row) for h, hd in by_head],
        out_shape=[jax.ShapeDtypeStruct((m, w), F32) for w in widths]
                  + [jax.ShapeDtypeStruct((m * h, hd), F32) for h, hd in by_head],
        compiler_params=_cparams("parallel"),
    )(p, tab_a, tab_i, gains)


def _sortable_key(x):
    bits = pltpu.bitcast(x, I32)
    return jnp.where(bits >= 0, bits, bits ^ jnp.int32(0x7FFFFFFF))


def _topk_mask(score, pos, topk, n_pos_bits):
    rows, n = score.shape
    key = _sortable_key(score)

    def count(m):
        return jnp.sum(jnp.where(m, 1.0, 0.0), axis=-1, keepdims=True).astype(I32)

    nbits = 32
    start = jnp.zeros((rows, 1), I32)

    groups = 2 if rows % 16 == 0 else 1
    step = rows // groups
    parts = [key[g * step:(g + 1) * step] for g in range(groups)]

    def value_step(it, tus):
        bit = jnp.left_shift(jnp.int32(1), nbits - 1 - it)
        out = []
        for part, tu in zip(parts, tus):
            cand_u = tu | bit
            cnt = count(part >= (cand_u ^ jnp.int32(INT_MIN)))
            out.append(jnp.where(cnt >= topk, cand_u, tu))
        return tuple(out)

    tus = lax.fori_loop(0, nbits, value_step, tuple(start[g * step:(g + 1) * step] for g in range(groups)))
    thr = jnp.concatenate(tus, axis=0) ^ jnp.int32(INT_MIN)
    gt = key > thr
    eq = key == thr
    need = topk - count(gt)
    excess = count(eq) - need

    def tie_search(_):
        def pos_step(it, x):
            cand = x | jnp.left_shift(jnp.int32(1), n_pos_bits - 1 - it)
            cnt = count(eq & (pos < cand))
            return jnp.where(cnt < need, cand, x)
        return lax.fori_loop(0, n_pos_bits, pos_step, jnp.zeros((rows, 1), I32))

    last = lax.cond(jnp.max(excess) > 0, tie_search,
                    lambda _: jnp.full((rows, 1), 2 ** n_pos_bits, I32), 0)
    return gt | (eq & (pos <= last))


def _masked_attention(q, k, v, mask, scale):
    s = _dot_nt(q, k) * scale
    s = jnp.where(mask, s, NEG_INF)
    m = jnp.max(s, axis=-1, keepdims=True)
    p = jnp.exp(s - m)
    l = jnp.sum(p, axis=-1, keepdims=True)
    return _dot(p.astype(BF16), v) / l


def _dsa_prompt_kernel(q_ref, k_ref, v_ref, qi_ref, kiwi_ref, kiwi_blk_ref, o_ref, k16_ref, v16_ref, ki16_ref,
                       *, topk, bucket):
    tq = q_ref.shape[0]
    s_len = k_ref.shape[0]
    i = pl.program_id(1)

    @pl.when(i == 0)
    def _():
        k16_ref[...] = k_ref[...].astype(BF16)
        v16_ref[...] = v_ref[...].astype(BF16)
        ki16_ref[...] = kiwi_ref[:, 0:IDX_HD].astype(BF16)

    wi = kiwi_blk_ref[:, IDX_HD:IDX_HD + IDX_HEADS]
    qi16 = qi_ref[...].astype(BF16)
    q16 = q_ref[...].astype(BF16)

    def attend(n_keys):
        qpos = i * tq + lax.broadcasted_iota(I32, (tq, n_keys), 0)
        kpos = lax.broadcasted_iota(I32, (tq, n_keys), 1)
        causal = kpos <= qpos
        ki = ki16_ref[0:n_keys, :]
        isc = jnp.zeros((tq, n_keys), F32)
        for h in range(IDX_HEADS):
            s = _dot_nt(qi16[:, h * IDX_HD:(h + 1) * IDX_HD], ki)
            isc += jnp.maximum(s, 0.0) * wi[:, h:h + 1]
        mask = _topk_mask(jnp.where(causal, isc, NEG_INF), kpos, topk, (n_keys - 1).bit_length()) & causal
        for h in range(A_HEADS):
            sl = slice(h * A_HD, (h + 1) * A_HD)
            o = _masked_attention(q16[:, sl], k16_ref[0:n_keys, sl], v16_ref[0:n_keys, sl], mask, A_HD ** -0.5)
            o_ref[:, sl] = o.astype(o_ref.dtype)

    for bkt in range(s_len // bucket):
        pl.when(i // (bucket // tq) == bkt)(functools.partial(attend, (bkt + 1) * bucket))


def dsa_prompt(q, k, v, qi, kiwi):
    b, s, _ = q.shape
    tq = DSA_QUERY_TILE
    topk = min(A_TOPK_MAX, s // 4)
    bucket = min(s, DSA_KEY_BUCKET)
    assert s % bucket == 0 and bucket % tq == 0 and bucket >= topk
    blk = lambda bi, i: (bi, i, 0)
    full = lambda bi, i: (bi, 0, 0)
    return pl.pallas_call(
        functools.partial(_dsa_prompt_kernel, topk=topk, bucket=bucket),
        grid=(b, s // tq),
        in_specs=[pl.BlockSpec((None, tq, A_WIDTH), blk),
                  pl.BlockSpec((None, s, A_WIDTH), full),
                  pl.BlockSpec((None, s, A_WIDTH), full),
                  pl.BlockSpec((None, tq, IDX_HEADS * IDX_HD), blk),
                  pl.BlockSpec((None, s, LANES), full),
                  pl.BlockSpec((None, tq, LANES), blk)],
        out_specs=pl.BlockSpec((None, tq, A_WIDTH), blk),
        out_shape=jax.ShapeDtypeStruct((b, s, A_WIDTH), BF16),
        scratch_shapes=[pltpu.VMEM((s, A_WIDTH), BF16), pltpu.VMEM((s, A_WIDTH), BF16),
                        pltpu.VMEM((s, IDX_HD), BF16)],
        compiler_params=_cparams("parallel", "arbitrary"),
    )(q, k, v, qi, kiwi, kiwi)


def _moba_prompt_kernel(q_ref, k_ref, v_ref, o_ref, kmean_ref, k16_ref, v16_ref, *, ksel):
    tq = q_ref.shape[0]
    s_len, width = k_ref.shape
    nb = s_len // MOBA_BLOCK
    own_id = pl.program_id(1)

    @pl.when(own_id == 0)
    def _():
        k = k_ref[...]
        k16_ref[...] = k.astype(BF16)
        v16_ref[...] = v_ref[...].astype(BF16)
        kmean_ref[...] = jnp.mean(k.reshape(nb, MOBA_BLOCK, width), axis=1)

    q16 = q_ref[...].astype(BF16)
    diagonal = jnp.where(lax.broadcasted_iota(I32, (tq, MOBA_BLOCK), 1)
                         <= lax.broadcasted_iota(I32, (tq, MOBA_BLOCK), 0), 1.0, 0.0)

    def attend(own):
        nidx = lax.broadcasted_iota(I32, (tq, nb), 1)
        for h in range(B_HEADS):
            sl = slice(h * B_HD, (h + 1) * B_HD)
            q = q16[:, sl]
            g = jnp.where(nidx < own, _dot_nt(q, kmean_ref[:, sl].astype(BF16)), NEG_INF)
            rank = jnp.zeros((tq, nb), I32)
            for m in range(own):
                gm = g[:, m:m + 1]
                rank += jnp.where((gm > g) | ((gm == g) & (m < nidx)), 1, 0)
            chosen = jnp.where((nidx < own) & (rank < ksel), 1.0, 0.0)
            mask = jnp.concatenate(
                [jnp.broadcast_to(chosen[:, n:n + 1], (tq, MOBA_BLOCK)) for n in range(own)] + [diagonal], axis=1)
            n_keys = (own + 1) * MOBA_BLOCK
            o = _masked_attention(q, k16_ref[0:n_keys, sl], v16_ref[0:n_keys, sl], mask > 0.5, B_HD ** -0.5)
            o_ref[:, sl] = o.astype(o_ref.dtype)

    for own in range(nb):
        pl.when(own_id == own)(functools.partial(attend, own))


def moba_prompt(q, k, v):
    b, s, _ = q.shape
    nb = s // MOBA_BLOCK
    tq = MOBA_BLOCK
    blk = lambda bi, i: (bi, i, 0)
    full = lambda bi, i: (bi, 0, 0)
    return pl.pallas_call(
        functools.partial(_moba_prompt_kernel, ksel=max(1, min(MOBA_TOPK, nb - 1))),
        grid=(b, nb),
        in_specs=[pl.BlockSpec((None, tq, B_WIDTH), blk),
                  pl.BlockSpec((None, s, B_WIDTH), full),
                  pl.BlockSpec((None, s, B_WIDTH), full)],
        out_specs=pl.BlockSpec((None, tq, B_WIDTH), blk),
        out_shape=jax.ShapeDtypeStruct((b, s, B_WIDTH), BF16),
        scratch_shapes=[pltpu.VMEM((nb, B_WIDTH), F32), pltpu.VMEM((s, B_WIDTH), BF16),
                        pltpu.VMEM((s, B_WIDTH), BF16)],
        compiler_params=_cparams("parallel", "arbitrary"),
    )(q, k, v)


def _split3(x):
    hi = x.astype(BF16)
    r1 = x - hi.astype(F32)
    mid = r1.astype(BF16)
    lo = (r1 - mid.astype(F32)).astype(BF16)
    return hi, mid, lo


def _dot_tn(a, b):
    return lax.dot_general(a, b, (((0,), (0,)), ((), ())), preferred_element_type=F32)


def _wkv_chunk(r, ld, k, v, kkr, a, states, c):
    heads = range(len(states))
    hs = lambda x, h: x[:, h * C_HD:(h + 1) * C_HD]
    row = lax.broadcasted_iota(I32, (c, c), 0)
    col = lax.broadcasted_iota(I32, (c, c), 1)
    strict = row > col
    incl = row >= col
    incl16 = jnp.where(incl, 1.0, 0.0).astype(BF16)
    eye = jnp.where(row == col, 1.0, 0.0)
    hi, mid, lo = _split3(ld)
    cum = _dot(incl16, hi) + _dot(incl16, mid) + _dot(incl16, lo)
    g_inc = jnp.exp(cum)
    g_exc = jnp.exp(cum - ld)
    g_inv = jnp.exp(-cum)
    g_end = g_inc[c - 1:c, :]
    rt = r * g_inc
    kt = k * g_inv
    k_end = kt * g_end

    kk = []
    for h in heads:
        x = hs(kkr, h)
        kk.append(x / jnp.maximum(jnp.sqrt(jnp.sum(x * x, axis=-1, keepdims=True)), 1e-12))
    at16 = [(-kk[h] * hs(g_exc, h)).astype(BF16) for h in heads]
    bt = [kk[h] * hs(a, h) * hs(g_inv, h) for h in heads]
    ar16 = [jnp.concatenate([at16[h], hs(rt, h).astype(BF16)], axis=0) for h in heads]
    bk16 = [jnp.concatenate([bt[h], hs(kt, h)], axis=0).astype(BF16) for h in heads]
    gram = [_dot_nt(ar16[h], bk16[h]) for h in heads]
    l_ab = [jnp.where(strict, gram[h][:c, :c], 0.0) for h in heads]
    l_ak16 = [jnp.where(strict, gram[h][:c, c:], 0.0).astype(BF16) for h in heads]
    col2 = lax.broadcasted_iota(I32, (c, 2 * c), 1)
    incl2 = lax.broadcasted_iota(I32, (c, 2 * c), 0) >= jnp.where(col2 >= c, col2 - c, col2)
    l_r16 = [jnp.where(incl2, gram[h][c:, :], 0.0).astype(BF16) for h in heads]
    l_rb16 = [l_r16[h][:, :c] for h in heads]
    inv = [eye + l_ab[h] for h in heads]
    lp16 = [l_ab[h].astype(BF16) for h in heads]
    n = 2
    while n < c:
        lp16 = [_dot(lp16[h], lp16[h]).astype(BF16) for h in heads]
        inv = [inv[h] + _dot(inv[h].astype(BF16), lp16[h]) for h in heads]
        n *= 2
    inv16 = [inv[h].astype(BF16) for h in heads]
    v16 = [hs(v, h).astype(BF16) for h in heads]
    lakv16 = [_dot(l_ak16[h], v16[h]).astype(BF16) for h in heads]
    w16 = [_dot(inv16[h], at16[h]).astype(BF16) for h in heads]
    u0_16 = [_dot(inv16[h], lakv16[h]).astype(BF16) for h in heads]
    r2_16 = [(hs(rt, h) + _dot(l_rb16[h], w16[h])).astype(BF16) for h in heads]
    uv16 = [jnp.concatenate([u0_16[h], v16[h]], axis=0) for h in heads]
    y0 = [_dot(l_r16[h], uv16[h]) for h in heads]
    bend16 = [(bt[h] * hs(g_end, h)).astype(BF16) for h in heads]
    bkend16 = [jnp.concatenate([bend16[h], hs(k_end, h).astype(BF16)], axis=0) for h in heads]
    m2_16 = [_dot_tn(w16[h], bend16[h]).astype(BF16) for h in heads]
    nmat = [_dot_tn(uv16[h], bkend16[h]) for h in heads]
    s16 = [states[h].astype(BF16) for h in heads]
    ys = [_dot_nt(r2_16[h], s16[h]) + y0[h] for h in heads]
    new_states = [states[h] * hs(g_end, h) + _dot(s16[h], m2_16[h]) + nmat[h] for h in heads]
    return ys, new_states


WKV_VECS = ('c_w0', 'c_a0', 'c_k_k', 'c_k_a', 'c_ln_w', 'c_ln_b', 'c_r_k')


def _wkv_kernel(p_ref, prev_ref, mu_ref, vec_ref, w2_ref, a2_ref, g2_ref, s0_ref,
                o_ref, sfin_ref, state_ref, last_ref, *, chunk, valid_len):
    tb = p_ref.shape[0]
    t_blk = pl.program_id(1)
    hs = lambda x, h: x[:, h * C_HD:(h + 1) * C_HD]
    w0, a0, k_k, k_a, ln_w, ln_b, r_k = (vec_ref[i:i + 1, :] for i in range(len(WKV_VECS)))

    @pl.when(t_blk == 0)
    def _():
        state_ref[...] = s0_ref[...]
        last_ref[...] = prev_ref[...]

    def chunk_step(ci, prev_row):
        start = pl.multiple_of(ci * chunk, chunk)
        rows = pl.ds(start, chunk)
        pc = p_ref[rows, :]
        first = lax.broadcasted_iota(I32, pc.shape, 0) == 0
        shifted = jnp.where(first, prev_row, pltpu.roll(pc, 1, 0))
        xs = pc + mu_ref[...] * (shifted - pc)
        r = xs[:, 0:C_WIDTH]
        k = xs[:, C_WIDTH:2 * C_WIDTH]
        v = xs[:, 2 * C_WIDTH:3 * C_WIDTH]
        lora = xs[:, LORA_OFF:C_BLOCK]
        z = w0 + _dot(jnp.tanh(lora).astype(BF16), w2_ref[...])
        ld = -jnp.exp(jnp.minimum(z, 0.0) - jnp.log1p(jnp.exp(-jnp.abs(z))) - 0.5)
        a = 1.0 / (1.0 + jnp.exp(-(a0 + _dot(lora.astype(BF16), a2_ref[...]))))
        g = _dot((1.0 / (1.0 + jnp.exp(-lora))).astype(BF16), g2_ref[...])
        kkr = k * k_k
        k = k * (1.0 + (a - 1.0) * k_a)
        if valid_len is not None:
            live = (t_blk * tb + start + lax.broadcasted_iota(I32, k.shape, 0)) < valid_len
            ld, kkr, k, v = (jnp.where(live, x, 0.0) for x in (ld, kkr, k, v))
        states = [state_ref[h] for h in range(C_HEADS)]
        ys, new_states = _wkv_chunk(r, ld, k, v, kkr, a, states, chunk)
        for h in range(C_HEADS):
            state_ref[h] = new_states[h]
        rkr = r * k * r_k
        outs = []
        for h in range(C_HEADS):
            y = ys[h]
            mean = jnp.mean(y, axis=-1, keepdims=True)
            var = jnp.mean(jnp.square(y - mean), axis=-1, keepdims=True)
            yn = (y - mean) * lax.rsqrt(var + GN_EPS)
            bonus = jnp.sum(hs(rkr, h), axis=-1, keepdims=True) * hs(v, h)
            outs.append(yn * hs(ln_w, h) + hs(ln_b, h) + bonus)
        o_ref[rows, :] = (jnp.concatenate(outs, axis=1) * g).astype(o_ref.dtype)
        return pc[chunk - 1:chunk, :]

    last_ref[...] = lax.fori_loop(0, tb // chunk, chunk_step, last_ref[...])

    @pl.when(t_blk == pl.num_programs(1) - 1)
    def _():
        sfin_ref[...] = state_ref[...]


def wkv7(p, prev, mu, vecs, w2, a2, g2, s0, chunk, tb, valid_len):
    b, s, _ = p.shape
    const = lambda shape: pl.BlockSpec(shape, lambda bi, i: (0,) * len(shape))
    st = pl.BlockSpec((None, C_HEADS, C_HD, C_HD), lambda bi, i: (bi, 0, 0, 0))
    return pl.pallas_call(
        functools.partial(_wkv_kernel, chunk=chunk, valid_len=valid_len if valid_len < s else None),
        grid=(b, s // tb),
        in_specs=[pl.BlockSpec((None, tb, C_BLOCK), lambda bi, i: (bi, i, 0)),
                  pl.BlockSpec((None, 1, C_BLOCK), lambda bi, i: (bi, 0, 0)),
                  const(mu.shape), const(vecs.shape), const(w2.shape), const(a2.shape), const(g2.shape), st],
        out_specs=[pl.BlockSpec((None, tb, C_WIDTH), lambda bi, i: (bi, i, 0)), st],
        out_shape=[jax.ShapeDtypeStruct((b, s, C_WIDTH), BF16),
                   jax.ShapeDtypeStruct((b, C_HEADS, C_HD, C_HD), F32)],
        scratch_shapes=[pltpu.VMEM((C_HEADS, C_HD, C_HD), F32), pltpu.VMEM((1, C_BLOCK), F32)],
        compiler_params=_cparams("parallel", "arbitrary"),
    )(p, prev, mu, vecs, w2, a2, g2, s0)


def _split2(x):
    hi = x.astype(BF16)
    return hi, (x - hi.astype(F32)).astype(BF16)


def _page_specs(rows, width, layer, n_pages, pg):
    assert n_pages % pg == 0
    last_group = n_pages // pg - 1

    def spec(slot):
        def index_map(b, j, pt):
            return (layer, pt[b, jnp.minimum(j, last_group) * pg + slot], 0, 0)
        return pl.BlockSpec((None, None, rows, width), index_map)
    return [spec(slot) for slot in range(pg)]


def _idx_sample_kernel(pt_ref, qi_ref, w_ref, knew_ref, *rest, t):
    pages, o_ref = rest[:-1], rest[-1]
    j = pl.program_id(1)
    last = pl.num_programs(1) - 1
    q_hi, q_lo = _split2(qi_ref[...])
    w = w_ref[...]

    def scores(kpages):
        splits = [_split2(kp) for kp in kpages]
        s = [_dot(q_hi, k_hi) + _dot(q_hi, k_lo) + _dot(q_lo, k_hi) for k_hi, k_lo in splits]
        s = [jnp.maximum(x, 0.0) * w for x in s]
        return [jnp.sum(x.reshape(t, IDX_HEADS, PAGE_SIZE), axis=1) for x in s]

    @pl.when(j < last)
    def _():
        o_ref[...] = jnp.concatenate(scores([page[...] for page in pages]), axis=1)

    @pl.when(j == last)
    def _():
        o_ref[...] = jnp.zeros(o_ref.shape, F32)
        o_ref[:, 0:PAGE_SIZE] = scores([knew_ref[...]])[0]


def idx_sample(page_table, qi, w, knew, cache_idx, layer):
    db, rows, _ = qi.shape
    t = rows // IDX_HEADS
    n_pages = page_table.shape[1]
    pg = min(IDX_PAGES_PER_STEP, n_pages)
    steps = n_pages // pg + 1
    per_b = lambda b, j, pt: (b, 0, 0)
    return pl.pallas_call(
        functools.partial(_idx_sample_kernel, t=t),
        grid_spec=pltpu.PrefetchScalarGridSpec(
            num_scalar_prefetch=1,
            grid=(db, steps),
            in_specs=[pl.BlockSpec((None, rows, IDX_HD), per_b),
                      pl.BlockSpec((None, rows, 1), per_b),
                      pl.BlockSpec((None, IDX_HD, PAGE_SIZE), per_b)]
                     + _page_specs(IDX_HD, PAGE_SIZE, layer, n_pages, pg),
            out_specs=pl.BlockSpec((None, t, pg * PAGE_SIZE), lambda b, j, pt: (b, 0, j))),
        out_shape=jax.ShapeDtypeStruct((db, t, steps * pg * PAGE_SIZE), F32),
        compiler_params=_cparams("parallel", "arbitrary"),
    )(page_table, qi, w, knew, *([cache_idx] * pg))


def _select_sample_kernel(isc_ref, o_ref, *, topk, t, past_len):
    rows, n = isc_ref.shape
    pos = lax.broadcasted_iota(I32, (rows, n), 1)
    r = pl.program_id(0) * rows + lax.broadcasted_iota(I32, (rows, n), 0)
    causal = pos <= past_len + (r & (t - 1))
    mask = _topk_mask(jnp.where(causal, isc_ref[...], NEG_INF), pos, topk, n.bit_length()) & causal
    o_ref[...] = jnp.where(mask, 1.0, 0.0)


def select_sample(isc, topk, t, past_len):
    rows, n = isc.shape
    tr = 16
    assert t & (t - 1) == 0 and rows % tr == 0
    return pl.pallas_call(
        functools.partial(_select_sample_kernel, topk=topk, t=t, past_len=past_len),
        grid=(rows // tr,),
        in_specs=[pl.BlockSpec((tr, n), lambda i: (i, 0))],
        out_specs=pl.BlockSpec((tr, n), lambda i: (i, 0)),
        out_shape=jax.ShapeDtypeStruct((rows, n), F32),
        compiler_params=_cparams("parallel"),
    )(isc)


def _own_head(shape, heads):
    return ((lax.broadcasted_iota(I32, shape, 1) & (heads - 1))
            == (lax.broadcasted_iota(I32, shape, 0) & (heads - 1)))


def _dsa_sample_kernel(pt_ref, q_ref, mask_ref, knew_ref, vnew_ref, *rest, heads, pg):
    kpages, vpages = rest[:pg], rest[pg:2 * pg]
    o_ref, m_ref, l_ref, acc_ref = rest[2 * pg:]
    j = pl.program_id(1)
    last = pl.num_programs(1) - 1
    q = q_ref[...]
    page_rows = PAGE_SIZE * heads
    shift = heads.bit_length() - 1
    spread = jnp.where(
        jnp.right_shift(lax.broadcasted_iota(I32, (PAGE_SIZE, page_rows), 1), shift)
        == lax.broadcasted_iota(I32, (PAGE_SIZE, page_rows), 0), 1.0, 0.0).astype(BF16)

    @pl.when(j == 0)
    def _():
        m_ref[...] = jnp.full(m_ref.shape, NEG_INF, F32)
        l_ref[...] = jnp.zeros(l_ref.shape, F32)
        acc_ref[...] = jnp.zeros(acc_ref.shape, F32)

    def update(ks, vs, mask):
        s = jnp.concatenate([_dot_nt(q, kp.astype(BF16)) for kp in ks], axis=1) * (A_HD ** -0.5)
        sel = jnp.concatenate([_dot(mask[:, i * PAGE_SIZE:(i + 1) * PAGE_SIZE].astype(BF16), spread)
                               for i in range(len(ks))], axis=1)
        s = jnp.where((sel > 0.5) & _own_head(s.shape, heads), s, NEG_INF)
        m_old = m_ref[...]
        m_new = jnp.maximum(m_old, jnp.max(s, axis=-1, keepdims=True))
        m_safe = jnp.where(m_new == NEG_INF, 0.0, m_new)
        p = jnp.exp(s - m_safe)
        alpha = jnp.exp(m_old - m_safe)
        l_ref[...] = alpha * l_ref[...] + jnp.sum(p, axis=-1, keepdims=True)
        acc = alpha * acc_ref[...]
        for i, vp in enumerate(vs):
            acc += _dot(p[:, i * page_rows:(i + 1) * page_rows].astype(BF16), vp.astype(BF16))
        acc_ref[...] = acc
        m_ref[...] = m_new

    @pl.when(j < last)
    def _():
        update([kp[...] for kp in kpages], [vp[...] for vp in vpages], mask_ref[...])

    @pl.when(j == last)
    def _():
        update([knew_ref[...]], [vnew_ref[...]], mask_ref[:, 0:PAGE_SIZE])
        o_ref[...] = acc_ref[...] / l_ref[...]


def dsa_sample(page_table, q, mask, knew, vnew, cache_k, cache_v, layer, heads):
    db, rows, hd = q.shape
    n_pages = page_table.shape[1]
    pg = min(SAMPLE_PAGES_PER_STEP, n_pages)
    steps = n_pages // pg + 1
    page_rows = PAGE_SIZE * heads
    per_b = lambda b, j, pt: (b, 0, 0)
    return pl.pallas_call(
        functools.partial(_dsa_sample_kernel, heads=heads, pg=pg),
        grid_spec=pltpu.PrefetchScalarGridSpec(
            num_scalar_prefetch=1,
            grid=(db, steps),
            in_specs=[pl.BlockSpec((None, rows, hd), per_b),
                      pl.BlockSpec((None, rows, pg * PAGE_SIZE), lambda b, j, pt: (b, 0, j)),
                      pl.BlockSpec((None, page_rows, hd), per_b),
                      pl.BlockSpec((None, page_rows, hd), per_b)]
                     + _page_specs(page_rows, hd, layer, n_pages, pg) + _page_specs(page_rows, hd, layer, n_pages, pg),
            out_specs=pl.BlockSpec((None, rows, hd), per_b),
            scratch_shapes=[pltpu.VMEM((rows, 1), F32), pltpu.VMEM((rows, 1), F32),
                            pltpu.VMEM((rows, hd), F32)]),
        out_shape=jax.ShapeDtypeStruct((db, rows, hd), F32),
        compiler_params=_cparams("parallel", "arbitrary"),
    )(page_table, q, mask, knew, vnew, *([cache_k] * pg), *([cache_v] * pg))


def _moba_sample_kernel(pt_ref, q_ref, knew_ref, vnew_ref, *rest, heads, ksel, pg):
    ppb = MOBA_BLOCK // PAGE_SIZE
    kpages, vpages = rest[:pg], rest[pg:2 * pg]
    o_ref, g_ref, m_ref, l_ref, ob_ref = rest[2 * pg:]
    j = pl.program_id(1)
    last = pl.num_programs(1) - 1
    q = q_ref[...]
    q16 = q.astype(BF16)
    rows, hd = q.shape
    scale = B_HD ** -0.5
    sublanes = 8
    assert 2 * heads == sublanes and hd == LANES and rows % sublanes == 0
    wide = lambda x: jnp.broadcast_to(x, (rows, LANES))

    def partial_softmax(s, v16):
        m = jnp.max(s, axis=-1, keepdims=True)
        p = jnp.exp(s - m)
        return m, jnp.sum(p, axis=-1, keepdims=True), _dot(p.astype(BF16), v16)

    @pl.when(j < last)
    def _():
        blocks = range(pg // ppb)
        kblk = [jnp.concatenate([kpages[b * ppb + i][...] for i in range(ppb)], axis=0) for b in blocks]
        v16 = [jnp.concatenate([vpages[b * ppb + i][...] for i in range(ppb)], axis=0).astype(BF16)
               for b in blocks]
        s = [_dot_nt(q16, kblk[b].astype(BF16)) * scale for b in blocks]
        own = _own_head(s[0].shape, heads)
        s = [jnp.where(own, s[b], NEG_INF) for b in blocks]
        m = [jnp.max(s[b], axis=-1, keepdims=True) for b in blocks]
        p = [jnp.exp(s[b] - m[b]) for b in blocks]
        l = [jnp.sum(p[b], axis=-1, keepdims=True) for b in blocks]
        o = [_dot(p[b].astype(BF16), v16[b]) for b in blocks]
        fold = [jnp.sum(kblk[b].reshape(kblk[b].shape[0] // sublanes, sublanes, hd), axis=0) for b in blocks]
        kmean = [(fold[b] + pltpu.roll(fold[b], heads, 0)) * (1.0 / MOBA_BLOCK) for b in blocks]
        gate = [jnp.sum(q * jnp.concatenate([kmean[b]] * (rows // sublanes), axis=0), axis=-1, keepdims=True)
                for b in blocks]
        for b in blocks:
            n = j * (pg // ppb) + b
            g_ref[n] = wide(gate[b])
            m_ref[n] = wide(m[b])
            l_ref[n] = wide(l[b])
            ob_ref[n] = o[b]

    @pl.when(j == last)
    def _():
        shift = heads.bit_length() - 1
        s = _dot_nt(q16, knew_ref[...].astype(BF16)) * scale
        causal = (jnp.right_shift(lax.broadcasted_iota(I32, s.shape, 1), shift)
                  <= jnp.right_shift(lax.broadcasted_iota(I32, s.shape, 0), shift))
        s = jnp.where(causal & _own_head(s.shape, heads), s, NEG_INF)
        m_own, l_own, o_own = partial_softmax(s, vnew_ref[...].astype(BF16))
        m_own, l_own = wide(m_own), wide(l_own)
        gates = g_ref[...]
        nidx = lax.broadcasted_iota(I32, gates.shape, 0)
        chosen = jnp.zeros(gates.shape, F32)
        for _ in range(ksel):
            best = jnp.max(gates, axis=0, keepdims=True)
            first = jnp.min(jnp.where(gates == best, nidx, gates.shape[0]), axis=0, keepdims=True)
            pick = nidx == first
            chosen = jnp.where(pick, 1.0, chosen)
            gates = jnp.where(pick, NEG_INF, gates)
        m_blk = m_ref[...]
        m_tot = jnp.maximum(m_own, jnp.max(jnp.where(chosen > 0.5, m_blk, NEG_INF), axis=0))
        wgt = jnp.where(chosen > 0.5, jnp.exp(m_blk - m_tot[None]), 0.0)
        w_own = jnp.exp(m_own - m_tot)
        l_tot = w_own * l_own + jnp.sum(wgt * l_ref[...], axis=0)
        acc = w_own * o_own
        for n in range(gates.shape[0]):
            acc += wgt[n] * ob_ref[n]
        o_ref[...] = acc / l_tot


def moba_sample(page_table, q, knew, vnew, cache_k, cache_v, layer, heads):
    db, rows, hd = q.shape
    n_pages = page_table.shape[1]
    pg = min(SAMPLE_PAGES_PER_STEP, n_pages)
    steps = n_pages // pg + 1
    nbp = n_pages * PAGE_SIZE // MOBA_BLOCK
    page_rows = PAGE_SIZE * heads
    per_b = lambda b, j, pt: (b, 0, 0)
    return pl.pallas_call(
        functools.partial(_moba_sample_kernel, heads=heads, ksel=max(1, min(MOBA_TOPK, nbp)), pg=pg),
        grid_spec=pltpu.PrefetchScalarGridSpec(
            num_scalar_prefetch=1,
            grid=(db, steps),
            in_specs=[pl.BlockSpec((None, rows, hd), per_b),
                      pl.BlockSpec((None, page_rows, hd), per_b),
                      pl.BlockSpec((None, page_rows, hd), per_b)]
                     + _page_specs(page_rows, hd, layer, n_pages, pg) + _page_specs(page_rows, hd, layer, n_pages, pg),
            out_specs=pl.BlockSpec((None, rows, hd), per_b),
            scratch_shapes=[pltpu.VMEM((nbp, rows, LANES), F32), pltpu.VMEM((nbp, rows, LANES), F32),
                            pltpu.VMEM((nbp, rows, LANES), F32), pltpu.VMEM((nbp, rows, hd), F32)]),
        out_shape=jax.ShapeDtypeStruct((db, rows, hd), F32),
        compiler_params=_cparams("parallel", "arbitrary"),
    )(page_table, q, knew, vnew, *([cache_k] * pg), *([cache_v] * pg))


def _new_page(x, rows):
    return jnp.pad(x, ((0, 0), (0, rows - x.shape[1]), (0, 0)))


def _sample_attention(qa, ka, va, qi, kiwi, qb, kb, vb, caches, page_table, layer, db, t):
    cache_a_k, cache_a_v, cache_a_idx, cache_b_k, cache_b_v = caches
    past_len = page_table.shape[1] * PAGE_SIZE
    topk = min(A_TOPK_MAX, (past_len + t) // 4)
    ki = kiwi[:, :IDX_HD].reshape(db, t, IDX_HD)
    wi = kiwi[:, IDX_HD:IDX_HD + IDX_HEADS]
    knew = _new_page(ki, PAGE_SIZE).transpose(0, 2, 1)
    isc = idx_sample(page_table, qi.reshape(db, t * IDX_HEADS, IDX_HD), wi.reshape(db, t * IDX_HEADS, 1),
                     knew, cache_a_idx, layer)
    n = isc.shape[-1]
    mask = select_sample(isc.reshape(db * t, n), topk, t, past_len).reshape(db, t, n)
    rows_a = lambda z: z.reshape(db, t * A_HEADS, A_HD)
    rows_b = lambda z: z.reshape(db, t * B_HEADS, B_HD)
    oa = dsa_sample(page_table, rows_a(qa).astype(BF16), jnp.repeat(mask, A_HEADS, axis=1),
                    _new_page(rows_a(ka), PAGE_SIZE * A_HEADS), _new_page(rows_a(va), PAGE_SIZE * A_HEADS),
                    cache_a_k, cache_a_v, layer, A_HEADS)
    ob = moba_sample(page_table, rows_b(qb), _new_page(rows_b(kb), PAGE_SIZE * B_HEADS),
                     _new_page(rows_b(vb), PAGE_SIZE * B_HEADS), cache_b_k, cache_b_v, layer, B_HEADS)
    return oa.reshape(db * t, A_WIDTH), ob.reshape(db * t, B_WIDTH)


def _pad_cols(x, width):
    return jnp.pad(x, [(0, 0)] * (x.ndim - 1) + [(0, width - x.shape[-1])])


def _layer_params(l, norm1_g, w_in, a_q_norm, a_k_norm, idx_k_norm, b_q_norm, b_k_norm, c_mu, c_w0, c_w2, c_a0,
                  c_a2, c_g2, c_k_k, c_k_a, c_r_k, c_ln_w, c_ln_b, w_o, norm2_g, w_ff1, w_ff2):
    w = w_in[l]
    a_cols, b_cols, c_cols = w[:, :A_COLS], w[:, A_COLS:A_COLS + B_COLS], w[:, A_COLS + B_COLS:]
    w_pack = jnp.concatenate([_pad_cols(c_cols, A_OFF), _pad_cols(a_cols, B_OFF - A_OFF), b_cols],
                             axis=1).astype(BF16)
    lora_rows = lambda x, off: jnp.pad(x, ((off, LORA_BLOCK - off - x.shape[0]), (0, 0))).astype(BF16)
    vec = dict(c_w0=c_w0[l], c_a0=c_a0[l], c_k_k=c_k_k[l], c_k_a=c_k_a[l], c_ln_w=c_ln_w[l], c_ln_b=c_ln_b[l],
               c_r_k=c_r_k[l].reshape(-1))
    vecs = jnp.stack([vec[name] for name in WKV_VECS] + [jnp.zeros((C_WIDTH,), F32)])
    gains = jnp.stack([a_q_norm[l], a_k_norm[l], b_q_norm[l], b_k_norm[l], _pad_cols(idx_k_norm[l], LANES)]
                      + [jnp.zeros((LANES,), F32)] * 3)
    return dict(norm1_g=norm1_g[l], w_in=w_pack, gains=gains, mu=_pad_cols(c_mu[l], C_BLOCK)[None], vecs=vecs,
                w2=lora_rows(c_w2[l], 0), a2=lora_rows(c_a2[l], W_LORA), g2=lora_rows(c_g2[l], W_LORA + A_LORA),
                w_o=w_o[l].astype(BF16), norm2_g=norm2_g[l], w_ff1=w_ff1[l].astype(BF16),
                w_ff2=w_ff2[l].astype(BF16))


def _layer(x, tabs, prev_shift, wkv0, lw, sample_ctx):
    b, s, d = x.shape
    m = b * s
    p = norm_matmul(x.reshape(m, d), lw['norm1_g'], lw['w_in'], min(m, 1024), PROJ_TILE_N)
    qa, ka, va, qi, kiwi, qb, kb, vb, ka_rows, va_rows, kb_rows, vb_rows = prep(
        p, tabs[0], tabs[1], lw['gains'], min(m, 256))
    p3 = p.reshape(b, s, P_PACK)
    if sample_ctx is None:
        seq = lambda z: z.reshape(b, s, z.shape[-1])
        oa = dsa_prompt(seq(qa), seq(ka), seq(va), seq(qi), seq(kiwi)).reshape(m, A_WIDTH)
        ob = moba_prompt(seq(qb), seq(kb), seq(vb)).reshape(m, B_WIDTH)
        p_scan = p3
    else:
        caches, page_table, layer = sample_ctx
        oa, ob = _sample_attention(qa, ka_rows, va_rows, qi, kiwi, qb, kb_rows, vb_rows, caches, page_table,
                                   layer, b, s)
        oa, ob = oa.astype(BF16), ob.astype(BF16)
        p_scan = jnp.pad(p3, ((0, 0), (0, -s % 16), (0, 0)))
    sp = p_scan.shape[1]
    chunk = min(WKV_CHUNK, sp)
    oc, wkv_fin = wkv7(p_scan, _pad_cols(prev_shift, C_BLOCK)[:, None, :], lw['mu'], lw['vecs'], lw['w2'],
                       lw['a2'], lw['g2'], wkv0, chunk, min(sp, 4 * chunk), s)
    tm2 = min(m, 512)
    x2 = out_proj(x.reshape(m, d), oa, ob, oc[:, :s].reshape(m, C_WIDTH), lw['w_o'], tm2)
    x3 = mlp(x2, lw['norm2_g'], lw['w_ff1'], lw['w_ff2'], tm2, min(MLP_TILE_F, lw['w_ff1'].shape[1])).reshape(b, s, d)
    rows = (ka_rows.reshape(b, s, A_HEADS, A_HD), va_rows.reshape(b, s, A_HEADS, A_HD),
            kiwi[:, :IDX_HD].reshape(b, s, IDX_HD), kb_rows.reshape(b, s, B_HEADS, B_HD),
            vb_rows.reshape(b, s, B_HEADS, B_HD), wkv_fin, p3[:, s - 1, :C_COLS])
    return x3, rows


def kernel(x_prompt, x_sample, cache_a_k, cache_a_v, cache_a_idx, cache_b_k, cache_b_v, state_wkv, state_shift, page_table, norm1_g, w_in, a_q_norm, a_k_norm, idx_k_norm, b_q_norm, b_k_norm, c_mu, c_w0, c_w2, c_a0, c_a2, c_g2, c_k_k, c_k_a, c_r_k, c_ln_w, c_ln_b, w_o, norm2_g, w_ff1, w_ff2):
    depth = w_in.shape[0]
    bp, sp = x_prompt.shape[:2]
    db, t = x_sample.shape[:2]
    pos_p = jnp.arange(sp)
    pos_s = jnp.tile(page_table.shape[1] * PAGE_SIZE + jnp.arange(t), db)
    tabs_p = (_rope_tables(pos_p, A_HD, 1), _rope_tables(pos_p, IDX_HD, LANES // IDX_HD))
    tabs_s = (_rope_tables(pos_s, A_HD, 1), _rope_tables(pos_s, IDX_HD, LANES // IDX_HD))
    shift0 = jnp.zeros((bp, C_COLS), x_prompt.dtype)
    wkv0 = jnp.zeros((bp, C_HEADS, C_HD, C_HD), F32)
    xp, xs = x_prompt, x_sample
    by_row = lambda c: c.reshape(c.shape[0], c.shape[1], -1, c.shape[-1])
    caches = (by_row(cache_a_k), by_row(cache_a_v), cache_a_idx.transpose(0, 1, 3, 2),
              by_row(cache_b_k), by_row(cache_b_v))
    p_rows, s_rows = [], []
    for l in range(depth):
        lw = _layer_params(l, norm1_g, w_in, a_q_norm, a_k_norm, idx_k_norm, b_q_norm, b_k_norm, c_mu, c_w0, c_w2,
                           c_a0, c_a2, c_g2, c_k_k, c_k_a, c_r_k, c_ln_w, c_ln_b, w_o, norm2_g, w_ff1, w_ff2)
        xp, rows = _layer(xp, tabs_p, shift0, wkv0, lw, None)
        p_rows.append(rows)
        xs, rows = _layer(xs, tabs_s, state_shift[l], state_wkv[l], lw, (caches, page_table, l))
        s_rows.append(rows)
    p_out = [jnp.stack(z) for z in zip(*p_rows)]
    s_out = [jnp.stack(z) for z in zip(*s_rows)]
    return (xp, xs, *p_out, *s_out)
```

```python
import functools

import jax
import jax.numpy as jnp
from jax import lax
from jax.experimental import pallas as pl
from jax.experimental.pallas import tpu as pltpu

F32 = jnp.float32
BF16 = jnp.bfloat16
I32 = jnp.int32

VMEM_LIMIT_BYTES = 56 * 1024 * 1024
LANES = 128

PAGE_SIZE = 128
A_HEADS = 4
A_HD = 128
A_WIDTH = A_HEADS * A_HD
IDX_HEADS = 16
IDX_HD = 64
A_TOPK_MAX = 256
B_HEADS = 4
B_HD = 128
B_WIDTH = B_HEADS * B_HD
MOBA_BLOCK = 256
MOBA_TOPK = 3
C_HD = 64
C_HEADS = 16
C_WIDTH = C_HEADS * C_HD
W_LORA = 64
A_LORA = 64
G_LORA = 160
A_COLS = 3 * A_WIDTH + IDX_HEADS * IDX_HD + IDX_HD + IDX_HEADS
B_COLS = 3 * B_WIDTH
C_COLS = 3 * C_WIDTH + W_LORA + A_LORA + G_LORA
ROPE_THETA = 500000.0
NORM_EPS = 1e-6
GN_EPS = 64e-5
IDX_W_SCALE = (IDX_HEADS * IDX_HD) ** -0.5
WKV_CHUNK = 64
SAMPLE_PAGES_PER_STEP = 16
IDX_PAGES_PER_STEP = 64
DSA_KEY_BUCKET = 256
DSA_QUERY_TILE = 128
PROJ_TILE_N = 1536
MLP_TILE_F = 1024

C_BLOCK = C_COLS + (-C_COLS % LANES)
A_OFF = C_BLOCK
B_OFF = A_OFF + A_COLS + (-A_COLS % LANES)
P_PACK = B_OFF + B_COLS
assert P_PACK % PROJ_TILE_N == 0
LORA_OFF = 3 * C_WIDTH
LORA_BLOCK = C_BLOCK - LORA_OFF
A_QI_OFF = A_OFF + 3 * A_WIDTH
A_KI_OFF = A_QI_OFF + IDX_HEADS * IDX_HD
assert A_KI_OFF + LANES == B_OFF and IDX_HD + IDX_HEADS <= LANES

NEG_INF = float("-inf")
INT_MIN = -2 ** 31


def _cparams(*sem):
    return pltpu.CompilerParams(dimension_semantics=sem, vmem_limit_bytes=VMEM_LIMIT_BYTES)


def _dot_nt(a, b):
    return lax.dot_general(a, b, (((1,), (1,)), ((), ())), preferred_element_type=F32)


def _dot(a, b):
    return jnp.dot(a, b, preferred_element_type=F32)


def _norm_matmul_kernel(x_ref, g_ref, w_ref, o_ref, xn_ref):
    @pl.when(pl.program_id(1) == 0)
    def _():
        x = x_ref[...]
        y = x * lax.rsqrt(jnp.mean(x * x, axis=-1, keepdims=True) + NORM_EPS)
        xn_ref[...] = (y * g_ref[...]).astype(BF16)

    o_ref[...] = _dot(xn_ref[...], w_ref[...])


def norm_matmul(x, g, w, tm, tn):
    m, d = x.shape
    p = w.shape[1]
    return pl.pallas_call(
        _norm_matmul_kernel,
        grid=(m // tm, p // tn),
        in_specs=[pl.BlockSpec((tm, d), lambda i, j: (i, 0)),
                  pl.BlockSpec((1, d), lambda i, j: (0, 0)),
                  pl.BlockSpec((d, tn), lambda i, j: (0, j))],
        out_specs=pl.BlockSpec((tm, tn), lambda i, j: (i, j)),
        out_shape=jax.ShapeDtypeStruct((m, p), F32),
        scratch_shapes=[pltpu.VMEM((tm, d), BF16)],
        compiler_params=_cparams("parallel", "arbitrary"),
    )(x, g.reshape(1, d), w)


def _out_proj_kernel(x_ref, oa_ref, ob_ref, oc_ref, w_ref, o_ref):
    wa = oa_ref.shape[1]
    wb = ob_ref.shape[1]
    acc = _dot(oa_ref[...], w_ref[0:wa, :])
    acc += _dot(ob_ref[...], w_ref[wa:wa + wb, :])
    acc += _dot(oc_ref[...], w_ref[wa + wb:, :])
    o_ref[...] = x_ref[...] + acc


def out_proj(x, oa, ob, oc, w, tm):
    m, d = x.shape
    row = lambda i: (i, 0)
    return pl.pallas_call(
        _out_proj_kernel,
        grid=(m // tm,),
        in_specs=[pl.BlockSpec((tm, d), row),
                  pl.BlockSpec((tm, oa.shape[1]), row),
                  pl.BlockSpec((tm, ob.shape[1]), row),
                  pl.BlockSpec((tm, oc.shape[1]), row),
                  pl.BlockSpec(w.shape, lambda i: (0, 0))],
        out_specs=pl.BlockSpec((tm, d), row),
        out_shape=jax.ShapeDtypeStruct((m, d), F32),
        compiler_params=_cparams("parallel"),
    )(x, oa, ob, oc, w)


def _mlp_kernel(x_ref, g_ref, w1_ref, w2_ref, o_ref, xn_ref):
    @pl.when(pl.program_id(1) == 0)
    def _():
        x = x_ref[...]
        y = x * lax.rsqrt(jnp.mean(x * x, axis=-1, keepdims=True) + NORM_EPS)
        xn_ref[...] = (y * g_ref[...]).astype(BF16)
        o_ref[...] = x

    h = jnp.maximum(_dot(xn_ref[...], w1_ref[...]), 0.0)
    o_ref[...] += _dot((h * h).astype(BF16), w2_ref[...])


def mlp(x, g, w1, w2, tm, tf):
    m, d = x.shape
    f = w1.shape[1]
    return pl.pallas_call(
        _mlp_kernel,
        grid=(m // tm, f // tf),
        in_specs=[pl.BlockSpec((tm, d), lambda i, j: (i, 0)),
                  pl.BlockSpec((1, d), lambda i, j: (0, 0)),
                  pl.BlockSpec((d, tf), lambda i, j: (0, j)),
                  pl.BlockSpec((tf, d), lambda i, j: (j, 0))],
        out_specs=pl.BlockSpec((tm, d), lambda i, j: (i, 0)),
        out_shape=jax.ShapeDtypeStruct((m, d), F32),
        scratch_shapes=[pltpu.VMEM((tm, d), BF16)],
        compiler_params=_cparams("parallel", "arbitrary"),
    )(x, g.reshape(1, d), w1, w2)


def _rope_tables(pos, hd, reps):
    rot = hd // 4
    half = rot // 2
    inv = ROPE_THETA ** (-2.0 * jnp.arange(half, dtype=F32) / rot)
    ang = pos.astype(F32)[:, None] * inv[None, :]
    cos, sin = jnp.cos(ang), jnp.sin(ang)
    zeros = jnp.zeros_like(sin)
    rest = jnp.zeros((pos.shape[0], hd - rot), F32)
    tabs = [jnp.concatenate([cos, cos, rest + 1.0], axis=1),
            jnp.concatenate([zeros, sin, rest], axis=1),
            jnp.concatenate([-sin, zeros, rest], axis=1)]
    return jnp.stack([jnp.tile(z, (1, reps)) for z in tabs])


def _rope(y, tab_ref, half):
    width = y.shape[1]
    return (y * tab_ref[0] + pltpu.roll(y, half, 1) * tab_ref[1]
            + pltpu.roll(y, width - half, 1) * tab_ref[2])


def _prep_kernel(p_ref, ta_ref, ti_ref, g_ref, qa_ref, ka_ref, va_ref, qi_ref, kiwi_ref, qb_ref, kb_ref, vb_ref,
                 ka_rows_ref, va_rows_ref, kb_rows_ref, vb_rows_ref):
    tm = p_ref.shape[0]

    def normed(x, g):
        return x * lax.rsqrt(jnp.mean(x * x, axis=-1, keepdims=True) + NORM_EPS) * g

    for base, q_ref, k_ref, v_ref, k_rows_ref, v_rows_ref, heads, hd, gq, gk in (
            (A_OFF, qa_ref, ka_ref, va_ref, ka_rows_ref, va_rows_ref, A_HEADS, A_HD, 0, 1),
            (B_OFF, qb_ref, kb_ref, vb_ref, kb_rows_ref, vb_rows_ref, B_HEADS, B_HD, 2, 3)):
        width = heads * hd
        for h in range(heads):
            sl = slice(h * hd, (h + 1) * hd)
            q_ref[:, sl] = _rope(normed(p_ref[:, base + h * hd:base + (h + 1) * hd], g_ref[gq:gq + 1, :]),
                                 ta_ref, hd // 8)
            k = _rope(normed(p_ref[:, base + width + h * hd:base + width + (h + 1) * hd],
                             g_ref[gk:gk + 1, :]), ta_ref, hd // 8)
            v = p_ref[:, base + 2 * width + h * hd:base + 2 * width + (h + 1) * hd]
            k_ref[:, sl] = k
            v_ref[:, sl] = v
            k_rows_ref[pl.ds(h, tm, stride=heads), :] = k
            v_rows_ref[pl.ds(h, tm, stride=heads), :] = v
    for j in range(IDX_HEADS * IDX_HD // LANES):
        sl = slice(j * LANES, (j + 1) * LANES)
        qi_ref[:, sl] = _rope(p_ref[:, A_QI_OFF + j * LANES:A_QI_OFF + (j + 1) * LANES], ti_ref, IDX_HD // 8)
    x = p_ref[:, A_KI_OFF:A_KI_OFF + LANES]
    is_key = lax.broadcasted_iota(I32, x.shape, 1) < IDX_HD
    ms = jnp.sum(jnp.where(is_key, x * x, 0.0), axis=-1, keepdims=True) * (1.0 / IDX_HD)
    ki = _rope(x * lax.rsqrt(ms + NORM_EPS) * g_ref[4:5, :], ti_ref, IDX_HD // 8)
    kiwi_ref[...] = jnp.where(is_key, ki, x * IDX_W_SCALE)


def prep(p, tab_a, tab_i, gains, tm):
    m = p.shape[0]
    period = tab_a.shape[1] // tm
    row = lambda i: (i, 0)
    tab = pl.BlockSpec((3, tm, LANES), lambda i: (0, i % period, 0))
    widths = (A_WIDTH, A_WIDTH, A_WIDTH, IDX_HEADS * IDX_HD, LANES, B_WIDTH, B_WIDTH, B_WIDTH)
    by_head = ((A_HEADS, A_HD), (A_HEADS, A_HD), (B_HEADS, B_HD), (B_HEADS, B_HD))
    return pl.pallas_call(
        _prep_kernel,
        grid=(m // tm,),
        in_specs=[pl.BlockSpec((tm, P_PACK), row), tab, tab, pl.BlockSpec(gains.shape, lambda i: (0, 0))],
        out_specs=[pl.BlockSpec((tm, w), row) for w in widths]
                  + [pl.BlockSpec((tm * h, hd), row) for h, hd in by_head],
        out_shape=[jax.ShapeDtypeStruct((m, w), F32) for w in widths]
                  + [jax.ShapeDtypeStruct((m * h, hd), F32) for h, hd in by_head],
        compiler_params=_cparams("parallel"),
    )(p, tab_a, tab_i, gains)


def _sortable_key(x):
    bits = pltpu.bitcast(x, I32)
    return jnp.where(bits >= 0, bits, bits ^ jnp.int32(0x7FFFFFFF))


def _topk_mask(score, pos, topk, n_pos_bits):
    rows = score.shape[0]
    key = _sortable_key(score)

    def count(m):
        return jnp.sum(jnp.where(m, 1.0, 0.0), axis=-1, keepdims=True).astype(I32)

    groups = 2 if rows % 16 == 0 else 1
    step = rows // groups
    parts = [key[g * step:(g + 1) * step] for g in range(groups)]

    def value_step(it, tus):
        bit = jnp.left_shift(jnp.int32(1), 31 - it)
        out = []
        for part, tu in zip(parts, tus):
            cand_u = tu | bit
            cnt = count(part >= (cand_u ^ jnp.int32(INT_MIN)))
            out.append(jnp.where(cnt >= topk, cand_u, tu))
        return tuple(out)

    tus = lax.fori_loop(0, 32, value_step, tuple(jnp.zeros((step, 1), I32) for _ in range(groups)))
    thr = jnp.concatenate(tus, axis=0) ^ jnp.int32(INT_MIN)
    gt = key > thr
    eq = key == thr
    need = topk - count(gt)
    excess = count(eq) - need

    def tie_search(_):
        def pos_step(it, x):
            cand = x | jnp.left_shift(jnp.int32(1), n_pos_bits - 1 - it)
            cnt = count(eq & (pos < cand))
            return jnp.where(cnt < need, cand, x)
        return lax.fori_loop(0, n_pos_bits, pos_step, jnp.zeros((rows, 1), I32))

    last = lax.cond(jnp.max(excess) > 0, tie_search,
                    lambda _: jnp.full((rows, 1), 2 ** n_pos_bits, I32), 0)
    return gt | (eq & (pos <= last))


def _masked_attention(q, k, v, mask, scale):
    s = _dot_nt(q, k) * scale
    s = jnp.where(mask, s, NEG_INF)
    m = jnp.max(s, axis=-1, keepdims=True)
    p = jnp.exp(s - m)
    l = jnp.sum(p, axis=-1, keepdims=True)
    return _dot(p.astype(BF16), v) / l


def _dsa_prompt_kernel(q_ref, k_ref, v_ref, qi_ref, kiwi_ref, kiwi_blk_ref, o_ref, k16_ref, v16_ref, ki16_ref,
                       *, topk, bucket):
    tq = q_ref.shape[0]
    s_len = k_ref.shape[0]
    i = pl.program_id(1)

    @pl.when(i == 0)
    def _():
        k16_ref[...] = k_ref[...].astype(BF16)
        v16_ref[...] = v_ref[...].astype(BF16)
        ki16_ref[...] = kiwi_ref[:, 0:IDX_HD].astype(BF16)

    wi = kiwi_blk_ref[:, IDX_HD:IDX_HD + IDX_HEADS]
    qi16 = qi_ref[...].astype(BF16)
    q16 = q_ref[...].astype(BF16)

    def attend(n_keys):
        qpos = i * tq + lax.broadcasted_iota(I32, (tq, n_keys), 0)
        kpos = lax.broadcasted_iota(I32, (tq, n_keys), 1)
        causal = kpos <= qpos
        ki = ki16_ref[0:n_keys, :]
        isc = jnp.zeros((tq, n_keys), F32)
        for h in range(IDX_HEADS):
            s = _dot_nt(qi16[:, h * IDX_HD:(h + 1) * IDX_HD], ki)
            isc += jnp.maximum(s, 0.0) * wi[:, h:h + 1]
        mask = _topk_mask(jnp.where(causal, isc, NEG_INF), kpos, topk, (n_keys - 1).bit_length()) & causal
        for h in range(A_HEADS):
            sl = slice(h * A_HD, (h + 1) * A_HD)
            o = _masked_attention(q16[:, sl], k16_ref[0:n_keys, sl], v16_ref[0:n_keys, sl], mask, A_HD ** -0.5)
            o_ref[:, sl] = o.astype(o_ref.dtype)

    for bkt in range(s_len // bucket):
        pl.when(i // (bucket // tq) == bkt)(functools.partial(attend, (bkt + 1) * bucket))


def dsa_prompt(q, k, v, qi, kiwi):
    b, s, _ = q.shape
    tq = DSA_QUERY_TILE
    topk = min(A_TOPK_MAX, s // 4)
    bucket = min(s, DSA_KEY_BUCKET)
    assert s % bucket == 0 and bucket % tq == 0 and bucket >= topk
    blk = lambda bi, i: (bi, i, 0)
    full = lambda bi, i: (bi, 0, 0)
    return pl.pallas_call(
        functools.partial(_dsa_prompt_kernel, topk=topk, bucket=bucket),
        grid=(b, s // tq),
        in_specs=[pl.BlockSpec((None, tq, A_WIDTH), blk),
                  pl.BlockSpec((None, s, A_WIDTH), full),
                  pl.BlockSpec((None, s, A_WIDTH), full),
                  pl.BlockSpec((None, tq, IDX_HEADS * IDX_HD), blk),
                  pl.BlockSpec((None, s, LANES), full),
                  pl.BlockSpec((None, tq, LANES), blk)],
        out_specs=pl.BlockSpec((None, tq, A_WIDTH), blk),
        out_shape=jax.ShapeDtypeStruct((b, s, A_WIDTH), BF16),
        scratch_shapes=[pltpu.VMEM((s, A_WIDTH), BF16), pltpu.VMEM((s, A_WIDTH), BF16),
                        pltpu.VMEM((s, IDX_HD), BF16)],
        compiler_params=_cparams("parallel", "arbitrary"),
    )(q, k, v, qi, kiwi, kiwi)


def _moba_prompt_kernel(q_ref, k_ref, v_ref, o_ref, kmean_ref, k16_ref, v16_ref, *, ksel):
    tq = q_ref.shape[0]
    s_len, width = k_ref.shape
    nb = s_len // MOBA_BLOCK
    own_id = pl.program_id(1)

    @pl.when(own_id == 0)
    def _():
        k = k_ref[...]
        k16_ref[...] = k.astype(BF16)
        v16_ref[...] = v_ref[...].astype(BF16)
        kmean_ref[...] = jnp.mean(k.reshape(nb, MOBA_BLOCK, width), axis=1)

    q16 = q_ref[...].astype(BF16)
    diagonal = jnp.where(lax.broadcasted_iota(I32, (tq, MOBA_BLOCK), 1)
                         <= lax.broadcasted_iota(I32, (tq, MOBA_BLOCK), 0), 1.0, 0.0)

    def attend(own):
        nidx = lax.broadcasted_iota(I32, (tq, nb), 1)
        for h in range(B_HEADS):
            sl = slice(h * B_HD, (h + 1) * B_HD)
            q = q16[:, sl]
            g = jnp.where(nidx < own, _dot_nt(q, kmean_ref[:, sl].astype(BF16)), NEG_INF)
            rank = jnp.zeros((tq, nb), I32)
            for m in range(own):
                gm = g[:, m:m + 1]
                rank += jnp.where((gm > g) | ((gm == g) & (m < nidx)), 1, 0)
            chosen = jnp.where((nidx < own) & (rank < ksel), 1.0, 0.0)
            mask = jnp.concatenate(
                [jnp.broadcast_to(chosen[:, n:n + 1], (tq, MOBA_BLOCK)) for n in range(own)] + [diagonal], axis=1)
            n_keys = (own + 1) * MOBA_BLOCK
            o = _masked_attention(q, k16_ref[0:n_keys, sl], v16_ref[0:n_keys, sl], mask > 0.5, B_HD ** -0.5)
            o_ref[:, sl] = o.astype(o_ref.dtype)

    for own in range(nb):
        pl.when(own_id == own)(functools.partial(attend, own))


def moba_prompt(q, k, v):
    b, s, _ = q.shape
    nb = s // MOBA_BLOCK
    tq = MOBA_BLOCK
    blk = lambda bi, i: (bi, i, 0)
    full = lambda bi, i: (bi, 0, 0)
    return pl.pallas_call(
        functools.partial(_moba_prompt_kernel, ksel=max(1, min(MOBA_TOPK, nb - 1))),
        grid=(b, nb),
        in_specs=[pl.BlockSpec((None, tq, B_WIDTH), blk),
                  pl.BlockSpec((None, s, B_WIDTH), full),
                  pl.BlockSpec((None, s, B_WIDTH), full)],
        out_specs=pl.BlockSpec((None, tq, B_WIDTH), blk),
        out_shape=jax.ShapeDtypeStruct((b, s, B_WIDTH), BF16),
        scratch_shapes=[pltpu.VMEM((nb, B_WIDTH), F32), pltpu.VMEM((s, B_WIDTH), BF16),
                        pltpu.VMEM((s, B_WIDTH), BF16)],
        compiler_params=_cparams("parallel", "arbitrary"),
    )(q, k, v)


def _split3(x):
    hi = x.astype(BF16)
    r1 = x - hi.astype(F32)
    mid = r1.astype(BF16)
    lo = (r1 - mid.astype(F32)).astype(BF16)
    return hi, mid, lo


def _dot_tn(a, b):
    return lax.dot_general(a, b, (((0,), (0,)), ((), ())), preferred_element_type=F32)


def _wkv_chunk(r, ld, k, v, kkr, a, states, c):
    heads = range(len(states))
    hs = lambda x, h: x[:, h * C_HD:(h + 1) * C_HD]
    row = lax.broadcasted_iota(I32, (c, c), 0)
    col = lax.broadcasted_iota(I32, (c, c), 1)
    strict = row > col
    incl = row >= col
    incl16 = jnp.where(incl, 1.0, 0.0).astype(BF16)
    eye = jnp.where(row == col, 1.0, 0.0)
    hi, mid, lo = _split3(ld)
    cum = _dot(incl16, hi) + _dot(incl16, mid) + _dot(incl16, lo)
    g_inc = jnp.exp(cum)
    g_exc = jnp.exp(cum - ld)
    g_inv = jnp.exp(-cum)
    g_end = g_inc[c - 1:c, :]
    rt = r * g_inc
    kt = k * g_inv
    k_end = kt * g_end

    kk = []
    for h in heads:
        x = hs(kkr, h)
        kk.append(x / jnp.maximum(jnp.sqrt(jnp.sum(x * x, axis=-1, keepdims=True)), 1e-12))
    at16 = [(-kk[h] * hs(g_exc, h)).astype(BF16) for h in heads]
    bt = [kk[h] * hs(a, h) * hs(g_inv, h) for h in heads]
    ar16 = [jnp.concatenate([at16[h], hs(rt, h).astype(BF16)], axis=0) for h in heads]
    bk16 = [jnp.concatenate([bt[h], hs(kt, h)], axis=0).astype(BF16) for h in heads]
    gram = [_dot_nt(ar16[h], bk16[h]) for h in heads]
    l_ab = [jnp.where(strict, gram[h][:c, :c], 0.0) for h in heads]
    l_ak16 = [jnp.where(strict, gram[h][:c, c:], 0.0).astype(BF16) for h in heads]
    col2 = lax.broadcasted_iota(I32, (c, 2 * c), 1)
    incl2 = lax.broadcasted_iota(I32, (c, 2 * c), 0) >= jnp.where(col2 >= c, col2 - c, col2)
    l_r16 = [jnp.where(incl2, gram[h][c:, :], 0.0).astype(BF16) for h in heads]
    l_rb16 = [l_r16[h][:, :c] for h in heads]
    inv = [eye + l_ab[h] for h in heads]
    lp16 = [l_ab[h].astype(BF16) for h in heads]
    n = 2
    while n < c:
        lp16 = [_dot(lp16[h], lp16[h]).astype(BF16) for h in heads]
        inv = [inv[h] + _dot(inv[h].astype(BF16), lp16[h]) for h in heads]
        n *= 2
    inv16 = [inv[h].astype(BF16) for h in heads]
    v16 = [hs(v, h).astype(BF16) for h in heads]
    lakv16 = [_dot(l_ak16[h], v16[h]).astype(BF16) for h in heads]
    w16 = [_dot(inv16[h], at16[h]).astype(BF16) for h in heads]
    u0_16 = [_dot(inv16[h], lakv16[h]).astype(BF16) for h in heads]
    r2_16 = [(hs(rt, h) + _dot(l_rb16[h], w16[h])).astype(BF16) for h in heads]
    uv16 = [jnp.concatenate([u0_16[h], v16[h]], axis=0) for h in heads]
    y0 = [_dot(l_r16[h], uv16[h]) for h in heads]
    bend16 = [(bt[h] * hs(g_end, h)).astype(BF16) for h in heads]
    bkend16 = [jnp.concatenate([bend16[h], hs(k_end, h).astype(BF16)], axis=0) for h in heads]
    m2_16 = [_dot_tn(w16[h], bend16[h]).astype(BF16) for h in heads]
    nmat = [_dot_tn(uv16[h], bkend16[h]) for h in heads]
    s16 = [states[h].astype(BF16) for h in heads]
    ys = [_dot_nt(r2_16[h], s16[h]) + y0[h] for h in heads]
    new_states = [states[h] * hs(g_end, h) + _dot(s16[h], m2_16[h]) + nmat[h] for h in heads]
    return ys, new_states


WKV_VECS = ('c_w0', 'c_a0', 'c_k_k', 'c_k_a', 'c_ln_w', 'c_ln_b', 'c_r_k')


def _wkv_kernel(p_ref, prev_ref, mu_ref, vec_ref, w2_ref, a2_ref, g2_ref, s0_ref,
                o_ref, sfin_ref, state_ref, last_ref, *, chunk, valid_len):
    tb = p_ref.shape[0]
    t_blk = pl.program_id(1)
    hs = lambda x, h: x[:, h * C_HD:(h + 1) * C_HD]
    w0, a0, k_k, k_a, ln_w, ln_b, r_k = (vec_ref[i:i + 1, :] for i in range(len(WKV_VECS)))

    @pl.when(t_blk == 0)
    def _():
        state_ref[...] = s0_ref[...]
        last_ref[...] = prev_ref[...]

    def chunk_step(ci, prev_row):
        start = pl.multiple_of(ci * chunk, chunk)
        rows = pl.ds(start, chunk)
        pc = p_ref[rows, :]
        first = lax.broadcasted_iota(I32, pc.shape, 0) == 0
        shifted = jnp.where(first, prev_row, pltpu.roll(pc, 1, 0))
        xs = pc + mu_ref[...] * (shifted - pc)
        r = xs[:, 0:C_WIDTH]
        k = xs[:, C_WIDTH:2 * C_WIDTH]
        v = xs[:, 2 * C_WIDTH:3 * C_WIDTH]
        lora = xs[:, LORA_OFF:C_BLOCK]
        z = w0 + _dot(jnp.tanh(lora).astype(BF16), w2_ref[...])
        ld = -jnp.exp(jnp.minimum(z, 0.0) - jnp.log1p(jnp.exp(-jnp.abs(z))) - 0.5)
        a = 1.0 / (1.0 + jnp.exp(-(a0 + _dot(lora.astype(BF16), a2_ref[...]))))
        g = _dot((1.0 / (1.0 + jnp.exp(-lora))).astype(BF16), g2_ref[...])
        kkr = k * k_k
        k = k * (1.0 + (a - 1.0) * k_a)
        if valid_len is not None:
            live = (t_blk * tb + start + lax.broadcasted_iota(I32, k.shape, 0)) < valid_len
            ld, kkr, k, v = (jnp.where(live, x, 0.0) for x in (ld, kkr, k, v))
        states = [state_ref[h] for h in range(C_HEADS)]
        ys, new_states = _wkv_chunk(r, ld, k, v, kkr, a, states, chunk)
        for h in range(C_HEADS):
            state_ref[h] = new_states[h]
        rkr = r * k * r_k
        outs = []
        for h in range(C_HEADS):
            y = ys[h]
            mean = jnp.mean(y, axis=-1, keepdims=True)
            var = jnp.mean(jnp.square(y - mean), axis=-1, keepdims=True)
            yn = (y - mean) * lax.rsqrt(var + GN_EPS)
            bonus = jnp.sum(hs(rkr, h), axis=-1, keepdims=True) * hs(v, h)
            outs.append(yn * hs(ln_w, h) + hs(ln_b, h) + bonus)
        o_ref[rows, :] = (jnp.concatenate(outs, axis=1) * g).astype(o_ref.dtype)
        return pc[chunk - 1:chunk, :]

    last_ref[...] = lax.fori_loop(0, tb // chunk, chunk_step, last_ref[...])

    @pl.when(t_blk == pl.num_programs(1) - 1)
    def _():
        sfin_ref[...] = state_ref[...]


def wkv7(p, prev, mu, vecs, w2, a2, g2, s0, chunk, tb, valid_len):
    b, s, _ = p.shape
    const = lambda shape: pl.BlockSpec(shape, lambda bi, i: (0,) * len(shape))
    st = pl.BlockSpec((None, C_HEADS, C_HD, C_HD), lambda bi, i: (bi, 0, 0, 0))
    return pl.pallas_call(
        functools.partial(_wkv_kernel, chunk=chunk, valid_len=valid_len if valid_len < s else None),
        grid=(b, s // tb),
        in_specs=[pl.BlockSpec((None, tb, C_BLOCK), lambda bi, i: (bi, i, 0)),
                  pl.BlockSpec((None, 1, C_BLOCK), lambda bi, i: (bi, 0, 0)),
                  const(mu.shape), const(vecs.shape), const(w2.shape), const(a2.shape), const(g2.shape), st],
        out_specs=[pl.BlockSpec((None, tb, C_WIDTH), lambda bi, i: (bi, i, 0)), st],
        out_shape=[jax.ShapeDtypeStruct((b, s, C_WIDTH), BF16),
                   jax.ShapeDtypeStruct((b, C_HEADS, C_HD, C_HD), F32)],
        scratch_shapes=[pltpu.VMEM((C_HEADS, C_HD, C_HD), F32), pltpu.VMEM((1, C_BLOCK), F32)],
        compiler_params=_cparams("parallel", "arbitrary"),
    )(p, prev, mu, vecs, w2, a2, g2, s0)


def _split2(x):
    hi = x.astype(BF16)
    return hi, (x - hi.astype(F32)).astype(BF16)


def _page_specs(rows, width, layer, n_pages, pg):
    assert n_pages % pg == 0
    last_group = n_pages // pg - 1

    def spec(slot):
        def index_map(b, j, pt):
            return (layer, pt[b, jnp.minimum(j, last_group) * pg + slot], 0, 0)
        return pl.BlockSpec((None, None, rows, width), index_map)
    return [spec(slot) for slot in range(pg)]


def _idx_sample_kernel(pt_ref, qi_ref, w_ref, knew_ref, *rest, t):
    pages, o_ref = rest[:-1], rest[-1]
    j = pl.program_id(1)
    last = pl.num_programs(1) - 1
    q_hi, q_lo = _split2(qi_ref[...])
    w = w_ref[...]

    def scores(kpages):
        splits = [_split2(kp) for kp in kpages]
        s = [_dot(q_hi, k_hi) + _dot(q_hi, k_lo) + _dot(q_lo, k_hi) for k_hi, k_lo in splits]
        s = [jnp.maximum(x, 0.0) * w for x in s]
        return [jnp.sum(x.reshape(t, IDX_HEADS, PAGE_SIZE), axis=1) for x in s]

    @pl.when(j < last)
    def _():
        o_ref[...] = jnp.concatenate(scores([page[...] for page in pages]), axis=1)

    @pl.when(j == last)
    def _():
        o_ref[...] = jnp.zeros(o_ref.shape, F32)
        o_ref[:, 0:PAGE_SIZE] = scores([knew_ref[...]])[0]


def idx_sample(page_table, qi, w, knew, cache_idx, layer):
    db, rows, _ = qi.shape
    t = rows // IDX_HEADS
    n_pages = page_table.shape[1]
    pg = min(IDX_PAGES_PER_STEP, n_pages)
    steps = n_pages // pg + 1
    per_b = lambda b, j, pt: (b, 0, 0)
    return pl.pallas_call(
        functools.partial(_idx_sample_kernel, t=t),
        grid_spec=pltpu.PrefetchScalarGridSpec(
            num_scalar_prefetch=1,
            grid=(db, steps),
            in_specs=[pl.BlockSpec((None, rows, IDX_HD), per_b),
                      pl.BlockSpec((None, rows, 1), per_b),
                      pl.BlockSpec((None, IDX_HD, PAGE_SIZE), per_b)]
                     + _page_specs(IDX_HD, PAGE_SIZE, layer, n_pages, pg),
            out_specs=pl.BlockSpec((None, t, pg * PAGE_SIZE), lambda b, j, pt: (b, 0, j))),
        out_shape=jax.ShapeDtypeStruct((db, t, steps * pg * PAGE_SIZE), F32),
        compiler_params=_cparams("parallel", "arbitrary"),
    )(page_table, qi, w, knew, *([cache_idx] * pg))


def _select_sample_kernel(isc_ref, o_ref, *, topk, t, past_len):
    rows, n = isc_ref.shape
    pos = lax.broadcasted_iota(I32, (rows, n), 1)
    r = pl.program_id(0) * rows + lax.broadcasted_iota(I32, (rows, n), 0)
    causal = pos <= past_len + (r & (t - 1))
    mask = _topk_mask(jnp.where(causal, isc_ref[...], NEG_INF), pos, topk, n.bit_length()) & causal
    o_ref[...] = jnp.where(mask, 1.0, 0.0)


def select_sample(isc, topk, t, past_len):
    rows, n = isc.shape
    tr = 16
    assert t & (t - 1) == 0 and rows % tr == 0
    return pl.pallas_call(
        functools.partial(_select_sample_kernel, topk=topk, t=t, past_len=past_len),
        grid=(rows // tr,),
        in_specs=[pl.BlockSpec((tr, n), lambda i: (i, 0))],
        out_specs=pl.BlockSpec((tr, n), lambda i: (i, 0)),
        out_shape=jax.ShapeDtypeStruct((rows, n), F32),
        compiler_params=_cparams("parallel"),
    )(isc)


def _own_head(shape, heads):
    return ((lax.broadcasted_iota(I32, shape, 1) & (heads - 1))
            == (lax.broadcasted_iota(I32, shape, 0) & (heads - 1)))


def _dsa_sample_kernel(pt_ref, q_ref, mask_ref, knew_ref, vnew_ref, *rest, heads, pg):
    kpages, vpages = rest[:pg], rest[pg:2 * pg]
    o_ref, m_ref, l_ref, acc_ref = rest[2 * pg:]
    j = pl.program_id(1)
    last = pl.num_programs(1) - 1
    q = q_ref[...]
    page_rows = PAGE_SIZE * heads
    shift = heads.bit_length() - 1
    spread = jnp.where(
        jnp.right_shift(lax.broadcasted_iota(I32, (PAGE_SIZE, page_rows), 1), shift)
        == lax.broadcasted_iota(I32, (PAGE_SIZE, page_rows), 0), 1.0, 0.0).astype(BF16)

    @pl.when(j == 0)
    def _():
        m_ref[...] = jnp.full(m_ref.shape, NEG_INF, F32)
        l_ref[...] = jnp.zeros(l_ref.shape, F32)
        acc_ref[...] = jnp.zeros(acc_ref.shape, F32)

    def update(ks, vs, mask):
        s = jnp.concatenate([_dot_nt(q, kp.astype(BF16)) for kp in ks], axis=1) * (A_HD ** -0.5)
        sel = jnp.concatenate([_dot(mask[:, i * PAGE_SIZE:(i + 1) * PAGE_SIZE].astype(BF16), spread)
                               for i in range(len(ks))], axis=1)
        s = jnp.where((sel > 0.5) & _own_head(s.shape, heads), s, NEG_INF)
        m_old = m_ref[...]
        m_new = jnp.maximum(m_old, jnp.max(s, axis=-1, keepdims=True))
        m_safe = jnp.where(m_new == NEG_INF, 0.0, m_new)
        p = jnp.exp(s - m_safe)
        alpha = jnp.exp(m_old - m_safe)
        l_ref[...] = alpha * l_ref[...] + jnp.sum(p, axis=-1, keepdims=True)
        acc = alpha * acc_ref[...]
        for i, vp in enumerate(vs):
            acc += _dot(p[:, i * page_rows:(i + 1) * page_rows].astype(BF16), vp.astype(BF16))
        acc_ref[...] = acc
        m_ref[...] = m_new

    @pl.when(j < last)
    def _():
        update([kp[...] for kp in kpages], [vp[...] for vp in vpages], mask_ref[...])

    @pl.when(j == last)
    def _():
        update([knew_ref[...]], [vnew_ref[...]], mask_ref[:, 0:PAGE_SIZE])
        o_ref[...] = acc_ref[...] / l_ref[...]


def dsa_sample(page_table, q, mask, knew, vnew, cache_k, cache_v, layer, heads):
    db, rows, hd = q.shape
    n_pages = page_table.shape[1]
    pg = min(SAMPLE_PAGES_PER_STEP, n_pages)
    steps = n_pages // pg + 1
    page_rows = PAGE_SIZE * heads
    per_b = lambda b, j, pt: (b, 0, 0)
    return pl.pallas_call(
        functools.partial(_dsa_sample_kernel, heads=heads, pg=pg),
        grid_spec=pltpu.PrefetchScalarGridSpec(
            num_scalar_prefetch=1,
            grid=(db, steps),
            in_specs=[pl.BlockSpec((None, rows, hd), per_b),
                      pl.BlockSpec((None, rows, pg * PAGE_SIZE), lambda b, j, pt: (b, 0, j)),
                      pl.BlockSpec((None, page_rows, hd), per_b),
                      pl.BlockSpec((None, page_rows, hd), per_b)]
                     + _page_specs(page_rows, hd, layer, n_pages, pg) + _page_specs(page_rows, hd, layer, n_pages, pg),
            out_specs=pl.BlockSpec((None, rows, hd), per_b),
            scratch_shapes=[pltpu.VMEM((rows, 1), F32), pltpu.VMEM((rows, 1), F32),
                            pltpu.VMEM((rows, hd), F32)]),
        out_shape=jax.ShapeDtypeStruct((db, rows, hd), F32),
        compiler_params=_cparams("parallel", "arbitrary"),
    )(page_table, q, mask, knew, vnew, *([cache_k] * pg), *([cache_v] * pg))


def _moba_sample_kernel(pt_ref, q_ref, knew_ref, vnew_ref, *rest, heads, ksel, pg):
    ppb = MOBA_BLOCK // PAGE_SIZE
    kpages, vpages = rest[:pg], rest[pg:2 * pg]
    o_ref, g_ref, m_ref, l_ref, ob_ref = rest[2 * pg:]
    j = pl.program_id(1)
    last = pl.num_programs(1) - 1
    q = q_ref[...]
    q16 = q.astype(BF16)
    rows, hd = q.shape
    scale = B_HD ** -0.5
    sublanes = 8
    assert 2 * heads == sublanes and hd == LANES and rows % sublanes == 0
    wide = lambda x: jnp.broadcast_to(x, (rows, LANES))

    def partial_softmax(s, v16):
        m = jnp.max(s, axis=-1, keepdims=True)
        p = jnp.exp(s - m)
        return m, jnp.sum(p, axis=-1, keepdims=True), _dot(p.astype(BF16), v16)

    @pl.when(j < last)
    def _():
        blocks = range(pg // ppb)
        kblk = [jnp.concatenate([kpages[b * ppb + i][...] for i in range(ppb)], axis=0) for b in blocks]
        v16 = [jnp.concatenate([vpages[b * ppb + i][...] for i in range(ppb)], axis=0).astype(BF16)
               for b in blocks]
        s = [_dot_nt(q16, kblk[b].astype(BF16)) * scale for b in blocks]
        own = _own_head(s[0].shape, heads)
        s = [jnp.where(own, s[b], NEG_INF) for b in blocks]
        m = [jnp.max(s[b], axis=-1, keepdims=True) for b in blocks]
        p = [jnp.exp(s[b] - m[b]) for b in blocks]
        l = [jnp.sum(p[b], axis=-1, keepdims=True) for b in blocks]
        o = [_dot(p[b].astype(BF16), v16[b]) for b in blocks]
        fold = [jnp.sum(kblk[b].reshape(kblk[b].shape[0] // sublanes, sublanes, hd), axis=0) for b in blocks]
        kmean = [(fold[b] + pltpu.roll(fold[b], heads, 0)) * (1.0 / MOBA_BLOCK) for b in blocks]
        gate = [jnp.sum(q * jnp.concatenate([kmean[b]] * (rows // sublanes), axis=0), axis=-1, keepdims=True)
                for b in blocks]
        for b in blocks:
            n = j * (pg // ppb) + b
            g_ref[n] = wide(gate[b])
            m_ref[n] = wide(m[b])
            l_ref[n] = wide(l[b])
            ob_ref[n] = o[b]

    @pl.when(j == last)
    def _():
        shift = heads.bit_length() - 1
        s = _dot_nt(q16, knew_ref[...].astype(BF16)) * scale
        causal = (jnp.right_shift(lax.broadcasted_iota(I32, s.shape, 1), shift)
                  <= jnp.right_shift(lax.broadcasted_iota(I32, s.shape, 0), shift))
        s = jnp.where(causal & _own_head(s.shape, heads), s, NEG_INF)
        m_own, l_own, o_own = partial_softmax(s, vnew_ref[...].astype(BF16))
        m_own, l_own = wide(m_own), wide(l_own)
        gates = g_ref[...]
        nidx = lax.broadcasted_iota(I32, gates.shape, 0)
        chosen = jnp.zeros(gates.shape, F32)
        for _ in range(ksel):
            best = jnp.max(gates, axis=0, keepdims=True)
            first = jnp.min(jnp.where(gates == best, nidx, gates.shape[0]), axis=0, keepdims=True)
            pick = nidx == first
            chosen = jnp.where(pick, 1.0, chosen)
            gates = jnp.where(pick, NEG_INF, gates)
        m_blk = m_ref[...]
        m_tot = jnp.maximum(m_own, jnp.max(jnp.where(chosen > 0.5, m_blk, NEG_INF), axis=0))
        wgt = jnp.where(chosen > 0.5, jnp.exp(m_blk - m_tot[None]), 0.0)
        w_own = jnp.exp(m_own - m_tot)
        l_tot = w_own * l_own + jnp.sum(wgt * l_ref[...], axis=0)
        acc = w_own * o_own
        for n in range(gates.shape[0]):
            acc += wgt[n] * ob_ref[n]
        o_ref[...] = acc / l_tot


def moba_sample(page_table, q, knew, vnew, cache_k, cache_v, layer, heads):
    db, rows, hd = q.shape
    n_pages = page_table.shape[1]
    pg = min(SAMPLE_PAGES_PER_STEP, n_pages)
    steps = n_pages // pg + 1
    nbp = n_pages * PAGE_SIZE // MOBA_BLOCK
    page_rows = PAGE_SIZE * heads
    per_b = lambda b, j, pt: (b, 0, 0)
    return pl.pallas_call(
        functools.partial(_moba_sample_kernel, heads=heads, ksel=max(1, min(MOBA_TOPK, nbp)), pg=pg),
        grid_spec=pltpu.PrefetchScalarGridSpec(
            num_scalar_prefetch=1,
            grid=(db, steps),
            in_specs=[pl.BlockSpec((None, rows, hd), per_b),
                      pl.BlockSpec((None, page_rows, hd), per_b),
                      pl.BlockSpec((None, page_rows, hd), per_b)]
                     + _page_specs(page_rows, hd, layer, n_pages, pg) + _page_specs(page_rows, hd, layer, n_pages, pg),
            out_specs=pl.BlockSpec((None, rows, hd), per_b),
            scratch_shapes=[pltpu.VMEM((nbp, rows, LANES), F32), pltpu.VMEM((nbp, rows, LANES), F32),
                            pltpu.VMEM((nbp, rows, LANES), F32), pltpu.VMEM((nbp, rows, hd), F32)]),
        out_shape=jax.ShapeDtypeStruct((db, rows, hd), F32),
        compiler_params=_cparams("parallel", "arbitrary"),
    )(page_table, q, knew, vnew, *([cache_k] * pg), *([cache_v] * pg))


def _new_page(x, rows):
    return jnp.pad(x, ((0, 0), (0, rows - x.shape[1]), (0, 0)))


def _sample_attention(qa, ka, va, qi, kiwi, qb, kb, vb, caches, page_table, layer, db, t):
    cache_a_k, cache_a_v, cache_a_idx, cache_b_k, cache_b_v = caches
    past_len = page_table.shape[1] * PAGE_SIZE
    topk = min(A_TOPK_MAX, (past_len + t) // 4)
    ki = kiwi[:, :IDX_HD].reshape(db, t, IDX_HD)
    wi = kiwi[:, IDX_HD:IDX_HD + IDX_HEADS]
    knew = _new_page(ki, PAGE_SIZE).transpose(0, 2, 1)
    isc = idx_sample(page_table, qi.reshape(db, t * IDX_HEADS, IDX_HD), wi.reshape(db, t * IDX_HEADS, 1),
                     knew, cache_a_idx, layer)
    n = isc.shape[-1]
    mask = select_sample(isc.reshape(db * t, n), topk, t, past_len).reshape(db, t, n)
    rows_a = lambda z: z.reshape(db, t * A_HEADS, A_HD)
    rows_b = lambda z: z.reshape(db, t * B_HEADS, B_HD)
    oa = dsa_sample(page_table, rows_a(qa).astype(BF16), jnp.repeat(mask, A_HEADS, axis=1),
                    _new_page(rows_a(ka), PAGE_SIZE * A_HEADS), _new_page(rows_a(va), PAGE_SIZE * A_HEADS),
                    cache_a_k, cache_a_v, layer, A_HEADS)
    ob = moba_sample(page_table, rows_b(qb), _new_page(rows_b(kb), PAGE_SIZE * B_HEADS),
                     _new_page(rows_b(vb), PAGE_SIZE * B_HEADS), cache_b_k, cache_b_v, layer, B_HEADS)
    return oa.reshape(db * t, A_WIDTH), ob.reshape(db * t, B_WIDTH)


def _pad_cols(x, width):
    return jnp.pad(x, [(0, 0)] * (x.ndim - 1) + [(0, width - x.shape[-1])])


def _layer_params(l, norm1_g, w_in, a_q_norm, a_k_norm, idx_k_norm, b_q_norm, b_k_norm, c_mu, c_w0, c_w2, c_a0,
                  c_a2, c_g2, c_k_k, c_k_a, c_r_k, c_ln_w, c_ln_b, w_o, norm2_g, w_ff1, w_ff2):
    w = w_in[l]
    a_cols, b_cols, c_cols = w[:, :A_COLS], w[:, A_COLS:A_COLS + B_COLS], w[:, A_COLS + B_COLS:]
    w_pack = jnp.concatenate([_pad_cols(c_cols, A_OFF), _pad_cols(a_cols, B_OFF - A_OFF), b_cols],
                             axis=1).astype(BF16)
    lora_rows = lambda x, off: jnp.pad(x, ((off, LORA_BLOCK - off - x.shape[0]), (0, 0))).astype(BF16)
    vec = dict(c_w0=c_w0[l], c_a0=c_a0[l], c_k_k=c_k_k[l], c_k_a=c_k_a[l], c_ln_w=c_ln_w[l], c_ln_b=c_ln_b[l],
               c_r_k=c_r_k[l].reshape(-1))
    vecs = jnp.stack([vec[name] for name in WKV_VECS] + [jnp.zeros((C_WIDTH,), F32)])
    gains = jnp.stack([a_q_norm[l], a_k_norm[l], b_q_norm[l], b_k_norm[l], _pad_cols(idx_k_norm[l], LANES)]
                      + [jnp.zeros((LANES,), F32)] * 3)
    return dict(norm1_g=norm1_g[l], w_in=w_pack, gains=gains, mu=_pad_cols(c_mu[l], C_BLOCK)[None], vecs=vecs,
                w2=lora_rows(c_w2[l], 0), a2=lora_rows(c_a2[l], W_LORA), g2=lora_rows(c_g2[l], W_LORA + A_LORA),
                w_o=w_o[l].astype(BF16), norm2_g=norm2_g[l], w_ff1=w_ff1[l].astype(BF16),
                w_ff2=w_ff2[l].astype(BF16))


def _layer(x, tabs, prev_shift, wkv0, lw, sample_ctx):
    b, s, d = x.shape
    m = b * s
    p = norm_matmul(x.reshape(m, d), lw['norm1_g'], lw['w_in'], min(m, 1024), PROJ_TILE_N)
    qa, ka, va, qi, kiwi, qb, kb, vb, ka_rows, va_rows, kb_rows, vb_rows = prep(
        p, tabs[0], tabs[1], lw['gains'], min(m, 256))
    p3 = p.reshape(b, s, P_PACK)
    if sample_ctx is None:
        seq = lambda z: z.reshape(b, s, z.shape[-1])
        oa = dsa_prompt(seq(qa), seq(ka), seq(va), seq(qi), seq(kiwi)).reshape(m, A_WIDTH)
        ob = moba_prompt(seq(qb), seq(kb), seq(vb)).reshape(m, B_WIDTH)
        p_scan = p3
    else:
        caches, page_table, layer = sample_ctx
        oa, ob = _sample_attention(qa, ka_rows, va_rows, qi, kiwi, qb, kb_rows, vb_rows, caches, page_table,
                                   layer, b, s)
        oa, ob = oa.astype(BF16), ob.astype(BF16)
        p_scan = jnp.pad(p3, ((0, 0), (0, -s % 16), (0, 0)))
    sp = p_scan.shape[1]
    chunk = min(WKV_CHUNK, sp)
    oc, wkv_fin = wkv7(p_scan, _pad_cols(prev_shift, C_BLOCK)[:, None, :], lw['mu'], lw['vecs'], lw['w2'],
                       lw['a2'], lw['g2'], wkv0, chunk, min(sp, 4 * chunk), s)
    tm2 = min(m, 512)
    x2 = out_proj(x.reshape(m, d), oa, ob, oc[:, :s].reshape(m, C_WIDTH), lw['w_o'], tm2)
    x3 = mlp(x2, lw['norm2_g'], lw['w_ff1'], lw['w_ff2'], tm2, min(MLP_TILE_F, lw['w_ff1'].shape[1])).reshape(b, s, d)
    rows = (ka_rows.reshape(b, s, A_HEADS, A_HD), va_rows.reshape(b, s, A_HEADS, A_HD),
            kiwi[:, :IDX_HD].reshape(b, s, IDX_HD), kb_rows.reshape(b, s, B_HEADS, B_HD),
            vb_rows.reshape(b, s, B_HEADS, B_HD), wkv_fin, p3[:, s - 1, :C_COLS])
    return x3, rows


def kernel(x_prompt, x_sample, cache_a_k, cache_a_v, cache_a_idx, cache_b_k, cache_b_v, state_wkv, state_shift, page_table, norm1_g, w_in, a_q_norm, a_k_norm, idx_k_norm, b_q_norm, b_k_norm, c_mu, c_w0, c_w2, c_a0, c_a2, c_g2, c_k_k, c_k_a, c_r_k, c_ln_w, c_ln_b, w_o, norm2_g, w_ff1, w_ff2):
    depth = w_in.shape[0]
    bp, sp = x_prompt.shape[:2]
    db, t = x_sample.shape[:2]
    pos_p = jnp.arange(sp)
    pos_s = jnp.tile(page_table.shape[1] * PAGE_SIZE + jnp.arange(t), db)
    tabs_p = (_rope_tables(pos_p, A_HD, 1), _rope_tables(pos_p, IDX_HD, LANES // IDX_HD))
    tabs_s = (_rope_tables(pos_s, A_HD, 1), _rope_tables(pos_s, IDX_HD, LANES // IDX_HD))
    shift0 = jnp.zeros((bp, C_COLS), x_prompt.dtype)
    wkv0 = jnp.zeros((bp, C_HEADS, C_HD, C_HD), F32)
    xp, xs = x_prompt, x_sample
    by_row = lambda c: c.reshape(c.shape[0], c.shape[1], -1, c.shape[-1])
    caches = (by_row(cache_a_k), by_row(cache_a_v), cache_a_idx.transpose(0, 1, 3, 2),
              by_row(cache_b_k), by_row(cache_b_v))
    p_rows, s_rows = [], []
    for l in range(depth):
        lw = _layer_params(l, norm1_g, w_in, a_q_norm, a_k_norm, idx_k_norm, b_q_norm, b_k_norm, c_mu, c_w0, c_w2,
                           c_a0, c_a2, c_g2, c_k_k, c_k_a, c_r_k, c_ln_w, c_ln_b, w_o, norm2_g, w_ff1, w_ff2)
        xp, rows = _layer(xp, tabs_p, shift0, wkv0, lw, None)
        p_rows.append(rows)
        xs, rows = _layer(xs, tabs_s, state_shift[l], state_wkv[l], lw, (caches, page_table, l))
        s_rows.append(rows)
    p_out = [jnp.stack(z) for z in zip(*p_rows)]
    s_out = [jnp.stack(z) for z in zip(*s_rows)]
    return (xp, xs, *p_out, *s_out)
```
